```python
import math
import jax, jax.numpy as jnp
from jax import lax
import numpy as np

D_MODEL = 1024
BATCH = 2
SEQ = 8192
DEPTH = 1
DEC_BATCH = 32
DEC_SEQ = 8
PAST_LEN = 8192
PAGE_SIZE = 128

HEAD_DIM = 64
ATT_GROUPS = ((128, 1), (512, 4), (2048, 16))
ATT_HEADS_PER_GROUP = 4
N_ATT_HEADS = ATT_HEADS_PER_GROUP * len(ATT_GROUPS)
ATT_WIDTH = N_ATT_HEADS * HEAD_DIM
Q_BLOCK = 128
ROPE_THETA = 10000.0
MLSTM_HEADS = 4
MLSTM_HEAD_DIM = 64
MLSTM_WIDTH = MLSTM_HEADS * MLSTM_HEAD_DIM
MLSTM_CHUNK = 64
CONV_WIDTH = 4
MIX_WIDTH = ATT_WIDTH + MLSTM_WIDTH
IN_SPLITS = (ATT_WIDTH, 2 * ATT_WIDTH, 3 * ATT_WIDTH, 3 * ATT_WIDTH + MLSTM_WIDTH, 3 * ATT_WIDTH + 2 * MLSTM_WIDTH)
IN_WIDTH = 3 * ATT_WIDTH + 3 * MLSTM_WIDTH
N_MEM = 256
MEM_HEADS = 4
MEM_HEAD_DIM = D_MODEL // MEM_HEADS
N_EXPERTS = 64
TOP_K = 6
N_EXPERT_GROUPS = 8
TOPK_GROUPS = 4
EXPERT_FF = 256
SHARED_FF = 256
ROUTED_SCALE = 2.5
MOE_BLOCK = 1024
LN_EPS = 1e-5
DEEPNORM_ALPHA = (2 * DEPTH) ** 0.25
DEEPNORM_BETA = (8 * DEPTH) ** -0.25

kernel_name = 'dilated_mlstm_hymba_moe_decode_step'

F32 = jnp.float32


def layer_norm(x, g, b):
    xf = x.astype(F32)
    mu = xf.mean(-1, keepdims=True)
    var = jnp.square(xf - mu).mean(-1, keepdims=True)
    return ((xf - mu) * lax.rsqrt(var + LN_EPS) * g + b).astype(x.dtype)


def rope(x, pos):
    half = x.shape[-1] // 2
    inv_freq = ROPE_THETA ** (-jnp.arange(half, dtype=F32) / half)
    ang = pos.astype(F32)[:, None] * inv_freq[None, :]
    cos = jnp.cos(ang)[None, :, None, :]
    sin = jnp.sin(ang)[None, :, None, :]
    xf = x.astype(F32)
    x1, x2 = xf[..., :half], xf[..., half:]
    return jnp.concatenate([x1 * cos - x2 * sin, x2 * cos + x1 * sin], axis=-1).astype(x.dtype)


def dilated_window_attention(q, k_seq, v_seq, window, dilation):
    B, T, H, D = q.shape
    start = k_seq.shape[1] - T
    n_keys = window // dilation + 1
    qb = math.gcd(T, Q_BLOCK)
    dist = jnp.arange(n_keys) * dilation
    scale = D ** -0.5

    def block(i):
        q_blk = lax.dynamic_slice_in_dim(q, i * qb, qb, axis=1)
        q_idx = start + i * qb + jnp.arange(qb)
        k_idx = q_idx[:, None] - dist[None, :]
        valid = k_idx >= 0
        k_idx = jnp.maximum(k_idx, 0)
        kg = k_seq[:, k_idx]
        vg = v_seq[:, k_idx]
        s = jnp.einsum('bqhd,bqkhd->bqhk', q_blk, kg).astype(F32) * scale
        s = jnp.where(valid[None, :, None, :], s, -jnp.inf)
        lse = jax.nn.logsumexp(s, axis=-1)
        p = jnp.exp(s - lse[..., None]).astype(vg.dtype)
        o = jnp.einsum('bqhk,bqkhd->bqhd', p, vg)
        return o, lse

    o, lse = lax.map(block, jnp.arange(T // qb))
    o = jnp.moveaxis(o, 0, 1).reshape(B, T, H, D)
    lse = jnp.moveaxis(lse, 0, 1).reshape(B, T, H)
    return o, lse


def mlstm_chunkwise(q, k, v, i_pre, f_pre, C0, n0, m0):
    B, T, H, D = q.shape
    L = math.gcd(T, MLSTM_CHUNK)
    nc = T // L

    def chunks(a):
        a = a.astype(F32)
        return jnp.moveaxis(a.reshape((B, nc, L) + a.shape[2:]), 1, 0)

    qs = chunks(q)
    ks = chunks(k.astype(F32) * (D ** -0.5))
    vs = chunks(v)
    igs = chunks(i_pre)
    lfs = chunks(jax.nn.log_sigmoid(f_pre.astype(F32)))
    causal = jnp.tril(jnp.ones((L, L), dtype=bool))

    def step(carry, inp):
        C, n, m = carry
        qc, kc, vc, ic, lfc = inp
        b = jnp.cumsum(lfc, axis=1).transpose(0, 2, 1)
        ih = ic.transpose(0, 2, 1)
        d_intra = jnp.where(causal, b[:, :, :, None] - b[:, :, None, :] + ih[:, :, None, :], -jnp.inf)
        a_inter = b + m[:, :, None]
        m_t = jnp.maximum(a_inter, d_intra.max(-1))
        s = jnp.einsum('bthd,bshd->bhts', qc, kc) * jnp.exp(d_intra - m_t[..., None])
        w_inter = jnp.exp(a_inter - m_t)
        num = jnp.einsum('bhts,bshd->bhtd', s, vc) + w_inter[..., None] * jnp.einsum('bhvd,bthd->bhtv', C, qc)
        den = s.sum(-1) + w_inter * jnp.einsum('bhd,bthd->bht', n, qc)
        h = num / jnp.maximum(jnp.abs(den), jnp.exp(-m_t))[..., None]
        g = b[:, :, -1]
        a_end = g + m
        d_end = g[:, :, None] - b + ih
        m_new = jnp.maximum(a_end, d_end.max(-1))
        w_s = jnp.exp(d_end - m_new[..., None])
        decay = jnp.exp(a_end - m_new)
        C_new = decay[..., None, None] * C + jnp.einsum('bhs,bshv,bshd->bhvd', w_s, vc, kc)
        n_new = decay[..., None] * n + jnp.einsum('bhs,bshd->bhd', w_s, kc)
        return (C_new, n_new, m_new), h.transpose(0, 2, 1, 3)

    (C, n, m), hs = lax.scan(step, (C0.astype(F32), n0.astype(F32), m0.astype(F32)), (qs, ks, vs, igs, lfs))
    h = jnp.moveaxis(hs, 0, 1).reshape(B, T, H, D)
    return h, C, n, m


def parallel_mixer(x, pos, win_bufs, conv_buf, C0, n0, m0, p):
    B, T, _ = x.shape
    proj = x @ p['w_in']
    q_a, k_a, v_a, c_in, v_m, z = jnp.split(proj, IN_SPLITS, axis=-1)
    q_a = rope(q_a.reshape(B, T, N_ATT_HEADS, HEAD_DIM), pos)
    k_a = rope(k_a.reshape(B, T, N_ATT_HEADS, HEAD_DIM), pos)
    v_a = v_a.reshape(B, T, N_ATT_HEADS, HEAD_DIM)
    outs, lses, new_bufs = [], [], []
    for g, (window, dilation) in enumerate(ATT_GROUPS):
        hs = slice(g * ATT_HEADS_PER_GROUP, (g + 1) * ATT_HEADS_PER_GROUP)
        buf = win_bufs[g]
        k_seq = jnp.concatenate([buf[:, :, 0].astype(k_a.dtype), k_a[:, :, hs]], axis=1)
        v_seq = jnp.concatenate([buf[:, :, 1].astype(v_a.dtype), v_a[:, :, hs]], axis=1)
        o, lse = dilated_window_attention(q_a[:, :, hs], k_seq, v_seq, window, dilation)
        outs.append(o)
        lses.append(lse)
        keep = min(window, k_seq.shape[1])
        new_bufs.append(jnp.stack([k_seq[:, -keep:], v_seq[:, -keep:]], axis=2))
    alpha = jax.nn.softmax(jnp.stack(lses), axis=0)
    att = jnp.concatenate([o * a[..., None].astype(o.dtype) for o, a in zip(outs, alpha)], axis=2)
    att = att.reshape(B, T, ATT_WIDTH)
    cpad = jnp.concatenate([conv_buf.astype(c_in.dtype), c_in], axis=1)
    taps = jnp.stack([cpad[:, j:j + T] for j in range(CONV_WIDTH)], axis=2)
    xc = jax.nn.silu(jnp.einsum('btjc,jc->btc', taps, p['conv_w']) + p['conv_b'])
    new_conv = cpad[:, -(CONV_WIDTH - 1):]
    xch = xc.reshape(B, T, MLSTM_HEADS, MLSTM_HEAD_DIM)
    q_m = jnp.einsum('bthd,hde->bthe', xch, p['wq_m'])
    k_m = jnp.einsum('bthd,hde->bthe', xch, p['wk_m'])
    v_mh = v_m.reshape(B, T, MLSTM_HEADS, MLSTM_HEAD_DIM)
    gate_in = jnp.concatenate([q_m.reshape(B, T, MLSTM_WIDTH), k_m.reshape(B, T, MLSTM_WIDTH), v_m], axis=-1)
    gate_pre = gate_in @ p['w_if'] + p['b_if']
    i_pre, f_pre = gate_pre[..., :MLSTM_HEADS], gate_pre[..., MLSTM_HEADS:]
    h, C, n, m = mlstm_chunkwise(q_m, k_m, v_mh, i_pre, f_pre, C0, n0, m0)
    mu = h.mean(-1, keepdims=True)
    var = jnp.square(h - mu).mean(-1, keepdims=True)
    hn = ((h - mu) * lax.rsqrt(var + LN_EPS)).reshape(B, T, MLSTM_WIDTH) * p['mh_norm_w']
    hm = ((hn + p['skip_m'] * xc.astype(F32)) * jax.nn.silu(z.astype(F32))).astype(x.dtype)
    y = jnp.concatenate([att, hm], axis=-1) @ p['w_out']
    return y, tuple(new_bufs), new_conv, C, n, m


def memory_kv(mem, w_ck, w_cv):
    B, M, _ = mem.shape
    k = (mem @ w_ck).reshape(B, M, MEM_HEADS, MEM_HEAD_DIM)
    v = (mem @ w_cv).reshape(B, M, MEM_HEADS, MEM_HEAD_DIM)
    return jnp.stack([k, v], axis=2)


def memory_cross_attention(x, mem_kv, w_cq, w_co):
    B, T, _ = x.shape
    q = (x @ w_cq).reshape(B, T, MEM_HEADS, MEM_HEAD_DIM)
    s = jnp.einsum('bthd,bmhd->bhtm', q, mem_kv[:, :, 0].astype(q.dtype)).astype(F32) * (MEM_HEAD_DIM ** -0.5)
    pr = jax.nn.softmax(s, axis=-1).astype(x.dtype)
    o = jnp.einsum('bhtm,bmhd->bthd', pr, mem_kv[:, :, 1].astype(x.dtype)).reshape(B, T, D_MODEL)
    return o @ w_co


def moe_ffn(x, w_router, b_router, w_gate, w_up, w_down, ws_gate, ws_up, ws_down):
    B, T, D = x.shape
    N = B * T
    xt = x.reshape(N, D)
    scores = jax.nn.sigmoid((xt @ w_router).astype(F32))
    biased = scores + b_router.astype(F32)
    per_group = N_EXPERTS // N_EXPERT_GROUPS
    group_score = lax.top_k(biased.reshape(N, N_EXPERT_GROUPS, per_group), 2)[0].sum(-1)
    _, g_idx = lax.top_k(group_score, TOPK_GROUPS)
    g_mask = jax.nn.one_hot(g_idx, N_EXPERT_GROUPS, dtype=F32).sum(1)
    e_mask = jnp.repeat(g_mask, per_group, axis=1) > 0
    _, e_idx = lax.top_k(jnp.where(e_mask, biased, -jnp.inf), TOP_K)
    w_sel = jnp.take_along_axis(scores, e_idx, axis=1)
    w_sel = w_sel / w_sel.sum(-1, keepdims=True) * ROUTED_SCALE
    gates = jnp.einsum('nk,nke->ne', w_sel, jax.nn.one_hot(e_idx, N_EXPERTS, dtype=F32))
    blk = math.gcd(N, MOE_BLOCK)

    def expert_block(args):
        xb, gb = args
        hg = jnp.einsum('nd,edf->nef', xb, w_gate)
        hu = jnp.einsum('nd,edf->nef', xb, w_up)
        hh = jax.nn.silu(hg) * hu * gb[..., None].astype(hu.dtype)
        return jnp.einsum('nef,efd->nd', hh, w_down)

    routed = lax.map(expert_block, (xt.reshape(N // blk, blk, D), gates.reshape(N // blk, blk, N_EXPERTS)))
    routed = routed.reshape(N, D)
    shared = (jax.nn.silu(xt @ ws_gate) * (xt @ ws_up)) @ ws_down
    return (routed + shared).reshape(B, T, D)


def layer_forward(x, pos, win_bufs, conv_buf, C0, n0, m0, mem_kv, p):
    a = DEEPNORM_ALPHA
    mix, new_bufs, conv, C, n, m = parallel_mixer(x, pos, win_bufs, conv_buf, C0, n0, m0, p)
    x = layer_norm(a * x + mix, p['ln1_g'], p['ln1_b'])
    x = layer_norm(a * x + memory_cross_attention(x, mem_kv, p['w_cq'], p['w_co']), p['ln2_g'], p['ln2_b'])
    ff = moe_ffn(x, p['w_router'], p['b_router'], p['w_gate'], p['w_up'], p['w_down'], p['ws_gate'], p['ws_up'], p['ws_down'])
    x = layer_norm(a * x + ff, p['ln3_g'], p['ln3_b'])
    return x, new_bufs + (conv, C, n, m)


def setup_inputs(seed: int = 0) -> dict:
    key = jax.random.key(seed)
    ks = list(jax.random.split(key, 48))

    def nrm(shape, scale):
        return jax.random.normal(ks.pop(), shape, F32) * scale

    beta = DEEPNORM_BETA
    col_scale = jnp.concatenate([jnp.ones((2 * ATT_WIDTH,), F32), jnp.full((ATT_WIDTH,), beta, F32),
                                 jnp.ones((MLSTM_WIDTH,), F32), jnp.full((MLSTM_WIDTH,), beta, F32),
                                 jnp.ones((MLSTM_WIDTH,), F32)])
    b_i = nrm((DEPTH, MLSTM_HEADS), 0.1)
    b_f = jnp.linspace(3.0, 6.0, MLSTM_HEADS, dtype=F32)[None, :] + nrm((DEPTH, MLSTM_HEADS), 0.01)

    def win(w):
        return nrm((DEPTH, DEC_BATCH, min(w, PAST_LEN), 2, ATT_HEADS_PER_GROUP, HEAD_DIM), 1.0)

    return {
        'x_prompt': nrm((BATCH, SEQ, D_MODEL), 1.0),
        'x_sample': nrm((DEC_BATCH, DEC_SEQ, D_MODEL), 1.0),
        'mem_prompt': nrm((BATCH, N_MEM, D_MODEL), 1.0),
        'cache_win128': win(128),
        'cache_win512': win(512),
        'cache_win2048': win(2048),
        'cache_mem_kv': nrm((DEPTH, DEC_BATCH, N_MEM, 2, MEM_HEADS, MEM_HEAD_DIM), 1.0),
        'state_conv': nrm((DEPTH, DEC_BATCH, CONV_WIDTH - 1, MLSTM_WIDTH), 1.0),
        'state_C': nrm((DEPTH, DEC_BATCH, MLSTM_HEADS, MLSTM_HEAD_DIM, MLSTM_HEAD_DIM), 0.1),
        'state_n': nrm((DEPTH, DEC_BATCH, MLSTM_HEADS, MLSTM_HEAD_DIM), 0.5),
        'state_m': nrm((DEPTH, DEC_BATCH, MLSTM_HEADS), 1.0),
        'w_in': nrm((DEPTH, D_MODEL, IN_WIDTH), D_MODEL ** -0.5) * col_scale,
        'conv_w': nrm((DEPTH, CONV_WIDTH, MLSTM_WIDTH), CONV_WIDTH ** -0.5),
        'conv_b': nrm((DEPTH, MLSTM_WIDTH), 0.01),
        'wq_m': nrm((DEPTH, MLSTM_HEADS, MLSTM_HEAD_DIM, MLSTM_HEAD_DIM), MLSTM_HEAD_DIM ** -0.5),
        'wk_m': nrm((DEPTH, MLSTM_HEADS, MLSTM_HEAD_DIM, MLSTM_HEAD_DIM), MLSTM_HEAD_DIM ** -0.5),
        'w_if': nrm((DEPTH, 3 * MLSTM_WIDTH, 2 * MLSTM_HEADS), 0.02),
        'b_if': jnp.concatenate([b_i, b_f], axis=-1),
        'mh_norm_w': 1.0 + nrm((DEPTH, MLSTM_WIDTH), 0.01),
        'skip_m': 1.0 + nrm((DEPTH, MLSTM_WIDTH), 0.01),
        'w_out': nrm((DEPTH, MIX_WIDTH, D_MODEL), MIX_WIDTH ** -0.5 * beta),
        'ln1_g': 1.0 + nrm((DEPTH, D_MODEL), 0.01),
        'ln1_b': nrm((DEPTH, D_MODEL), 0.01),
        'w_cq': nrm((DEPTH, D_MODEL, D_MODEL), D_MODEL ** -0.5),
        'w_ck': nrm((DEPTH, D_MODEL, D_MODEL), D_MODEL ** -0.5),
        'w_cv': nrm((DEPTH, D_MODEL, D_MODEL), D_MODEL ** -0.5 * beta),
        'w_co': nrm((DEPTH, D_MODEL, D_MODEL), D_MODEL ** -0.5 * beta),
        'ln2_g': 1.0 + nrm((DEPTH, D_MODEL), 0.01),
        'ln2_b': nrm((DEPTH, D_MODEL), 0.01),
        'w_router': nrm((DEPTH, D_MODEL, N_EXPERTS), D_MODEL ** -0.5),
        'b_router': nrm((DEPTH, N_EXPERTS), 0.01),
        'w_gate': nrm((DEPTH, N_EXPERTS, D_MODEL, EXPERT_FF), D_MODEL ** -0.5),
        'w_up': nrm((DEPTH, N_EXPERTS, D_MODEL, EXPERT_FF), D_MODEL ** -0.5),
        'w_down': nrm((DEPTH, N_EXPERTS, EXPERT_FF, D_MODEL), EXPERT_FF ** -0.5 * beta),
        'ws_gate': nrm((DEPTH, D_MODEL, SHARED_FF), D_MODEL ** -0.5),
        'ws_up': nrm((DEPTH, D_MODEL, SHARED_FF), D_MODEL ** -0.5),
        'ws_down': nrm((DEPTH, SHARED_FF, D_MODEL), SHARED_FF ** -0.5 * beta),
        'ln3_g': 1.0 + nrm((DEPTH, D_MODEL), 0.01),
        'ln3_b': nrm((DEPTH, D_MODEL), 0.01),
    }


def reference(x_prompt, x_sample, mem_prompt, cache_win128, cache_win512, cache_win2048, cache_mem_kv,
              state_conv, state_C, state_n, state_m, w_in, conv_w, conv_b, wq_m, wk_m, w_if, b_if,
              mh_norm_w, skip_m, w_out, ln1_g, ln1_b, w_cq, w_ck, w_cv, w_co, ln2_g, ln2_b,
              w_router, b_router, w_gate, w_up, w_down, ws_gate, ws_up, ws_down, ln3_g, ln3_b):
    pos_prompt = jnp.arange(SEQ)
    pos_sample = PAST_LEN + jnp.arange(DEC_SEQ)
    hp, hs = x_prompt, x_sample
    new_p, new_s, mem_p = [], [], []
    for l in range(DEPTH):
        p = dict(w_in=w_in[l], conv_w=conv_w[l], conv_b=conv_b[l], wq_m=wq_m[l], wk_m=wk_m[l],
                 w_if=w_if[l], b_if=b_if[l], mh_norm_w=mh_norm_w[l], skip_m=skip_m[l], w_out=w_out[l],
                 ln1_g=ln1_g[l], ln1_b=ln1_b[l], w_cq=w_cq[l], w_co=w_co[l], ln2_g=ln2_g[l], ln2_b=ln2_b[l],
                 w_router=w_router[l], b_router=b_router[l], w_gate=w_gate[l], w_up=w_up[l],
                 w_down=w_down[l], ws_gate=ws_gate[l], ws_up=ws_up[l], ws_down=ws_down[l],
                 ln3_g=ln3_g[l], ln3_b=ln3_b[l])
        kv_p = memory_kv(mem_prompt, w_ck[l], w_cv[l])
        empty = tuple(jnp.zeros((BATCH, 0, 2, ATT_HEADS_PER_GROUP, HEAD_DIM), x_prompt.dtype) for _ in ATT_GROUPS)
        hp, st_p = layer_forward(
            hp, pos_prompt, empty,
            jnp.zeros((BATCH, CONV_WIDTH - 1, MLSTM_WIDTH), x_prompt.dtype),
            jnp.zeros((BATCH, MLSTM_HEADS, MLSTM_HEAD_DIM, MLSTM_HEAD_DIM), F32),
            jnp.zeros((BATCH, MLSTM_HEADS, MLSTM_HEAD_DIM), F32),
            jnp.zeros((BATCH, MLSTM_HEADS), F32),
            kv_p, p)
        hs, st_s = layer_forward(
            hs, pos_sample, (cache_win128[l], cache_win512[l], cache_win2048[l]),
            state_conv[l], state_C[l], state_n[l], state_m[l], cache_mem_kv[l], p)
        new_p.append(st_p)
        new_s.append(st_s)
        mem_p.append(kv_p)
    p_w128, p_w512, p_w2048, p_conv, p_C, p_n, p_m = [jnp.stack(a) for a in zip(*new_p)]
    s_w128, s_w512, s_w2048, s_conv, s_C, s_n, s_m = [jnp.stack(a) for a in zip(*new_s)]
    p_mem_kv = jnp.stack(mem_p)
    return (hp, hs, p_w128, p_w512, p_w2048, p_mem_kv, p_conv, p_C, p_n, p_m,
            s_w128, s_w512, s_w2048, s_conv, s_C, s_n, s_m)
```

```python
import functools
import math

import jax
import jax.numpy as jnp
from jax import lax
from jax.experimental import pallas as pl
from jax.experimental.pallas import tpu as pltpu

F32 = jnp.float32
BF16 = jnp.bfloat16

HEAD_DIM = 64
ATT_GROUPS = ((128, 1), (512, 4), (2048, 16))
HEADS_PER_GROUP = 4
GROUP_WIDTH = HEADS_PER_GROUP * HEAD_DIM
ATT_WIDTH = GROUP_WIDTH * len(ATT_GROUPS)
N_KEYS = 129
ROPE_THETA = 10000.0
PAST_LEN = 8192
MLSTM_HEADS = 4
MLSTM_WIDTH = MLSTM_HEADS * HEAD_DIM
CONV_WIDTH = 4
MEM_HEADS = 4
N_EXPERT_GROUPS = 8
TOPK_GROUPS = 4
TOP_K = 6
ROUTED_SCALE = 2.5
LN_EPS = 1e-5
NEG = -1e30
VMEM_LIMIT = 56 * 1024 * 1024


def _cparams(*sem):
    return pltpu.CompilerParams(dimension_semantics=sem, vmem_limit_bytes=VMEM_LIMIT)


def _dot(a, b):
    return jnp.dot(a, b, preferred_element_type=F32)


def _dot_nt(a, b, precision=None):
    return lax.dot_general(a, b, (((1,), (1,)), ((), ())), precision=precision,
                           preferred_element_type=F32)


def _dot_tn(a, b):
    return lax.dot_general(a, b, (((0,), (0,)), ((), ())), preferred_element_type=F32)


def _dot_hi(a, b):
    return jnp.dot(a, b, precision=lax.Precision.HIGHEST, preferred_element_type=F32)


def _layer_norm(x, g, b):
    mu = jnp.mean(x, axis=-1, keepdims=True)
    xc = x - mu
    var = jnp.mean(xc * xc, axis=-1, keepdims=True)
    return xc * lax.rsqrt(var + LN_EPS) * g + b


def _sigmoid(x):
    return 1.0 / (1.0 + jnp.exp(-x))


def _silu(x):
    return x * _sigmoid(x)


def _log_sigmoid(x):
    return jnp.minimum(x, 0.0) - jnp.log(1.0 + jnp.exp(-jnp.abs(x)))


def _inproj_kernel(x_ref, w_ref, cos_ref, sin_ref, q_ref, k_ref, v_ref, m_ref, *tails):
    tm = x_ref.shape[0]
    x = x_ref[...].astype(BF16)
    cos = jnp.concatenate([cos_ref[...]] * (ATT_WIDTH // 128), axis=1)
    sin = jnp.concatenate([sin_ref[...]] * (ATT_WIDTH // 128), axis=1)
    lane = lax.broadcasted_iota(jnp.int32, (tm, ATT_WIDTH), 1)
    first_half = (lane % HEAD_DIM) < (HEAD_DIM // 2)

    def rope(t):
        fwd = pltpu.roll(t, ATT_WIDTH - HEAD_DIM // 2, 1)
        bwd = pltpu.roll(t, HEAD_DIM // 2, 1)
        return t * cos + jnp.where(first_half, fwd, bwd) * sin

    q = rope(_dot(x, w_ref[:, 0:ATT_WIDTH])) * (HEAD_DIM ** -0.5)
    q_ref[...] = q.astype(q_ref.dtype)
    k = rope(_dot(x, w_ref[:, ATT_WIDTH:2 * ATT_WIDTH]))
    k_ref[...] = k.astype(k_ref.dtype)
    v = _dot(x, w_ref[:, 2 * ATT_WIDTH:3 * ATT_WIDTH])
    v_ref[...] = v.astype(v_ref.dtype)
    m_ref[...] = _dot(x, w_ref[:, 3 * ATT_WIDTH:])
    for g, t_ref in enumerate(tails):
        rows = t_ref.shape[1]
        cols = slice(g * GROUP_WIDTH, (g + 1) * GROUP_WIDTH)
        t_ref[0, :, 0:GROUP_WIDTH] = k[tm - rows:, cols]
        t_ref[0, :, GROUP_WIDTH:] = v[tm - rows:, cols]


def _inproj(x, w_in, cos, sin, *, seq, tm, qkv_dtype, with_tails):
    n, d = x.shape
    nt = n // tm
    tiles_per_seq = seq // tm
    out_shape = [jax.ShapeDtypeStruct((n, ATT_WIDTH), qkv_dtype)] * 3 + [
        jax.ShapeDtypeStruct((n, w_in.shape[1] - 3 * ATT_WIDTH), F32)]
    row_spec = lambda w: pl.BlockSpec((tm, w), lambda i: (i, 0))
    out_specs = [row_spec(ATT_WIDTH)] * 3 + [row_spec(w_in.shape[1] - 3 * ATT_WIDTH)]
    if with_tails:
        batch = n // seq
        for window, _ in ATT_GROUPS:
            rows = min(window, tm)
            first = tiles_per_seq - window // rows if window > rows else tiles_per_seq - 1
            out_shape.append(jax.ShapeDtypeStruct((batch, min(window, seq), 2 * GROUP_WIDTH), F32))
            out_specs.append(pl.BlockSpec(
                (1, rows, 2 * GROUP_WIDTH),
                lambda i, first=first: (i // tiles_per_seq, jnp.maximum(i % tiles_per_seq - first, 0), 0)))
    return pl.pallas_call(
        _inproj_kernel,
        grid=(nt,),
        in_specs=[row_spec(d),
                  pl.BlockSpec(w_in.shape, lambda i: (0, 0)),
                  pl.BlockSpec((tm, 128), lambda i: (i % tiles_per_seq, 0)),
                  pl.BlockSpec((tm, 128), lambda i: (i % tiles_per_seq, 0))],
        out_specs=out_specs,
        out_shape=out_shape,
        compiler_params=_cparams("arbitrary"),
    )(x, w_in, cos, sin)


def _win_attn_kernel(q_ref, kp_ref, kc_ref, vp_ref, vc_ref, o_ref, lse_ref):
    j = pl.program_id(2)
    tq = q_ref.shape[1]
    assert tq == N_KEYS - 1
    q = q_ref[0]
    kk = jnp.concatenate([kp_ref[0], kc_ref[0]], axis=0)
    vv = jnp.concatenate([vp_ref[0], vc_ref[0]], axis=0)
    row = lax.broadcasted_iota(jnp.int32, (tq, 2 * tq), 0)
    col = lax.broadcasted_iota(jnp.int32, (tq, 2 * tq), 1)
    valid = (col >= row) & (col <= row + tq) & ((j > 0) | (col >= tq))
    lane = lax.broadcasted_iota(jnp.int32, (tq, GROUP_WIDTH), 1)
    lane_l = lax.broadcasted_iota(jnp.int32, (tq, 128), 1)
    o_acc = jnp.zeros((tq, GROUP_WIDTH), F32)
    lse_acc = jnp.zeros((tq, 128), F32)
    for h in range(HEADS_PER_GROUP):
        qh = jnp.where(lane // HEAD_DIM == h, q, jnp.zeros_like(q))
        s = jnp.where(valid, _dot_nt(qh, kk), NEG)
        m = jnp.max(s, axis=1, keepdims=True)
        p = jnp.exp(s - m)
        l = jnp.sum(p, axis=1, keepdims=True)
        o = _dot(p.astype(BF16), vv) / l
        o_acc = jnp.where(lane // HEAD_DIM == h, o, o_acc)
        lse_acc = jnp.where(lane_l // 32 == h, m + jnp.log(l), lse_acc)
    o_ref[0] = o_acc.astype(o_ref.dtype)
    lse_ref[0] = lse_acc


def _win_attn(q, k, v, g, dil, *, tq=128):
    b, t, _ = q.shape
    ts = t // dil
    view = lambda a: a.reshape(b, ts, dil * ATT_WIDTH)
    ncol = ATT_WIDTH // GROUP_WIDTH
    cur = pl.BlockSpec((1, tq, GROUP_WIDTH), lambda bi, r, j: (bi, j, r * ncol + g))
    prev = pl.BlockSpec((1, tq, GROUP_WIDTH), lambda bi, r, j: (bi, jnp.maximum(j - 1, 0), r * ncol + g))
    o, lse = pl.pallas_call(
        _win_attn_kernel,
        grid=(b, dil, ts // tq),
        in_specs=[cur, prev, cur, prev, cur],
        out_specs=[pl.BlockSpec((1, tq, GROUP_WIDTH), lambda bi, r, j: (bi, j, r)),
                   pl.BlockSpec((1, tq, 128), lambda bi, r, j: (bi, j, r))],
        out_shape=[jax.ShapeDtypeStruct((b, ts, dil * GROUP_WIDTH), BF16),
                   jax.ShapeDtypeStruct((b, ts, dil * 128), F32)],
        compiler_params=_cparams("arbitrary", "arbitrary", "arbitrary"),
    )(view(q), view(k), view(k), view(v), view(v))
    return o.reshape(b * t, GROUP_WIDTH), lse.reshape(b * t, 128)


def _dec_attn_kernel(q_ref, k_ref, v_ref, c0_ref, c1_ref, c2_ref, att_ref, o0_ref, o1_ref, o2_ref,
                     e0_ref, e1_ref, e2_ref):
    t_new = q_ref.shape[1]
    q = q_ref[0]
    k_new = k_ref[0]
    v_new = v_ref[0]
    lane = lax.broadcasted_iota(jnp.int32, (8, GROUP_WIDTH), 1)
    sub = lax.broadcasted_iota(jnp.int32, (8, GROUP_WIDTH), 0)
    head_sel = (lane // HEAD_DIM) == sub
    outs = [[None] * t_new for _ in ATT_GROUPS]
    lses = [[None] * t_new for _ in ATT_GROUPS]
    for g, ((window, dil), c_ref, o_ref, e_ref) in enumerate(
            zip(ATT_GROUPS, (c0_ref, c1_ref, c2_ref), (o0_ref, o1_ref, o2_ref), (e0_ref, e1_ref, e2_ref))):
        w = c_ref.shape[1]
        cols = slice(g * GROUP_WIDTH, (g + 1) * GROUP_WIDTH)
        kv_new = jnp.concatenate([k_new[:, cols], v_new[:, cols]], axis=1)
        n_chunk = e_ref.shape[0]
        for c in range(n_chunk):
            e_ref[c, 0:w, :] = c_ref[0, :, c * 128:(c + 1) * 128]
            e_ref[c, w:w + t_new, :] = kv_new[:, c * 128:(c + 1) * 128]
            o_ref[0, :, c * 128:(c + 1) * 128] = e_ref[c, t_new:w + t_new, :]
        for t in range(t_new):
            past = jnp.concatenate(
                [e_ref[c, pl.ds(t, N_KEYS - 1, stride=dil), :] for c in range(n_chunk)], axis=1)
            qm = jnp.where(head_sel, jnp.broadcast_to(q[t:t + 1, cols], (8, GROUP_WIDTH)), 0.0)
            s = _dot_nt(qm.astype(BF16), past[:, 0:GROUP_WIDTH].astype(BF16))
            k_self = kv_new[t:t + 1, 0:GROUP_WIDTH]
            v_self = kv_new[t:t + 1, GROUP_WIDTH:]
            s_self = jnp.sum(qm * k_self, axis=1, keepdims=True)
            m = jnp.maximum(jnp.max(s, axis=1, keepdims=True), s_self)
            p = jnp.exp(s - m)
            p_self = jnp.exp(s_self - m)
            l = jnp.sum(p, axis=1, keepdims=True) + p_self
            o = _dot(p.astype(BF16), past[:, GROUP_WIDTH:].astype(BF16))
            o = (o + p_self * v_self) / l
            outs[g][t] = o
            lses[g][t] = m + jnp.log(l)
    for t in range(t_new):
        top = jnp.maximum(jnp.maximum(lses[0][t], lses[1][t]), lses[2][t])
        es = [jnp.exp(lses[g][t] - top) for g in range(len(ATT_GROUPS))]
        tot = es[0] + es[1] + es[2]
        for g in range(len(ATT_GROUPS)):
            weighted = jnp.where(head_sel, outs[g][t] * (es[g] / tot), 0.0)
            att_ref[0, t:t + 1, g * GROUP_WIDTH:(g + 1) * GROUP_WIDTH] = jnp.sum(weighted, axis=0, keepdims=True)


def _dec_attn(q, k, v, caches):
    b, t_new, _ = q.shape
    tok = pl.BlockSpec((1, t_new, ATT_WIDTH), lambda i: (i, 0, 0))
    cspec = [pl.BlockSpec((1,) + c.shape[1:], lambda i: (i, 0, 0)) for c in caches]
    return pl.pallas_call(
        _dec_attn_kernel,
        grid=(b,),
        in_specs=[tok, tok, tok] + cspec,
        out_specs=[tok] + cspec,
        out_shape=[jax.ShapeDtypeStruct(q.shape, F32)] + [jax.ShapeDtypeStruct(c.shape, F32) for c in caches],
        scratch_shapes=[pltpu.VMEM((c.shape[2] // 128, c.shape[1] + t_new, 128), F32) for c in caches],
        compiler_params=_cparams("arbitrary"),
    )(q, k, v, *caches)


def _mlstm_kernel(m3_ref, conv0_ref, c0_ref, n0_ref, m0_ref, convw_ref, convb_ref, wq_ref, wk_ref,
                  wif_ref, wift_ref, bif_ref, bift_ref, normw_ref, skip_ref,
                  hm_ref, convo_ref, co_ref, no_ref, mo_ref, cbuf, c_s, n_s, m_s):
    j = pl.program_id(1)
    L = m3_ref.shape[1]
    W = MLSTM_WIDTH
    D = HEAD_DIM

    @pl.when(j == 0)
    def _():
        cbuf[0:8, :] = jnp.zeros((8, W), F32)
        cbuf[8 - (CONV_WIDTH - 1):8, :] = conv0_ref[0]
        c_s[...] = c0_ref[0]
        n_s[...] = n0_ref[0]
        m_s[...] = m0_ref[0]

    blk = m3_ref[0]
    c_in = blk[:, 0:W]
    v_m = blk[:, W:2 * W]
    z = blk[:, 2 * W:3 * W]
    cbuf[8:8 + L, :] = c_in
    acc = jnp.zeros((L, W), F32) + convb_ref[...]
    for tap in range(CONV_WIDTH):
        off = 8 - (CONV_WIDTH - 1) + tap
        acc = acc + cbuf[off:off + L, :] * convw_ref[tap:tap + 1, :]
    xc = _silu(acc)
    convo_ref[0] = cbuf[8 + L - (CONV_WIDTH - 1):8 + L, :]
    cbuf[0:8, :] = cbuf[L:L + 8, :]

    xcb = xc.astype(BF16)
    q_m = _dot(xcb, wq_ref[...])
    k_m = _dot(xcb, wk_ref[...])
    gate_in = jnp.concatenate([q_m, k_m, v_m], axis=1)
    g_col = _dot_hi(gate_in, wif_ref[...]) + bif_ref[...]
    g_row = _dot_nt(wift_ref[...], gate_in, precision=lax.Precision.HIGHEST) + bift_ref[...]
    i_col, lf_col = g_col[:, 0:MLSTM_HEADS], _log_sigmoid(g_col[:, MLSTM_HEADS:])
    i_row, lf_row = g_row[0:MLSTM_HEADS, :], _log_sigmoid(g_row[MLSTM_HEADS:, :])
    rr = lax.broadcasted_iota(jnp.int32, (L, L), 0)
    cc = lax.broadcasted_iota(jnp.int32, (L, L), 1)
    causal = cc <= rr
    tri = causal.astype(F32)
    b_col = _dot_hi(tri, lf_col)
    b_row = _dot_nt(lf_row, tri, precision=lax.Precision.HIGHEST)
    ks = k_m * (D ** -0.5)
    qb = q_m.astype(BF16)
    kb = ks.astype(BF16)
    vb = v_m.astype(BF16)
    lane = lax.broadcasted_iota(jnp.int32, (L, W), 1)
    m_prev_all = m_s[...]
    h_all = jnp.zeros((L, W), F32)
    m_new_list = []
    for h in range(MLSTM_HEADS):
        hs = slice(h * D, (h + 1) * D)
        m_prev = m_prev_all[:, h:h + 1]
        bc = b_col[:, h:h + 1]
        br = b_row[h:h + 1, :]
        ir = i_row[h:h + 1, :]
        ic = i_col[:, h:h + 1]
        d_intra = jnp.where(causal, bc - br + ir, -jnp.inf)
        a_inter = bc + m_prev
        m_t = jnp.maximum(a_inter, jnp.max(d_intra, axis=1, keepdims=True))
        qh, kh, vh = qb[:, hs], kb[:, hs], vb[:, hs]
        s = _dot_nt(qh, kh) * jnp.exp(d_intra - m_t)
        w_inter = jnp.exp(a_inter - m_t)
        c_h = c_s[h * D:(h + 1) * D, :]
        n_h = n_s[h:h + 1, :]
        num = _dot(s.astype(BF16), vh) + w_inter * _dot_nt(qh, c_h.astype(BF16))
        den = jnp.sum(s, axis=1, keepdims=True) + w_inter * jnp.sum(q_m[:, hs] * n_h, axis=1, keepdims=True)
        hh = num / jnp.maximum(jnp.abs(den), jnp.exp(-m_t))
        mu = jnp.mean(hh, axis=1, keepdims=True)
        hc = hh - mu
        var = jnp.mean(hc * hc, axis=1, keepdims=True)
        hn = hc * lax.rsqrt(var + LN_EPS)
        h_all = jnp.where(lane // D == h, jnp.concatenate([hn] * MLSTM_HEADS, axis=1), h_all)
        g_tot = bc[L - 1:L, :]
        a_end = g_tot + m_prev
        d_end_c = g_tot - bc + ic
        m_new = jnp.maximum(a_end, jnp.max(d_end_c, axis=0, keepdims=True))
        w_s = jnp.exp(d_end_c - m_new)
        decay = jnp.exp(a_end - m_new)
        wv = (v_m[:, hs] * w_s).astype(BF16)
        c_s[h * D:(h + 1) * D, :] = decay * c_h + _dot_tn(wv, kh)
        n_s[h:h + 1, :] = decay * n_h + jnp.sum(w_s * ks[:, hs], axis=0, keepdims=True)
        m_new_list.append(m_new)
    m_s[...] = jnp.concatenate(m_new_list, axis=1)
    hm = (h_all * normw_ref[...] + skip_ref[...] * xc) * _silu(z)
    hm_ref[0] = hm.astype(hm_ref.dtype)
    co_ref[0] = c_s[...]
    no_ref[0] = n_s[...]
    mo_ref[0] = m_s[...]


def _mlstm(m3, conv0, c0, n0, m0, conv_w, conv_b, wq_bd, wk_bd, w_if, b_if, norm_w, skip, *, chunk):
    b, t, _ = m3.shape
    W = MLSTM_WIDTH
    full = lambda a: pl.BlockSpec(a.shape, lambda bi, j: (0,) * a.ndim)
    per_b = lambda a: pl.BlockSpec((1,) + a.shape[1:], lambda bi, j: (bi,) + (0,) * (a.ndim - 1))
    consts = [conv_w, conv_b.reshape(1, W), wq_bd, wk_bd, w_if, w_if.T, b_if.reshape(1, -1),
              b_if.reshape(-1, 1), norm_w.reshape(1, W), skip.reshape(1, W)]
    states = [conv0, c0, n0, m0]
    return pl.pallas_call(
        _mlstm_kernel,
        grid=(b, t // chunk),
        in_specs=[pl.BlockSpec((1, chunk, 3 * W), lambda bi, j: (bi, j, 0))] + [per_b(s) for s in states]
                 + [full(c) for c in consts],
        out_specs=[pl.BlockSpec((1, chunk, W), lambda bi, j: (bi, j, 0))] + [per_b(s) for s in states],
        out_shape=[jax.ShapeDtypeStruct((b, t, W), BF16)] + [jax.ShapeDtypeStruct(s.shape, F32) for s in states],
        scratch_shapes=[pltpu.VMEM((chunk + 8, W), F32), pltpu.VMEM(c0.shape[1:], F32),
                        pltpu.VMEM(n0.shape[1:], F32), pltpu.VMEM(m0.shape[1:], F32)],
        compiler_params=_cparams("arbitrary", "arbitrary"),
    )(m3, *states, *consts)


def _combine(o_refs, lse_refs):
    tm = o_refs[0].shape[0]
    lane = lax.broadcasted_iota(jnp.int32, (tm, GROUP_WIDTH), 1)

    def spread(l_ref):
        l2 = l_ref[...]
        out = jnp.zeros((tm, GROUP_WIDTH), F32)
        for h in range(HEADS_PER_GROUP):
            out = jnp.where(lane // HEAD_DIM == h, l2[:, 32 * h:32 * h + 1], out)
        return out

    ls = [spread(r) for r in lse_refs]
    top = jnp.maximum(jnp.maximum(ls[0], ls[1]), ls[2])
    es = [jnp.exp(l - top) for l in ls]
    tot = es[0] + es[1] + es[2]
    return [(o[...].astype(F32) * (e / tot)).astype(BF16) for o, e in zip(o_refs, es)]


def _outproj_kernel(combine, alpha, *refs):
    if combine:
        (o0, o1, o2, l0, l1, l2, hm_ref, x_ref, wo_ref, g_ref, b_ref, wq_ref, x1_ref, qc_ref) = refs
        att = _combine((o0, o1, o2), (l0, l1, l2))
    else:
        (a_ref, hm_ref, x_ref, wo_ref, g_ref, b_ref, wq_ref, x1_ref, qc_ref) = refs
        att = [a_ref[:, g * GROUP_WIDTH:(g + 1) * GROUP_WIDTH].astype(BF16) for g in range(len(ATT_GROUPS))]
    mix = _dot(hm_ref[...].astype(BF16), wo_ref[ATT_WIDTH:, :])
    for g, a in enumerate(att):
        mix = mix + _dot(a, wo_ref[g * GROUP_WIDTH:(g + 1) * GROUP_WIDTH, :])
    x1 = _layer_norm(alpha * x_ref[...] + mix, g_ref[...], b_ref[...])
    x1_ref[...] = x1
    qc_ref[...] = (_dot(x1.astype(BF16), wq_ref[...]) * ((x1.shape[1] // MEM_HEADS) ** -0.5)).astype(qc_ref.dtype)


def _outproj(att_parts, hm, x, w_out, ln_g, ln_b, w_cq, alpha, *, tm):
    n, d = x.shape
    combine = len(att_parts) > 1
    row = lambda a: pl.BlockSpec((tm, a.shape[1]), lambda i: (i, 0))
    full = lambda a: pl.BlockSpec(a.shape, lambda i: (0, 0))
    ins = list(att_parts) + [hm, x]
    consts = [w_out, ln_g.reshape(1, d), ln_b.reshape(1, d), w_cq]
    return pl.pallas_call(
        functools.partial(_outproj_kernel, combine, alpha),
        grid=(n // tm,),
        in_specs=[row(a) for a in ins] + [full(c) for c in consts],
        out_specs=[pl.BlockSpec((tm, d), lambda i: (i, 0))] * 2,
        out_shape=[jax.ShapeDtypeStruct((n, d), F32), jax.ShapeDtypeStruct((n, d), BF16)],
        compiler_params=_cparams("arbitrary"),
    )(*ins, *consts)


def _memkv_kernel(mem_ref, wk_ref, wv_ref, kv_ref):
    d = mem_ref.shape[2]
    mem = mem_ref[0].astype(BF16)
    kv_ref[0, :, 0:d] = _dot(mem, wk_ref[...])
    kv_ref[0, :, d:] = _dot(mem, wv_ref[...])


def _memkv(mem, w_ck, w_cv):
    b, m, d = mem.shape
    return pl.pallas_call(
        _memkv_kernel,
        grid=(b,),
        in_specs=[pl.BlockSpec((1, m, d), lambda i: (i, 0, 0)),
                  pl.BlockSpec(w_ck.shape, lambda i: (0, 0)), pl.BlockSpec(w_cv.shape, lambda i: (0, 0))],
        out_specs=pl.BlockSpec((1, m, 2 * d), lambda i: (i, 0, 0)),
        out_shape=jax.ShapeDtypeStruct((b, m, 2 * d), F32),
        compiler_params=_cparams("arbitrary"),
    )(mem, w_ck, w_cv)


def _xattn_kernel(q_ref, kv_ref, o_ref):
    d = q_ref.shape[2]
    hd = d // MEM_HEADS
    q = q_ref[0]
    for h in range(MEM_HEADS):
        k = kv_ref[0, :, h * hd:(h + 1) * hd].astype(BF16)
        v = kv_ref[0, :, d + h * hd:d + (h + 1) * hd].astype(BF16)
        s = _dot_nt(q[:, h * hd:(h + 1) * hd], k)
        p = jnp.exp(s - jnp.max(s, axis=1, keepdims=True))
        p = p / jnp.sum(p, axis=1, keepdims=True)
        o_ref[0, :, h * hd:(h + 1) * hd] = _dot(p.astype(BF16), v).astype(o_ref.dtype)


def _xattn(qc, kv, *, tq):
    b, t, d = qc.shape
    m = kv.shape[1]
    return pl.pallas_call(
        _xattn_kernel,
        grid=(b, t // tq),
        in_specs=[pl.BlockSpec((1, tq, d), lambda bi, j: (bi, j, 0)),
                  pl.BlockSpec((1, m, 2 * d), lambda bi, j: (bi, 0, 0))],
        out_specs=pl.BlockSpec((1, tq, d), lambda bi, j: (bi, j, 0)),
        out_shape=jax.ShapeDtypeStruct((b, t, d), BF16),
        compiler_params=_cparams("arbitrary", "arbitrary"),
    )(qc, kv)


def _xout_kernel(alpha, o_ref, x_ref, w_ref, g_ref, b_ref, y_ref):
    y_ref[...] = _layer_norm(alpha * x_ref[...] + _dot(o_ref[...], w_ref[...]), g_ref[...], b_ref[...])


def _xout(oc, x1, w_co, ln_g, ln_b, alpha, *, tm):
    n, d = x1.shape
    row = pl.BlockSpec((tm, d), lambda i: (i, 0))
    full = lambda a: pl.BlockSpec(a.shape, lambda i: (0, 0))
    consts = [w_co, ln_g.reshape(1, d), ln_b.reshape(1, d)]
    return pl.pallas_call(
        functools.partial(_xout_kernel, alpha),
        grid=(n // tm,),
        in_specs=[row, row] + [full(c) for c in consts],
        out_specs=row,
        out_shape=jax.ShapeDtypeStruct((n, d), F32),
        compiler_params=_cparams("arbitrary"),
    )(oc, x1, *consts)


def _first_index_of_max(vals, idx, big):
    mx = jnp.max(vals, axis=0, keepdims=True)
    return mx, jnp.min(jnp.where(vals == mx, idx, big), axis=0, keepdims=True)


def _router_kernel(x_ref, wrt_ref, br_ref, gates_ref):
    tm = x_ref.shape[0]
    n_e = wrt_ref.shape[0]
    per_group = n_e // N_EXPERT_GROUPS
    logits = _dot_nt(wrt_ref[...], x_ref[...].astype(BF16))
    scores = _sigmoid(logits)
    biased = scores + br_ref[...]
    e_idx = lax.broadcasted_iota(jnp.int32, (n_e, tm), 0).astype(F32)
    g_scores = []
    for g in range(N_EXPERT_GROUPS):
        sub = biased[g * per_group:(g + 1) * per_group, :]
        sidx = lax.broadcasted_iota(jnp.int32, (per_group, tm), 0).astype(F32)
        m1, a1 = _first_index_of_max(sub, sidx, per_group)
        m2 = jnp.max(jnp.where(sidx == a1, -jnp.inf, sub), axis=0, keepdims=True)
        g_scores.append(m1 + m2)
    gs = jnp.concatenate(g_scores, axis=0)
    g_idx = lax.broadcasted_iota(jnp.int32, (N_EXPERT_GROUPS, tm), 0).astype(F32)
    g_sel = jnp.zeros((N_EXPERT_GROUPS, tm), F32)
    work = gs
    for _ in range(TOPK_GROUPS):
        _, a = _first_index_of_max(work, g_idx, N_EXPERT_GROUPS)
        hit = g_idx == a
        g_sel = jnp.where(hit, 1.0, g_sel)
        work = jnp.where(hit, -jnp.inf, work)
    e_mask = jnp.concatenate(
        [jnp.broadcast_to(g_sel[g:g + 1, :], (per_group, tm)) for g in range(N_EXPERT_GROUPS)], axis=0)
    work = jnp.where(e_mask > 0.5, biased, -jnp.inf)
    sel = jnp.zeros((n_e, tm), F32)
    for _ in range(TOP_K):
        cand = jnp.where(sel > 0.5, -jnp.inf, work)
        mx = jnp.max(cand, axis=0, keepdims=True)
        a = jnp.min(jnp.where((cand == mx) & (sel < 0.5), e_idx, float(n_e)), axis=0, keepdims=True)
        sel = jnp.where(e_idx == a, 1.0, sel)
    w_sel = sel * scores
    gates_t = w_sel / jnp.sum(w_sel, axis=0, keepdims=True) * ROUTED_SCALE
    pad = jnp.zeros((128 - n_e, tm), F32)
    gates_ref[...] = jnp.transpose(jnp.concatenate([gates_t, pad], axis=0))


def _router(x, w_router_t, b_router, *, tm):
    n, d = x.shape
    n_e = w_router_t.shape[0]
    return pl.pallas_call(
        _router_kernel,
        grid=(n // tm,),
        in_specs=[pl.BlockSpec((tm, d), lambda i: (i, 0)),
                  pl.BlockSpec(w_router_t.shape, lambda i: (0, 0)),
                  pl.BlockSpec((n_e, 1), lambda i: (0, 0))],
        out_specs=pl.BlockSpec((tm, 128), lambda i: (i, 0)),
        out_shape=jax.ShapeDtypeStruct((n, 128), F32),
        compiler_params=_cparams("arbitrary"),
    )(x, w_router_t, b_router.reshape(n_e, 1))


def _experts_kernel(alpha, x_ref, gates_ref, wg_ref, wu_ref, wd_ref, sg_ref, su_ref, sd_ref, g_ref, b_ref,
                    y_ref, acc_ref, xb_ref):
    e = pl.program_id(1)
    tm = x_ref.shape[0]

    @pl.when(e == 0)
    def _():
        xb = x_ref[...].astype(BF16)
        xb_ref[...] = xb
        hs = _silu(_dot(xb, sg_ref[...])) * _dot(xb, su_ref[...])
        acc_ref[...] = _dot(hs.astype(BF16), sd_ref[...])

    xb = xb_ref[...]
    lane = lax.broadcasted_iota(jnp.int32, (tm, 128), 1)
    gate = jnp.sum(jnp.where(lane == e, gates_ref[...], 0.0), axis=1, keepdims=True)
    hh = _silu(_dot(xb, wg_ref[0])) * _dot(xb, wu_ref[0]) * gate
    acc_ref[...] += _dot(hh.astype(BF16), wd_ref[0])

    @pl.when(e == pl.num_programs(1) - 1)
    def _():
        y_ref[...] = _layer_norm(alpha * x_ref[...] + acc_ref[...], g_ref[...], b_ref[...])


def _experts(x, gates, w_gate, w_up, w_down, ws_gate, ws_up, ws_down, ln_g, ln_b, alpha, *, tm):
    n, d = x.shape
    n_e, _, ff = w_gate.shape
    row = lambda w: pl.BlockSpec((tm, w), lambda i, e: (i, 0))
    full = lambda a: pl.BlockSpec(a.shape, lambda i, e: (0, 0))
    consts = [ws_gate, ws_up, ws_down, ln_g.reshape(1, d), ln_b.reshape(1, d)]
    return pl.pallas_call(
        functools.partial(_experts_kernel, alpha),
        grid=(n // tm, n_e),
        in_specs=[row(d), row(128),
                  pl.BlockSpec((1, d, ff), lambda i, e: (e, 0, 0)),
                  pl.BlockSpec((1, d, ff), lambda i, e: (e, 0, 0)),
                  pl.BlockSpec((1, ff, d), lambda i, e: (e, 0, 0))] + [full(c) for c in consts],
        out_specs=row(d),
        out_shape=jax.ShapeDtypeStruct((n, d), F32),
        scratch_shapes=[pltpu.VMEM((tm, d), F32), pltpu.VMEM((tm, d), BF16)],
        compiler_params=_cparams("arbitrary", "arbitrary"),
    )(x, gates, w_gate, w_up, w_down, *consts)


def _rope_tables(pos):
    half = HEAD_DIM // 2
    inv_freq = ROPE_THETA ** (-jnp.arange(half, dtype=F32) / half)
    ang = pos.astype(F32)[:, None] * inv_freq[None, :]
    cos, sin = jnp.cos(ang), jnp.sin(ang)
    return jnp.tile(jnp.concatenate([cos, cos], axis=1), (1, 2)), jnp.tile(jnp.concatenate([-sin, sin], axis=1), (1, 2))


def _block_diag(w):
    h, d, _ = w.shape
    out = jnp.zeros((h * d, h * d), w.dtype)
    for i in range(h):
        out = out.at[i * d:(i + 1) * d, i * d:(i + 1) * d].set(w[i])
    return out


def _pick(n, pref):
    return pref if n % pref == 0 else n


def kernel(x_prompt, x_sample, mem_prompt, cache_win128, cache_win512, cache_win2048, cache_mem_kv, state_conv, state_C, state_n, state_m, w_in, conv_w, conv_b, wq_m, wk_m, w_if, b_if, mh_norm_w, skip_m, w_out, ln1_g, ln1_b, w_cq, w_ck, w_cv, w_co, ln2_g, ln2_b, w_router, b_router, w_gate, w_up, w_down, ws_gate, ws_up, ws_down, ln3_g, ln3_b):
    depth = w_in.shape[0]
    assert depth == 1
    alpha = float((2 * depth) ** 0.25)
    bp, seq, d = x_prompt.shape
    bs, dec, _ = x_sample.shape
    assert seq % ATT_GROUPS[-1][0] == 0
    for c, (window, _) in zip((cache_win128, cache_win512, cache_win2048), ATT_GROUPS):
        assert c.shape[2] == window

    l = 0
    bf = lambda a: a.astype(BF16)
    w_in_b, w_out_b = bf(w_in[l]), bf(w_out[l])
    w_cq_b, w_ck_b, w_cv_b, w_co_b = bf(w_cq[l]), bf(w_ck[l]), bf(w_cv[l]), bf(w_co[l])
    wq_bd, wk_bd = bf(_block_diag(wq_m[l])), bf(_block_diag(wk_m[l]))
    w_router_t = bf(w_router[l].T)
    w_gate_b, w_up_b, w_down_b = bf(w_gate[l]), bf(w_up[l]), bf(w_down[l])
    ws_gate_b, ws_up_b, ws_down_b = bf(ws_gate[l]), bf(ws_up[l]), bf(ws_down[l])

    def tail_of_layer(x1, qc, kv, batch, t, tm, tq):
        n = batch * t
        oc = _xattn(qc.reshape(batch, t, d), kv, tq=tq).reshape(n, d)
        x2 = _xout(oc, x1, w_co_b, ln2_g[l], ln2_b[l], alpha, tm=tm)
        gates = _router(x2, w_router_t, b_router[l], tm=tm)
        return _experts(x2, gates, w_gate_b, w_up_b, w_down_b, ws_gate_b, ws_up_b, ws_down_b,
                        ln3_g[l], ln3_b[l], alpha, tm=_pick(n, 1024))

    def mlstm(m3, batch, t, states, chunk):
        return _mlstm(m3.reshape(batch, t, -1), *states, conv_w[l], conv_b[l], wq_bd, wk_bd, w_if[l], b_if[l],
                      mh_norm_w[l], skip_m[l], chunk=chunk)

    np_ = bp * seq
    xp = x_prompt.reshape(np_, d)
    cos_p, sin_p = _rope_tables(jnp.arange(seq))
    q, k, v, m3, t128, t512, t2048 = _inproj(xp, w_in_b, cos_p, sin_p, seq=seq, tm=512, qkv_dtype=BF16,
                                              with_tails=True)
    parts, lses = [], []
    for g, (_, dil) in enumerate(ATT_GROUPS):
        o, lse = _win_attn(q.reshape(bp, seq, -1), k.reshape(bp, seq, -1), v.reshape(bp, seq, -1), g, dil)
        parts.append(o)
        lses.append(lse)
    zeros_p = [jnp.zeros((bp, CONV_WIDTH - 1, MLSTM_WIDTH), F32), jnp.zeros((bp, MLSTM_WIDTH, HEAD_DIM), F32),
               jnp.zeros((bp, MLSTM_HEADS, HEAD_DIM), F32), jnp.zeros((bp, 1, MLSTM_HEADS), F32)]
    hm_p, p_conv, p_c, p_n, p_m = mlstm(m3, bp, seq, zeros_p, 128)
    x1, qc = _outproj(parts + lses, hm_p.reshape(np_, -1), xp, w_out_b, ln1_g[l], ln1_b[l], w_cq_b, alpha, tm=256)
    kv_p = _memkv(mem_prompt, w_ck_b, w_cv_b)
    y_p = tail_of_layer(x1, qc, kv_p, bp, seq, 256, 512)

    ns = bs * dec
    xs = x_sample.reshape(ns, d)
    cos_s, sin_s = _rope_tables(jnp.tile(PAST_LEN + jnp.arange(dec), bs))
    qs, ks, vs, m3s = _inproj(xs, w_in_b, cos_s, sin_s, seq=ns, tm=ns, qkv_dtype=F32, with_tails=False)
    caches = [c[l].reshape(bs, c.shape[2], 2 * GROUP_WIDTH) for c in (cache_win128, cache_win512, cache_win2048)]
    att_s, s128, s512, s2048 = _dec_attn(qs.reshape(bs, dec, -1), ks.reshape(bs, dec, -1), vs.reshape(bs, dec, -1),
                                         caches)
    states_s = [state_conv[l], state_C[l].reshape(bs, MLSTM_WIDTH, HEAD_DIM), state_n[l],
                state_m[l].reshape(bs, 1, MLSTM_HEADS)]
    hm_s, s_conv, s_c, s_n, s_m = mlstm(m3s, bs, dec, states_s, dec)
    x1s, qcs = _outproj([att_s.reshape(ns, -1)], hm_s.reshape(ns, -1), xs, w_out_b, ln1_g[l], ln1_b[l], w_cq_b,
                        alpha, tm=ns)
    kv_s = cache_mem_kv[l].reshape(bs, cache_mem_kv.shape[2], 2 * d)
    y_s = tail_of_layer(x1s, qcs, kv_s, bs, dec, ns, dec)

    win_shape = lambda a, b_: a.reshape(1, b_, a.shape[1], 2, HEADS_PER_GROUP, HEAD_DIM)
    return (y_p.reshape(bp, seq, d), y_s.reshape(bs, dec, d),
            win_shape(t128, bp), win_shape(t512, bp), win_shape(t2048, bp),
            kv_p.reshape(1, bp, mem_prompt.shape[1], 2, MEM_HEADS, d // MEM_HEADS),
            p_conv[None], p_c.reshape(1, bp, MLSTM_HEADS, HEAD_DIM, HEAD_DIM), p_n[None],
            p_m.reshape(1, bp, MLSTM_HEADS),
            win_shape(s128, bs), win_shape(s512, bs), win_shape(s2048, bs),
            s_conv[None], s_c.reshape(1, bs, MLSTM_HEADS, HEAD_DIM, HEAD_DIM), s_n[None],
            s_m.reshape(1, bs, MLSTM_HEADS))
```

```python
import functools
import math

import jax
import jax.numpy as jnp
from jax import lax
from jax.experimental import pallas as pl
from jax.experimental.pallas import tpu as pltpu

F32 = jnp.float32
BF16 = jnp.bfloat16

HEAD_DIM = 64
ATT_GROUPS = ((128, 1), (512, 4), (2048, 16))
HEADS_PER_GROUP = 4
GROUP_WIDTH = HEADS_PER_GROUP * HEAD_DIM
ATT_WIDTH = GROUP_WIDTH * len(ATT_GROUPS)
N_KEYS = 129
ROPE_THETA = 10000.0
PAST_LEN = 8192
MLSTM_HEADS = 4
MLSTM_WIDTH = MLSTM_HEADS * HEAD_DIM
CONV_WIDTH = 4
MEM_HEADS = 4
N_EXPERT_GROUPS = 8
TOPK_GROUPS = 4
TOP_K = 6
ROUTED_SCALE = 2.5
LN_EPS = 1e-5
NEG = -1e30
VMEM_LIMIT = 56 * 1024 * 1024


def _cparams(*sem):
    return pltpu.CompilerParams(dimension_semantics=sem, vmem_limit_bytes=VMEM_LIMIT)


def _dot(a, b):
    return jnp.dot(a, b, preferred_element_type=F32)


def _dot_nt(a, b, precision=None):
    return lax.dot_general(a, b, (((1,), (1,)), ((), ())), precision=precision,
                           preferred_element_type=F32)


def _dot_tn(a, b):
    return lax.dot_general(a, b, (((0,), (0,)), ((), ())), preferred_element_type=F32)


def _dot_hi(a, b):
    return jnp.dot(a, b, precision=lax.Precision.HIGHEST, preferred_element_type=F32)


def _layer_norm(x, g, b):
    mu = jnp.mean(x, axis=-1, keepdims=True)
    xc = x - mu
    var = jnp.mean(xc * xc, axis=-1, keepdims=True)
    return xc * lax.rsqrt(var + LN_EPS) * g + b


def _sigmoid(x):
    return 1.0 / (1.0 + jnp.exp(-x))


def _silu(x):
    return x * _sigmoid(x)


def _log_sigmoid(x):
    return jnp.minimum(x, 0.0) - jnp.log(1.0 + jnp.exp(-jnp.abs(x)))


def _inproj_kernel(x_ref, w_ref, cos_ref, sin_ref, q_ref, k_ref, v_ref, m_ref, *tails):
    tm = x_ref.shape[0]
    x = x_ref[...].astype(BF16)
    cos = jnp.concatenate([cos_ref[...]] * (ATT_WIDTH // 128), axis=1)
    sin = jnp.concatenate([sin_ref[...]] * (ATT_WIDTH // 128), axis=1)
    lane = lax.broadcasted_iota(jnp.int32, (tm, ATT_WIDTH), 1)
    first_half = (lane % HEAD_DIM) < (HEAD_DIM // 2)

    def rope(t):
        fwd = pltpu.roll(t, ATT_WIDTH - HEAD_DIM // 2, 1)
        bwd = pltpu.roll(t, HEAD_DIM // 2, 1)
        return t * cos + jnp.where(first_half, fwd, bwd) * sin

    q = rope(_dot(x, w_ref[:, 0:ATT_WIDTH])) * (HEAD_DIM ** -0.5)
    q_ref[...] = q.astype(q_ref.dtype)
    k = rope(_dot(x, w_ref[:, ATT_WIDTH:2 * ATT_WIDTH]))
    k_ref[...] = k.astype(k_ref.dtype)
    v = _dot(x, w_ref[:, 2 * ATT_WIDTH:3 * ATT_WIDTH])
    v_ref[...] = v.astype(v_ref.dtype)
    m_ref[...] = _dot(x, w_ref[:, 3 * ATT_WIDTH:])
    for g, t_ref in enumerate(tails):
        rows = t_ref.shape[1]
        cols = slice(g * GROUP_WIDTH, (g + 1) * GROUP_WIDTH)
        t_ref[0, :, 0:GROUP_WIDTH] = k[tm - rows:, cols]
        t_ref[0, :, GROUP_WIDTH:] = v[tm - rows:, cols]


def _inproj(x, w_in, cos, sin, *, seq, tm, qkv_dtype, with_tails):
    n, d = x.shape
    nt = n // tm
    tiles_per_seq = seq // tm
    out_shape = [jax.ShapeDtypeStruct((n, ATT_WIDTH), qkv_dtype)] * 3 + [
        jax.ShapeDtypeStruct((n, w_in.shape[1] - 3 * ATT_WIDTH), F32)]
    row_spec = lambda w: pl.BlockSpec((tm, w), lambda i: (i, 0))
    out_specs = [row_spec(ATT_WIDTH)] * 3 + [row_spec(w_in.shape[1] - 3 * ATT_WIDTH)]
    if with_tails:
        batch = n // seq
        for window, _ in ATT_GROUPS:
            rows = min(window, tm)
            first = tiles_per_seq - window // rows if window > rows else tiles_per_seq - 1
            out_shape.append(jax.ShapeDtypeStruct((batch, min(window, seq), 2 * GROUP_WIDTH), F32))
            out_specs.append(pl.BlockSpec(
                (1, rows, 2 * GROUP_WIDTH),
                lambda i, first=first: (i // tiles_per_seq, jnp.maximum(i % tiles_per_seq - first, 0), 0)))
    return pl.pallas_call(
        _inproj_kernel,
        grid=(nt,),
        in_specs=[row_spec(d),
                  pl.BlockSpec(w_in.shape, lambda i: (0, 0)),
                  pl.BlockSpec((tm, 128), lambda i: (i % tiles_per_seq, 0)),
                  pl.BlockSpec((tm, 128), lambda i: (i % tiles_per_seq, 0))],
        out_specs=out_specs,
        out_shape=out_shape,
        compiler_params=_cparams("arbitrary"),
        name="inproj_rope",
    )(x, w_in, cos, sin)


def _win_attn_kernel(q_ref, kp_ref, kc_ref, vp_ref, vc_ref, o_ref, lse_ref):
    j = pl.program_id(2)
    tq = q_ref.shape[1]
    sub = N_KEYS - 1
    assert kp_ref.shape[1] == sub and tq % sub == 0
    k_all = jnp.concatenate([kp_ref[0], kc_ref[0]], axis=0)
    v_all = jnp.concatenate([vp_ref[0], vc_ref[0]], axis=0)
    row = lax.broadcasted_iota(jnp.int32, (sub, 2 * sub), 0)
    col = lax.broadcasted_iota(jnp.int32, (sub, 2 * sub), 1)
    band = (col >= row) & (col <= row + sub)
    lane = lax.broadcasted_iota(jnp.int32, (sub, GROUP_WIDTH), 1)
    lane_l = lax.broadcasted_iota(jnp.int32, (sub, 128), 1)
    for i in range(tq // sub):
        q = q_ref[0, i * sub:(i + 1) * sub, :]
        kk = k_all[i * sub:(i + 2) * sub]
        vv = v_all[i * sub:(i + 2) * sub]
        valid = band & ((j > 0) | (col >= sub)) if i == 0 else band
        o_acc = jnp.zeros((sub, GROUP_WIDTH), F32)
        lse_acc = jnp.zeros((sub, 128), F32)
        for h in range(HEADS_PER_GROUP):
            qh = jnp.where(lane // HEAD_DIM == h, q, jnp.zeros_like(q))
            s = jnp.where(valid, _dot_nt(qh, kk), NEG)
            m = jnp.max(s, axis=1, keepdims=True)
            p = jnp.exp(s - m)
            l = jnp.sum(p, axis=1, keepdims=True)
            o = _dot(p.astype(BF16), vv) / l
            o_acc = jnp.where(lane // HEAD_DIM == h, o, o_acc)
            lse_acc = jnp.where(lane_l // 32 == h, m + jnp.log(l), lse_acc)
        o_ref[0, i * sub:(i + 1) * sub, :] = o_acc.astype(o_ref.dtype)
        lse_ref[0, i * sub:(i + 1) * sub, :] = lse_acc


def _win_attn(q, k, v, g, dil, *, tq=512):
    b, t, _ = q.shape
    ts = t // dil
    tq = min(tq, ts)
    sub = N_KEYS - 1
    view = lambda a: a.reshape(b, ts, dil * ATT_WIDTH)
    ncol = ATT_WIDTH // GROUP_WIDTH
    cur = pl.BlockSpec((1, tq, GROUP_WIDTH), lambda bi, r, j: (bi, j, r * ncol + g))
    prev = pl.BlockSpec((1, sub, GROUP_WIDTH),
                        lambda bi, r, j: (bi, jnp.maximum(j * (tq // sub) - 1, 0), r * ncol + g))
    o, lse = pl.pallas_call(
        _win_attn_kernel,
        grid=(b, dil, ts // tq),
        in_specs=[cur, prev, cur, prev, cur],
        out_specs=[pl.BlockSpec((1, tq, GROUP_WIDTH), lambda bi, r, j: (bi, j, r)),
                   pl.BlockSpec((1, tq, 128), lambda bi, r, j: (bi, j, r))],
        out_shape=[jax.ShapeDtypeStruct((b, ts, dil * GROUP_WIDTH), BF16),
                   jax.ShapeDtypeStruct((b, ts, dil * 128), F32)],
        compiler_params=_cparams("arbitrary", "arbitrary", "arbitrary"),
        name="window_attention",
    )(view(q), view(k), view(k), view(v), view(v))
    return o.reshape(b * t, GROUP_WIDTH), lse.reshape(b * t, 128)


def _dec_attn_kernel(q_ref, k_ref, v_ref, c0_ref, c1_ref, c2_ref, att_ref, o0_ref, o1_ref, o2_ref,
                     e0_ref, e1_ref, e2_ref):
    t_new = q_ref.shape[1]
    q = q_ref[0]
    k_new = k_ref[0]
    v_new = v_ref[0]
    lane = lax.broadcasted_iota(jnp.int32, (8, GROUP_WIDTH), 1)
    sub = lax.broadcasted_iota(jnp.int32, (8, GROUP_WIDTH), 0)
    head_sel = (lane // HEAD_DIM) == sub
    outs = [[None] * t_new for _ in ATT_GROUPS]
    lses = [[None] * t_new for _ in ATT_GROUPS]
    for g, ((window, dil), c_ref, o_ref, e_ref) in enumerate(
            zip(ATT_GROUPS, (c0_ref, c1_ref, c2_ref), (o0_ref, o1_ref, o2_ref), (e0_ref, e1_ref, e2_ref))):
        w = c_ref.shape[1]
        cols = slice(g * GROUP_WIDTH, (g + 1) * GROUP_WIDTH)
        kv_new = jnp.concatenate([k_new[:, cols], v_new[:, cols]], axis=1)
        n_chunk = e_ref.shape[0]
        for c in range(n_chunk):
            e_ref[c, 0:w, :] = c_ref[0, :, c * 128:(c + 1) * 128]
            e_ref[c, w:w + t_new, :] = kv_new[:, c * 128:(c + 1) * 128]
            o_ref[0, :, c * 128:(c + 1) * 128] = e_ref[c, t_new:w + t_new, :]
        for t in range(t_new):
            past = jnp.concatenate(
                [e_ref[c, pl.ds(t, N_KEYS - 1, stride=dil), :] for c in range(n_chunk)], axis=1)
            qm = jnp.where(head_sel, jnp.broadcast_to(q[t:t + 1, cols], (8, GROUP_WIDTH)), 0.0)
            s = _dot_nt(qm.astype(BF16), past[:, 0:GROUP_WIDTH].astype(BF16))
            k_self = kv_new[t:t + 1, 0:GROUP_WIDTH]
            v_self = kv_new[t:t + 1, GROUP_WIDTH:]
            s_self = jnp.sum(qm * k_self, axis=1, keepdims=True)
            m = jnp.maximum(jnp.max(s, axis=1, keepdims=True), s_self)
            p = jnp.exp(s - m)
            p_self = jnp.exp(s_self - m)
            l = jnp.sum(p, axis=1, keepdims=True) + p_self
            o = _dot(p.astype(BF16), past[:, GROUP_WIDTH:].astype(BF16))
            o = (o + p_self * v_self) / l
            outs[g][t] = o
            lses[g][t] = m + jnp.log(l)
    for t in range(t_new):
        top = jnp.maximum(jnp.maximum(lses[0][t], lses[1][t]), lses[2][t])
        es = [jnp.exp(lses[g][t] - top) for g in range(len(ATT_GROUPS))]
        tot = es[0] + es[1] + es[2]
        for g in range(len(ATT_GROUPS)):
            weighted = jnp.where(head_sel, outs[g][t] * (es[g] / tot), 0.0)
            att_ref[0, t:t + 1, g * GROUP_WIDTH:(g + 1) * GROUP_WIDTH] = jnp.sum(weighted, axis=0, keepdims=True)


def _dec_attn(q, k, v, caches):
    b, t_new, _ = q.shape
    tok = pl.BlockSpec((1, t_new, ATT_WIDTH), lambda i: (i, 0, 0))
    cspec = [pl.BlockSpec((1,) + c.shape[1:], lambda i: (i, 0, 0)) for c in caches]
    return pl.pallas_call(
        _dec_attn_kernel,
        grid=(b,),
        in_specs=[tok, tok, tok] + cspec,
        out_specs=[tok] + cspec,
        out_shape=[jax.ShapeDtypeStruct(q.shape, F32)] + [jax.ShapeDtypeStruct(c.shape, F32) for c in caches],
        scratch_shapes=[pltpu.VMEM((c.shape[2] // 128, c.shape[1] + t_new, 128), F32) for c in caches],
        compiler_params=_cparams("arbitrary"),
        name="decode_attention",
    )(q, k, v, *caches)


def _mlstm_kernel(m3_ref, conv0_ref, c0_ref, n0_ref, m0_ref, *rest):
    consts, (hm_ref, convo_ref, co_ref, no_ref, mo_ref, cbuf, c_s, n_s, m_s) = rest[:10], rest[10:]
    for b in range(m3_ref.shape[0]):
        _mlstm_chunk(m3_ref.at[b], conv0_ref.at[b], c0_ref.at[b], n0_ref.at[b], m0_ref.at[b], *consts,
                     hm_ref.at[b], convo_ref.at[b], co_ref.at[b], no_ref.at[b], mo_ref.at[b],
                     cbuf.at[b], c_s.at[b], n_s.at[b], m_s.at[b])


def _mlstm_chunk(m3_ref, conv0_ref, c0_ref, n0_ref, m0_ref, convw_ref, convb_ref, wq_ref, wk_ref,
                 wif_ref, wift_ref, bif_ref, bift_ref, normw_ref, skip_ref,
                 hm_ref, convo_ref, co_ref, no_ref, mo_ref, cbuf, c_s, n_s, m_s):
    j = pl.program_id(1)
    L = m3_ref.shape[0]
    W = MLSTM_WIDTH
    D = HEAD_DIM

    @pl.when(j == 0)
    def _():
        cbuf[0:8, :] = jnp.zeros((8, W), F32)
        cbuf[8 - (CONV_WIDTH - 1):8, :] = conv0_ref[...]
        c_s[...] = c0_ref[...]
        n_s[...] = n0_ref[...]
        m_s[...] = m0_ref[...]

    blk = m3_ref[...]
    c_in = blk[:, 0:W]
    v_m = blk[:, W:2 * W]
    z = blk[:, 2 * W:3 * W]
    cbuf[8:8 + L, :] = c_in
    acc = jnp.zeros((L, W), F32) + convb_ref[...]
    for tap in range(CONV_WIDTH):
        off = 8 - (CONV_WIDTH - 1) + tap
        acc = acc + cbuf[off:off + L, :] * convw_ref[tap:tap + 1, :]
    xc = _silu(acc)
    convo_ref[...] = cbuf[8 + L - (CONV_WIDTH - 1):8 + L, :]
    cbuf[0:8, :] = cbuf[L:L + 8, :]

    xcb = xc.astype(BF16)
    q_m = _dot(xcb, wq_ref[...])
    k_m = _dot(xcb, wk_ref[...])
    gate_in = jnp.concatenate([q_m, k_m, v_m], axis=1)
    g_col = _dot_hi(gate_in, wif_ref[...]) + bif_ref[...]
    g_row = _dot_nt(wift_ref[...], gate_in, precision=lax.Precision.HIGHEST) + bift_ref[...]
    i_col, lf_col = g_col[:, 0:MLSTM_HEADS], _log_sigmoid(g_col[:, MLSTM_HEADS:])
    i_row, lf_row = g_row[0:MLSTM_HEADS, :], _log_sigmoid(g_row[MLSTM_HEADS:, :])
    rr = lax.broadcasted_iota(jnp.int32, (L, L), 0)
    cc = lax.broadcasted_iota(jnp.int32, (L, L), 1)
    causal = cc <= rr
    tri = causal.astype(F32)
    b_col = _dot_hi(tri, lf_col)
    b_row = _dot_nt(lf_row, tri, precision=lax.Precision.HIGHEST)
    ks = k_m * (D ** -0.5)
    qb = q_m.astype(BF16)
    kb = ks.astype(BF16)
    vb = v_m.astype(BF16)
    lane = lax.broadcasted_iota(jnp.int32, (L, W), 1)
    m_prev_all = m_s[...]
    h_all = jnp.zeros((L, W), F32)
    m_new_list = []
    for h in range(MLSTM_HEADS):
        hs = slice(h * D, (h + 1) * D)
        m_prev = m_prev_all[:, h:h + 1]
        bc = b_col[:, h:h + 1]
        br = b_row[h:h + 1, :]
        ir = i_row[h:h + 1, :]
        ic = i_col[:, h:h + 1]
        d_intra = jnp.where(causal, bc - br + ir, -jnp.inf)
        a_inter = bc + m_prev
        m_t = jnp.maximum(a_inter, jnp.max(d_intra, axis=1, keepdims=True))
        qh, kh, vh = qb[:, hs], kb[:, hs], vb[:, hs]
        s = _dot_nt(qh, kh) * jnp.exp(d_intra - m_t)
        w_inter = jnp.exp(a_inter - m_t)
        c_h = c_s[h * D:(h + 1) * D, :]
        n_h = n_s[h:h + 1, :]
        num = _dot(s.astype(BF16), vh) + w_inter * _dot_nt(qh, c_h.astype(BF16))
        den = jnp.sum(s, axis=1, keepdims=True) + w_inter * jnp.sum(q_m[:, hs] * n_h, axis=1, keepdims=True)
        hh = num / jnp.maximum(jnp.abs(den), jnp.exp(-m_t))
        mu = jnp.mean(hh, axis=1, keepdims=True)
        hc = hh - mu
        var = jnp.mean(hc * hc, axis=1, keepdims=True)
        hn = hc * lax.rsqrt(var + LN_EPS)
        h_all = jnp.where(lane // D == h, jnp.concatenate([hn] * MLSTM_HEADS, axis=1), h_all)
        g_tot = bc[L - 1:L, :]
        a_end = g_tot + m_prev
        d_end_c = g_tot - bc + ic
        m_new = jnp.maximum(a_end, jnp.max(d_end_c, axis=0, keepdims=True))
        w_s = jnp.exp(d_end_c - m_new)
        decay = jnp.exp(a_end - m_new)
        wv = (v_m[:, hs] * w_s).astype(BF16)
        c_s[h * D:(h + 1) * D, :] = decay * c_h + _dot_tn(wv, kh)
        n_s[h:h + 1, :] = decay * n_h + jnp.sum(w_s * ks[:, hs], axis=0, keepdims=True)
        m_new_list.append(m_new)
    m_s[...] = jnp.concatenate(m_new_list, axis=1)
    hm = (h_all * normw_ref[...] + skip_ref[...] * xc) * _silu(z)
    hm_ref[...] = hm.astype(hm_ref.dtype)
    co_ref[...] = c_s[...]
    no_ref[...] = n_s[...]
    mo_ref[...] = m_s[...]


def _mlstm(m3, conv0, c0, n0, m0, conv_w, conv_b, wq_bd, wk_bd, w_if, b_if, norm_w, skip, *, chunk, bb):
    b, t, _ = m3.shape
    W = MLSTM_WIDTH
    full = lambda a: pl.BlockSpec(a.shape, lambda bi, j: (0,) * a.ndim)
    per_b = lambda a: pl.BlockSpec((bb,) + a.shape[1:], lambda bi, j: (bi,) + (0,) * (a.ndim - 1))
    consts = [conv_w, conv_b.reshape(1, W), wq_bd, wk_bd, w_if, w_if.T, b_if.reshape(1, -1),
              b_if.reshape(-1, 1), norm_w.reshape(1, W), skip.reshape(1, W)]
    states = [conv0, c0, n0, m0]
    return pl.pallas_call(
        _mlstm_kernel,
        grid=(b // bb, t // chunk),
        in_specs=[pl.BlockSpec((bb, chunk, 3 * W), lambda bi, j: (bi, j, 0))] + [per_b(s) for s in states]
                 + [full(c) for c in consts],
        out_specs=[pl.BlockSpec((bb, chunk, W), lambda bi, j: (bi, j, 0))] + [per_b(s) for s in states],
        out_shape=[jax.ShapeDtypeStruct((b, t, W), BF16)] + [jax.ShapeDtypeStruct(s.shape, F32) for s in states],
        scratch_shapes=[pltpu.VMEM((bb, chunk + 8, W), F32), pltpu.VMEM((bb,) + c0.shape[1:], F32),
                        pltpu.VMEM((bb,) + n0.shape[1:], F32), pltpu.VMEM((bb,) + m0.shape[1:], F32)],
        compiler_params=_cparams("arbitrary", "arbitrary"),
        name="mlstm",
    )(m3, *states, *consts)


def _combine(o_refs, lse_refs):
    tm = o_refs[0].shape[0]
    lane = lax.broadcasted_iota(jnp.int32, (tm, GROUP_WIDTH), 1)

    def spread(l_ref):
        l2 = l_ref[...]
        out = jnp.zeros((tm, GROUP_WIDTH), F32)
        for h in range(HEADS_PER_GROUP):
            out = jnp.where(lane // HEAD_DIM == h, l2[:, 32 * h:32 * h + 1], out)
        return out

    ls = [spread(r) for r in lse_refs]
    top = jnp.maximum(jnp.maximum(ls[0], ls[1]), ls[2])
    es = [jnp.exp(l - top) for l in ls]
    tot = es[0] + es[1] + es[2]
    return [(o[...].astype(F32) * (e / tot)).astype(BF16) for o, e in zip(o_refs, es)]


def _outproj_kernel(combine, alpha, *refs):
    if combine:
        (o0, o1, o2, l0, l1, l2, hm_ref, x_ref, wo_ref, g_ref, b_ref, wq_ref, x1_ref, qc_ref) = refs
        att = _combine((o0, o1, o2), (l0, l1, l2))
    else:
        (a_ref, hm_ref, x_ref, wo_ref, g_ref, b_ref, wq_ref, x1_ref, qc_ref) = refs
        att = [a_ref[:, g * GROUP_WIDTH:(g + 1) * GROUP_WIDTH].astype(BF16) for g in range(len(ATT_GROUPS))]
    mix = _dot(hm_ref[...].astype(BF16), wo_ref[ATT_WIDTH:, :])
    for g, a in enumerate(att):
        mix = mix + _dot(a, wo_ref[g * GROUP_WIDTH:(g + 1) * GROUP_WIDTH, :])
    x1 = _layer_norm(alpha * x_ref[...] + mix, g_ref[...], b_ref[...])
    x1_ref[...] = x1
    qc_ref[...] = (_dot(x1.astype(BF16), wq_ref[...]) * ((x1.shape[1] // MEM_HEADS) ** -0.5)).astype(qc_ref.dtype)


def _outproj(att_parts, hm, x, w_out, ln_g, ln_b, w_cq, alpha, *, tm):
    n, d = x.shape
    combine = len(att_parts) > 1
    row = lambda a: pl.BlockSpec((tm, a.shape[1]), lambda i: (i, 0))
    full = lambda a: pl.BlockSpec(a.shape, lambda i: (0, 0))
    ins = list(att_parts) + [hm, x]
    consts = [w_out, ln_g.reshape(1, d), ln_b.reshape(1, d), w_cq]
    return pl.pallas_call(
        functools.partial(_outproj_kernel, combine, alpha),
        grid=(n // tm,),
        in_specs=[row(a) for a in ins] + [full(c) for c in consts],
        out_specs=[pl.BlockSpec((tm, d), lambda i: (i, 0))] * 2,
        out_shape=[jax.ShapeDtypeStruct((n, d), F32), jax.ShapeDtypeStruct((n, d), BF16)],
        compiler_params=_cparams("arbitrary"),
        name="outproj_ln1",
    )(*ins, *consts)


def _memkv_kernel(mem_ref, wk_ref, wv_ref, kv_ref):
    d = mem_ref.shape[2]
    mem = mem_ref[0].astype(BF16)
    kv_ref[0, :, 0:d] = _dot(mem, wk_ref[...])
    kv_ref[0, :, d:] = _dot(mem, wv_ref[...])


def _memkv(mem, w_ck, w_cv):
    b, m, d = mem.shape
    return pl.pallas_call(
        _memkv_kernel,
        grid=(b,),
        in_specs=[pl.BlockSpec((1, m, d), lambda i: (i, 0, 0)),
                  pl.BlockSpec(w_ck.shape, lambda i: (0, 0)), pl.BlockSpec(w_cv.shape, lambda i: (0, 0))],
        out_specs=pl.BlockSpec((1, m, 2 * d), lambda i: (i, 0, 0)),
        out_shape=jax.ShapeDtypeStruct((b, m, 2 * d), F32),
        compiler_params=_cparams("arbitrary"),
        name="memory_kv",
    )(mem, w_ck, w_cv)


def _xattn_kernel(q_ref, kv_ref, o_ref):
    d = q_ref.shape[2]
    hd = d // MEM_HEADS
    q = q_ref[0]
    for h in range(MEM_HEADS):
        k = kv_ref[0, :, h * hd:(h + 1) * hd].astype(BF16)
        v = kv_ref[0, :, d + h * hd:d + (h + 1) * hd].astype(BF16)
        s = _dot_nt(q[:, h * hd:(h + 1) * hd], k)
        p = jnp.exp(s - jnp.max(s, axis=1, keepdims=True))
        p = p / jnp.sum(p, axis=1, keepdims=True)
        o_ref[0, :, h * hd:(h + 1) * hd] = _dot(p.astype(BF16), v).astype(o_ref.dtype)


def _xattn(qc, kv, *, tq):
    b, t, d = qc.shape
    m = kv.shape[1]
    return pl.pallas_call(
        _xattn_kernel,
        grid=(b, t // tq),
        in_specs=[pl.BlockSpec((1, tq, d), lambda bi, j: (bi, j, 0)),
                  pl.BlockSpec((1, m, 2 * d), lambda bi, j: (bi, 0, 0))],
        out_specs=pl.BlockSpec((1, tq, d), lambda bi, j: (bi, j, 0)),
        out_shape=jax.ShapeDtypeStruct((b, t, d), BF16),
        compiler_params=_cparams("arbitrary", "arbitrary"),
        name="cross_attention",
    )(qc, kv)


def _xout_kernel(alpha, o_ref, x_ref, w_ref, g_ref, b_ref, y_ref):
    y_ref[...] = _layer_norm(alpha * x_ref[...] + _dot(o_ref[...], w_ref[...]), g_ref[...], b_ref[...])


def _xout(oc, x1, w_co, ln_g, ln_b, alpha, *, tm):
    n, d = x1.shape
    row = pl.BlockSpec((tm, d), lambda i: (i, 0))
    full = lambda a: pl.BlockSpec(a.shape, lambda i: (0, 0))
    consts = [w_co, ln_g.reshape(1, d), ln_b.reshape(1, d)]
    return pl.pallas_call(
        functools.partial(_xout_kernel, alpha),
        grid=(n // tm,),
        in_specs=[row, row] + [full(c) for c in consts],
        out_specs=row,
        out_shape=jax.ShapeDtypeStruct((n, d), F32),
        compiler_params=_cparams("arbitrary"),
        name="cross_out_ln2",
    )(oc, x1, *consts)


def _first_index_of_max(vals, idx, big):
    mx = jnp.max(vals, axis=0, keepdims=True)
    return mx, jnp.min(jnp.where(vals == mx, idx, big), axis=0, keepdims=True)


def _route(xb, wrt_ref, br_ref):
    tm = xb.shape[0]
    n_e = wrt_ref.shape[0]
    per_group = n_e // N_EXPERT_GROUPS
    logits = _dot_nt(wrt_ref[...], xb)
    scores = _sigmoid(logits)
    biased = scores + br_ref[...]
    e_idx = lax.broadcasted_iota(jnp.int32, (n_e, tm), 0).astype(F32)
    g_scores = []
    for g in range(N_EXPERT_GROUPS):
        sub = biased[g * per_group:(g + 1) * per_group, :]
        sidx = lax.broadcasted_iota(jnp.int32, (per_group, tm), 0).astype(F32)
        m1, a1 = _first_index_of_max(sub, sidx, per_group)
        m2 = jnp.max(jnp.where(sidx == a1, -jnp.inf, sub), axis=0, keepdims=True)
        g_scores.append(m1 + m2)
    gs = jnp.concatenate(g_scores, axis=0)
    g_idx = lax.broadcasted_iota(jnp.int32, (N_EXPERT_GROUPS, tm), 0).astype(F32)
    g_sel = jnp.zeros((N_EXPERT_GROUPS, tm), F32)
    work = gs
    for _ in range(TOPK_GROUPS):
        _, a = _first_index_of_max(work, g_idx, N_EXPERT_GROUPS)
        hit = g_idx == a
        g_sel = jnp.where(hit, 1.0, g_sel)
        work = jnp.where(hit, -jnp.inf, work)
    e_mask = jnp.concatenate(
        [jnp.broadcast_to(g_sel[g:g + 1, :], (per_group, tm)) for g in range(N_EXPERT_GROUPS)], axis=0)
    work = jnp.where(e_mask > 0.5, biased, -jnp.inf)
    sel = jnp.zeros((n_e, tm), F32)
    picks = []
    for _ in range(TOP_K):
        cand = jnp.where(sel > 0.5, -jnp.inf, work)
        mx = jnp.max(cand, axis=0, keepdims=True)
        a = jnp.min(jnp.where((cand == mx) & (sel < 0.5), e_idx, float(n_e)), axis=0, keepdims=True)
        pick = jnp.where(e_idx == a, 1.0, 0.0)
        picks.append(pick)
        sel = sel + pick
    w_sel = sel * scores
    gates_t = w_sel / jnp.sum(w_sel, axis=0, keepdims=True) * ROUTED_SCALE
    return gates_t, sel, picks


ROW_CHUNK = 16
TILE_CHUNKS = 16
TILE_ROWS = ROW_CHUNK * TILE_CHUNKS
TOKEN_BLOCK = 256


def _slab_rows(tb, n_e):
    return -(-(TOP_K * tb + n_e * (ROW_CHUNK - 1)) // 128) * 128


def _dispatch_kernel(x_ref, wrt_ref, br_ref, xs_ref, pwt_ref, meta_ref):
    tb = x_ref.shape[0]
    n_e = wrt_ref.shape[0]
    slab = xs_ref.shape[1]
    xb = x_ref[...].astype(BF16)
    gates_t, sel, picks = _route(xb, wrt_ref, br_ref)
    cnt = jnp.sum(sel, axis=1, keepdims=True)
    padded = jnp.floor((cnt + (ROW_CHUNK - 1)) * (1.0 / ROW_CHUNK)) * ROW_CHUNK
    padded_b = jnp.broadcast_to(padded, (n_e, 128))
    er = lax.broadcasted_iota(jnp.int32, (n_e, n_e), 0)
    ec = lax.broadcasted_iota(jnp.int32, (n_e, n_e), 1)
    off_b = _dot((ec < er).astype(BF16), padded_b.astype(BF16))
    tr = lax.broadcasted_iota(jnp.int32, (tb, tb), 0)
    tc = lax.broadcasted_iota(jnp.int32, (tb, tb), 1)
    rank = _dot(sel.astype(BF16), (tr < tc).astype(BF16))
    dest = off_b[:, 0:1] + rank
    dest_k = [jnp.sum(p * dest, axis=0, keepdims=True) for p in picks]
    gate_k = [jnp.sum(p * gates_t, axis=0, keepdims=True) for p in picks]
    meta_ref[0, 0] = padded_b
    meta_ref[0, 1] = off_b

    step = 512
    for r0 in range(0, slab, step):
        rows = lax.broadcasted_iota(jnp.int32, (step, tb), 0).astype(F32) + float(r0)
        onehot = jnp.zeros((step, tb), F32)
        for dk in dest_k:
            onehot = onehot + jnp.where(rows == dk, 1.0, 0.0)
        xs_ref[0, r0:r0 + step, :] = _dot(onehot.astype(BF16), xb).astype(xs_ref.dtype)

    stacked = jnp.concatenate(dest_k + [jnp.zeros((8 - TOP_K, tb), F32)] + gate_k
                              + [jnp.zeros((128 - 8 - TOP_K, tb), F32)], axis=0)
    cols = jnp.transpose(stacked)
    for c0 in range(0, slab, step):
        lanes = lax.broadcasted_iota(jnp.int32, (tb, step), 1).astype(F32) + float(c0)
        w = jnp.zeros((tb, step), F32)
        for k in range(TOP_K):
            w = w + jnp.where(lanes == cols[:, k:k + 1], cols[:, 8 + k:9 + k], 0.0)
        pwt_ref[0, :, c0:c0 + step] = w.astype(pwt_ref.dtype)


def _dispatch(x, w_router_t, b_router):
    n, d = x.shape
    n_e = w_router_t.shape[0]
    tb = TOKEN_BLOCK
    nblk = n // tb
    slab = _slab_rows(tb, n_e)
    assert slab % 512 == 0
    return pl.pallas_call(
        _dispatch_kernel,
        grid=(nblk,),
        in_specs=[pl.BlockSpec((tb, d), lambda i: (i, 0)),
                  pl.BlockSpec(w_router_t.shape, lambda i: (0, 0)),
                  pl.BlockSpec((n_e, 1), lambda i: (0, 0))],
        out_specs=[pl.BlockSpec((1, slab, d), lambda i: (i, 0, 0)),
                   pl.BlockSpec((1, tb, slab), lambda i: (i, 0, 0)),
                   pl.BlockSpec((1, 2, n_e, 128), lambda i: (i, 0, 0, 0))],
        out_shape=[jax.ShapeDtypeStruct((nblk, slab, d), BF16),
                   jax.ShapeDtypeStruct((nblk, tb, slab), BF16),
                   jax.ShapeDtypeStruct((nblk, 2, n_e, 128), F32)],
        compiler_params=_cparams("arbitrary"),
        name="moe_dispatch",
    )(x, w_router_t, b_router.reshape(n_e, 1))


def _tile_schedule(meta, slab):
    nblk, _, n_e, _ = meta.shape
    padded = meta[:, 0, :, 0].astype(jnp.int32)
    off = meta[:, 1, :, 0].astype(jnp.int32)
    pieces = (padded // ROW_CHUNK).T
    cum = jnp.cumsum(pieces, axis=1)
    per_expert = cum[:, -1]
    tiles_e = (per_expert + TILE_CHUNKS - 1) // TILE_CHUNKS
    tile_end = jnp.cumsum(tiles_e)
    n_tiles = tile_end[-1]
    t_max = -(-(TOP_K * nblk * TOKEN_BLOCK + nblk * n_e * (ROW_CHUNK - 1)) // TILE_ROWS) + n_e + 2
    t = jnp.arange(t_max, dtype=jnp.int32)
    live = t < n_tiles
    e_raw = jnp.minimum(jnp.sum(t[:, None] >= tile_end[None, :], axis=1), n_e - 1).astype(jnp.int32)
    e_last = e_raw[jnp.maximum(n_tiles - 1, 0)]
    tile_e = jnp.where(live, e_raw, e_last)
    local = t - (tile_end - tiles_e)[tile_e]
    ordinal = local[:, None] * TILE_CHUNKS + jnp.arange(TILE_CHUNKS, dtype=jnp.int32)[None, :]
    valid = (ordinal < per_expert[tile_e][:, None]) & live[:, None]
    cum_t = cum[tile_e]
    blk = jnp.minimum(jnp.sum(ordinal[:, :, None] >= cum_t[:, None, :], axis=2), nblk - 1).astype(jnp.int32)
    before = jnp.take_along_axis(cum_t - pieces[tile_e], blk, axis=1)
    seg_off = off[blk, tile_e[:, None]] // ROW_CHUNK
    src = blk * (slab // ROW_CHUNK) + seg_off + (ordinal - before)
    src = jnp.where(valid, src, 0).astype(jnp.int32)
    n_valid = jnp.sum(valid, axis=1).astype(jnp.int32)
    return tile_e, n_valid, src.reshape(-1), n_tiles.reshape(1).astype(jnp.int32)


def _grouped_kernel(te_ref, nv_ref, src_ref, nt_ref, xs_hbm, wg_ref, wu_ref, wd_ref, ys_hbm,
                    lhs, obuf, sem_in, sem_out):
    t = pl.program_id(0)
    n_tiles = nt_ref[0]

    def piece(ref, tile, s):
        row = pl.multiple_of(src_ref[tile * TILE_CHUNKS + s] * ROW_CHUNK, ROW_CHUNK)
        return ref.at[pl.ds(row, ROW_CHUNK)]

    def local(buf, slot, s):
        return buf.at[slot, pl.ds(pl.multiple_of(s * ROW_CHUNK, ROW_CHUNK), ROW_CHUNK)]

    def gather(tile, s):
        slot = tile % 2
        return pltpu.make_async_copy(piece(xs_hbm, tile, s), local(lhs, slot, s), sem_in.at[slot])

    def scatter(tile, s):
        slot = tile % 2
        return pltpu.make_async_copy(local(obuf, slot, s), piece(ys_hbm, tile, s), sem_out.at[slot])

    def for_pieces(tile, fn):
        def body(s, carry):
            fn(tile, s)
            return carry
        lax.fori_loop(0, nv_ref[tile], body, 0)

    @pl.when((t == 0) & (n_tiles > 0))
    def _():
        for_pieces(t, lambda tile, s: gather(tile, s).start())

    @pl.when(t + 1 < n_tiles)
    def _():
        for_pieces(t + 1, lambda tile, s: gather(tile, s).start())

    @pl.when((t >= 2) & (t - 2 < n_tiles))
    def _():
        for_pieces(t - 2, lambda tile, s: scatter(tile, s).wait())

    @pl.when(t < n_tiles)
    def _():
        slot = t % 2
        for_pieces(t, lambda tile, s: gather(tile, s).wait())
        x = lhs[slot]
        h = _silu(_dot(x, wg_ref[0])) * _dot(x, wu_ref[0])
        obuf[slot] = _dot(h.astype(BF16), wd_ref[0]).astype(obuf.dtype)
        for_pieces(t, lambda tile, s: scatter(tile, s).start())


def _grouped_experts(xs, tile_e, n_valid, src, n_tiles, w_gate, w_up, w_down):
    rows, d = xs.shape
    n_e, _, ff = w_gate.shape
    t_max = tile_e.shape[0]
    grid_spec = pltpu.PrefetchScalarGridSpec(
        num_scalar_prefetch=4,
        grid=(t_max,),
        in_specs=[pl.BlockSpec(memory_space=pl.ANY),
                  pl.BlockSpec((1, d, ff), lambda t, te, nv, sr, nt: (te[t], 0, 0)),
                  pl.BlockSpec((1, d, ff), lambda t, te, nv, sr, nt: (te[t], 0, 0)),
                  pl.BlockSpec((1, ff, d), lambda t, te, nv, sr, nt: (te[t], 0, 0))],
        out_specs=pl.BlockSpec(memory_space=pl.ANY),
        scratch_shapes=[pltpu.VMEM((2, TILE_ROWS, d), BF16), pltpu.VMEM((2, TILE_ROWS, d), BF16),
                        pltpu.SemaphoreType.DMA((2,)), pltpu.SemaphoreType.DMA((2,))],
    )
    return pl.pallas_call(
        _grouped_kernel,
        grid_spec=grid_spec,
        out_shape=jax.ShapeDtypeStruct((rows, d), BF16),
        input_output_aliases={4: 0},
        compiler_params=_cparams("arbitrary"),
        name="moe_grouped_experts",
    )(tile_e, n_valid, src, n_tiles, xs, w_gate, w_up, w_down)


def _moe_out_kernel(alpha, pwt_ref, ys_ref, x_ref, sg_ref, su_ref, sd_ref, g_ref, b_ref, y_ref):
    routed = _dot(pwt_ref[0], ys_ref[0])
    x = x_ref[...]
    xb = x.astype(BF16)
    hs = _silu(_dot(xb, sg_ref[...])) * _dot(xb, su_ref[...])
    shared = _dot(hs.astype(BF16), sd_ref[...])
    y_ref[...] = _layer_norm(alpha * x + (routed + shared), g_ref[...], b_ref[...])


def _moe_out(pwt, ys, x, ws_gate, ws_up, ws_down, ln_g, ln_b, alpha):
    n, d = x.shape
    nblk, tb, slab = pwt.shape
    consts = [ws_gate, ws_up, ws_down, ln_g.reshape(1, d), ln_b.reshape(1, d)]
    return pl.pallas_call(
        functools.partial(_moe_out_kernel, alpha),
        grid=(nblk,),
        in_specs=[pl.BlockSpec((1, tb, slab), lambda i: (i, 0, 0)),
                  pl.BlockSpec((1, slab, d), lambda i: (i, 0, 0)),
                  pl.BlockSpec((tb, d), lambda i: (i, 0))]
                 + [pl.BlockSpec(c.shape, lambda i: (0, 0)) for c in consts],
        out_specs=pl.BlockSpec((tb, d), lambda i: (i, 0)),
        out_shape=jax.ShapeDtypeStruct((n, d), F32),
        compiler_params=_cparams("arbitrary"),
        name="moe_combine",
    )(pwt, ys.reshape(nblk, slab, d), x, *consts)


def _moe(x, w_router_t, b_router, w_gate, w_up, w_down, ws_gate, ws_up, ws_down, ln_g, ln_b, alpha):
    xs, pwt, meta = _dispatch(x, w_router_t, b_router)
    nblk, slab, d = xs.shape
    tile_e, n_valid, src, n_tiles = _tile_schedule(meta, slab)
    ys = _grouped_experts(xs.reshape(nblk * slab, d), tile_e, n_valid, src, n_tiles, w_gate, w_up, w_down)
    return _moe_out(pwt, ys, x, ws_gate, ws_up, ws_down, ln_g, ln_b, alpha)


def _rope_tables(pos):
    half = HEAD_DIM // 2
    inv_freq = ROPE_THETA ** (-jnp.arange(half, dtype=F32) / half)
    ang = pos.astype(F32)[:, None] * inv_freq[None, :]
    cos, sin = jnp.cos(ang), jnp.sin(ang)
    return jnp.tile(jnp.concatenate([cos, cos], axis=1), (1, 2)), jnp.tile(jnp.concatenate([-sin, sin], axis=1), (1, 2))


def _block_diag(w):
    h, d, _ = w.shape
    out = jnp.zeros((h * d, h * d), w.dtype)
    for i in range(h):
        out = out.at[i * d:(i + 1) * d, i * d:(i + 1) * d].set(w[i])
    return out


def _pick(n, pref):
    return pref if n % pref == 0 else n


def kernel(x_prompt, x_sample, mem_prompt, cache_win128, cache_win512, cache_win2048, cache_mem_kv, state_conv, state_C, state_n, state_m, w_in, conv_w, conv_b, wq_m, wk_m, w_if, b_if, mh_norm_w, skip_m, w_out, ln1_g, ln1_b, w_cq, w_ck, w_cv, w_co, ln2_g, ln2_b, w_router, b_router, w_gate, w_up, w_down, ws_gate, ws_up, ws_down, ln3_g, ln3_b):
    depth = w_in.shape[0]
    assert depth == 1
    alpha = float((2 * depth) ** 0.25)
    bp, seq, d = x_prompt.shape
    bs, dec, _ = x_sample.shape
    assert seq % ATT_GROUPS[-1][0] == 0
    for c, (window, _) in zip((cache_win128, cache_win512, cache_win2048), ATT_GROUPS):
        assert c.shape[2] == window

    l = 0
    bf = lambda a: a.astype(BF16)
    w_in_b, w_out_b = bf(w_in[l]), bf(w_out[l])
    w_cq_b, w_ck_b, w_cv_b, w_co_b = bf(w_cq[l]), bf(w_ck[l]), bf(w_cv[l]), bf(w_co[l])
    wq_bd, wk_bd = bf(_block_diag(wq_m[l])), bf(_block_diag(wk_m[l]))
    w_router_t = bf(w_router[l].T)
    w_gate_b, w_up_b, w_down_b = bf(w_gate[l]), bf(w_up[l]), bf(w_down[l])
    ws_gate_b, ws_up_b, ws_down_b = bf(ws_gate[l]), bf(ws_up[l]), bf(ws_down[l])

    def tail_of_layer(x1, qc, kv, batch, t, tm, tq):
        n = batch * t
        oc = _xattn(qc.reshape(batch, t, d), kv, tq=tq).reshape(n, d)
        x2 = _xout(oc, x1, w_co_b, ln2_g[l], ln2_b[l], alpha, tm=tm)
        return _moe(x2, w_router_t, b_router[l], w_gate_b, w_up_b, w_down_b, ws_gate_b, ws_up_b, ws_down_b,
                    ln3_g[l], ln3_b[l], alpha)

    def mlstm(m3, batch, t, states, chunk, bb):
        return _mlstm(m3.reshape(batch, t, -1), *states, conv_w[l], conv_b[l], wq_bd, wk_bd, w_if[l], b_if[l],
                      mh_norm_w[l], skip_m[l], chunk=chunk, bb=bb)

    np_ = bp * seq
    xp = x_prompt.reshape(np_, d)
    cos_p, sin_p = _rope_tables(jnp.arange(seq))
    q, k, v, m3, t128, t512, t2048 = _inproj(xp, w_in_b, cos_p, sin_p, seq=seq, tm=512, qkv_dtype=BF16,
                                              with_tails=True)
    parts, lses = [], []
    for g, (_, dil) in enumerate(ATT_GROUPS):
        o, lse = _win_attn(q.reshape(bp, seq, -1), k.reshape(bp, seq, -1), v.reshape(bp, seq, -1), g, dil)
        parts.append(o)
        lses.append(lse)
    zeros_p = [jnp.zeros((bp, CONV_WIDTH - 1, MLSTM_WIDTH), F32), jnp.zeros((bp, MLSTM_WIDTH, HEAD_DIM), F32),
               jnp.zeros((bp, MLSTM_HEADS, HEAD_DIM), F32), jnp.zeros((bp, 1, MLSTM_HEADS), F32)]
    hm_p, p_conv, p_c, p_n, p_m = mlstm(m3, bp, seq, zeros_p, 256, bp)
    x1, qc = _outproj(parts + lses, hm_p.reshape(np_, -1), xp, w_out_b, ln1_g[l], ln1_b[l], w_cq_b, alpha, tm=256)
    kv_p = _memkv(mem_prompt, w_ck_b, w_cv_b)
    y_p = tail_of_layer(x1, qc, kv_p, bp, seq, 256, 512)

    ns = bs * dec
    xs = x_sample.reshape(ns, d)
    cos_s, sin_s = _rope_tables(jnp.tile(PAST_LEN + jnp.arange(dec), bs))
    qs, ks, vs, m3s = _inproj(xs, w_in_b, cos_s, sin_s, seq=ns, tm=ns, qkv_dtype=F32, with_tails=False)
    caches = [c[l].reshape(bs, c.shape[2], 2 * GROUP_WIDTH) for c in (cache_win128, cache_win512, cache_win2048)]
    att_s, s128, s512, s2048 = _dec_attn(qs.reshape(bs, dec, -1), ks.reshape(bs, dec, -1), vs.reshape(bs, dec, -1),
                                         caches)
    states_s = [state_conv[l], state_C[l].reshape(bs, MLSTM_WIDTH, HEAD_DIM), state_n[l],
                state_m[l].reshape(bs, 1, MLSTM_HEADS)]
    hm_s, s_conv, s_c, s_n, s_m = mlstm(m3s, bs, dec, states_s, dec, 4 if bs % 4 == 0 else 1)
    x1s, qcs = _outproj([att_s.reshape(ns, -1)], hm_s.reshape(ns, -1), xs, w_out_b, ln1_g[l], ln1_b[l], w_cq_b,
                        alpha, tm=ns)
    kv_s = cache_mem_kv[l].reshape(bs, cache_mem_kv.shape[2], 2 * d)
    y_s = tail_of_layer(x1s, qcs, kv_s, bs, dec, ns, dec)

    win_shape = lambda a, b_: a.reshape(1, b_, a.shape[1], 2, HEADS_PER_GROUP, HEAD_DIM)
    return (y_p.reshape(bp, seq, d), y_s.reshape(bs, dec, d),
            win_shape(t128, bp), win_shape(t512, bp), win_shape(t2048, bp),
            kv_p.reshape(1, bp, mem_prompt.shape[1], 2, MEM_HEADS, d // MEM_HEADS),
            p_conv[None], p_c.reshape(1, bp, MLSTM_HEADS, HEAD_DIM, HEAD_DIM), p_n[None],
            p_m.reshape(1, bp, MLSTM_HEADS),
            win_shape(s128, bs), win_shape(s512, bs), win_shape(s2048, bs),
            s_conv[None], s_c.reshape(1, bs, MLSTM_HEADS, HEAD_DIM, HEAD_DIM), s_n[None],
            s_m.reshape(1, bs, MLSTM_HEADS))
```

```python
import functools
import math

import jax
import jax.numpy as jnp
from jax import lax
from jax.experimental import pallas as pl
from jax.experimental.pallas import tpu as pltpu

F32 = jnp.float32
BF16 = jnp.bfloat16

HEAD_DIM = 64
ATT_GROUPS = ((128, 1), (512, 4), (2048, 16))
HEADS_PER_GROUP = 4
GROUP_WIDTH = HEADS_PER_GROUP * HEAD_DIM
ATT_WIDTH = GROUP_WIDTH * len(ATT_GROUPS)
N_KEYS = 129
ROPE_THETA = 10000.0
PAST_LEN = 8192
MLSTM_HEADS = 4
MLSTM_WIDTH = MLSTM_HEADS * HEAD_DIM
CONV_WIDTH = 4
MEM_HEADS = 4
N_EXPERT_GROUPS = 8
TOPK_GROUPS = 4
TOP_K = 6
ROUTED_SCALE = 2.5
LN_EPS = 1e-5
NEG = -1e30
VMEM_LIMIT = 56 * 1024 * 1024


def _cparams(*sem):
    return pltpu.CompilerParams(dimension_semantics=sem, vmem_limit_bytes=VMEM_LIMIT)


def _dot(a, b):
    return jnp.dot(a, b, preferred_element_type=F32)


def _dot_nt(a, b, precision=None):
    return lax.dot_general(a, b, (((1,), (1,)), ((), ())), precision=precision,
                           preferred_element_type=F32)


def _dot_tn(a, b):
    return lax.dot_general(a, b, (((0,), (0,)), ((), ())), preferred_element_type=F32)


def _dot_hi(a, b):
    return jnp.dot(a, b, precision=lax.Precision.HIGHEST, preferred_element_type=F32)


def _layer_norm(x, g, b):
    mu = jnp.mean(x, axis=-1, keepdims=True)
    xc = x - mu
    var = jnp.mean(xc * xc, axis=-1, keepdims=True)
    return xc * lax.rsqrt(var + LN_EPS) * g + b


def _sigmoid(x):
    return 1.0 / (1.0 + jnp.exp(-x))


def _silu(x):
    return x * _sigmoid(x)


def _log_sigmoid(x):
    return jnp.minimum(x, 0.0) - jnp.log(1.0 + jnp.exp(-jnp.abs(x)))


def _project_qkv(x_ref, w_ref, cos_ref, sin_ref):
    tm = x_ref.shape[0]
    x = x_ref[...].astype(BF16)
    cos = jnp.concatenate([cos_ref[...]] * (ATT_WIDTH // 128), axis=1)
    sin = jnp.concatenate([sin_ref[...]] * (ATT_WIDTH // 128), axis=1)
    lane = lax.broadcasted_iota(jnp.int32, (tm, ATT_WIDTH), 1)
    first_half = (lane % HEAD_DIM) < (HEAD_DIM // 2)

    def rope(t):
        fwd = pltpu.roll(t, ATT_WIDTH - HEAD_DIM // 2, 1)
        bwd = pltpu.roll(t, HEAD_DIM // 2, 1)
        return t * cos + jnp.where(first_half, fwd, bwd) * sin

    q = rope(_dot(x, w_ref[:, 0:ATT_WIDTH])) * (HEAD_DIM ** -0.5)
    k = rope(_dot(x, w_ref[:, ATT_WIDTH:2 * ATT_WIDTH]))
    v = _dot(x, w_ref[:, 2 * ATT_WIDTH:3 * ATT_WIDTH])
    return q, k, v, _dot(x, w_ref[:, 3 * ATT_WIDTH:])


def _inproj_kernel(x_ref, w_ref, cos_ref, sin_ref, q_ref, k_ref, v_ref, m_ref):
    q, k, v, m = _project_qkv(x_ref, w_ref, cos_ref, sin_ref)
    q_ref[...] = q.astype(q_ref.dtype)
    k_ref[...] = k.astype(k_ref.dtype)
    v_ref[...] = v.astype(v_ref.dtype)
    m_ref[...] = m


def _inproj_prompt_kernel(x_ref, w_ref, cos_ref, sin_ref, *refs):
    n_g = len(ATT_GROUPS)
    qkv_refs, m_ref, tails, scr = refs[:3 * n_g], refs[3 * n_g], refs[3 * n_g + 1:4 * n_g + 1], refs[4 * n_g + 1]
    tm = x_ref.shape[0]
    q, k, v, m = _project_qkv(x_ref, w_ref, cos_ref, sin_ref)
    m_ref[...] = m
    for g, (_, dil) in enumerate(ATT_GROUPS):
        cols = slice(g * GROUP_WIDTH, (g + 1) * GROUP_WIDTH)
        for a, val in enumerate((q, k, v)):
            o_ref = qkv_refs[3 * g + a]
            if dil == 1:
                o_ref[0, 0] = val[:, cols].astype(o_ref.dtype)
                continue
            for c in range(GROUP_WIDTH // 128):
                scr[c] = val[:, g * GROUP_WIDTH + c * 128:g * GROUP_WIDTH + (c + 1) * 128]
            for r in range(dil):
                for c in range(GROUP_WIDTH // 128):
                    o_ref[0, r, :, c * 128:(c + 1) * 128] = scr[c, pl.ds(r, tm // dil, stride=dil), :].astype(o_ref.dtype)
        t_ref = tails[g]
        rows = t_ref.shape[1]
        t_ref[0, :, 0:GROUP_WIDTH] = k[tm - rows:, cols]
        t_ref[0, :, GROUP_WIDTH:] = v[tm - rows:, cols]


def _inproj_prompt(x, w_in, cos, sin, *, seq, tm):
    n, d = x.shape
    tiles_per_seq = seq // tm
    batch = n // seq
    out_shape, out_specs = [], []
    for _, dil in ATT_GROUPS:
        for _ in range(3):
            out_shape.append(jax.ShapeDtypeStruct((batch, dil, seq // dil, GROUP_WIDTH), BF16))
            out_specs.append(pl.BlockSpec((1, dil, tm // dil, GROUP_WIDTH),
                                          lambda i: (i // tiles_per_seq, 0, i % tiles_per_seq, 0)))
    m_width = w_in.shape[1] - 3 * ATT_WIDTH
    out_shape.append(jax.ShapeDtypeStruct((n, m_width), F32))
    out_specs.append(pl.BlockSpec((tm, m_width), lambda i: (i, 0)))
    for window, _ in ATT_GROUPS:
        rows = min(window, tm)
        first = tiles_per_seq - window // rows if window > rows else tiles_per_seq - 1
        out_shape.append(jax.ShapeDtypeStruct((batch, min(window, seq), 2 * GROUP_WIDTH), F32))
        out_specs.append(pl.BlockSpec(
            (1, rows, 2 * GROUP_WIDTH),
            lambda i, first=first: (i // tiles_per_seq, jnp.maximum(i % tiles_per_seq - first, 0), 0)))
    return pl.pallas_call(
        _inproj_prompt_kernel,
        grid=(n // tm,),
        in_specs=[pl.BlockSpec((tm, d), lambda i: (i, 0)),
                  pl.BlockSpec(w_in.shape, lambda i: (0, 0)),
                  pl.BlockSpec((tm, 128), lambda i: (i % tiles_per_seq, 0)),
                  pl.BlockSpec((tm, 128), lambda i: (i % tiles_per_seq, 0))],
        out_specs=out_specs,
        out_shape=out_shape,
        scratch_shapes=[pltpu.VMEM((GROUP_WIDTH // 128, tm, 128), F32)],
        compiler_params=_cparams("arbitrary"),
        name="inproj_rope_prompt",
    )(x, w_in, cos, sin)


def _inproj(x, w_in, cos, sin, *, tm):
    n, d = x.shape
    out_shape = [jax.ShapeDtypeStruct((n, ATT_WIDTH), F32)] * 3 + [
        jax.ShapeDtypeStruct((n, w_in.shape[1] - 3 * ATT_WIDTH), F32)]
    row_spec = lambda w: pl.BlockSpec((tm, w), lambda i: (i, 0))
    out_specs = [row_spec(ATT_WIDTH)] * 3 + [row_spec(w_in.shape[1] - 3 * ATT_WIDTH)]
    tiles_per_seq = cos.shape[0] // tm
    return pl.pallas_call(
        _inproj_kernel,
        grid=(n // tm,),
        in_specs=[row_spec(d),
                  pl.BlockSpec(w_in.shape, lambda i: (0, 0)),
                  pl.BlockSpec((tm, 128), lambda i: (i % tiles_per_seq, 0)),
                  pl.BlockSpec((tm, 128), lambda i: (i % tiles_per_seq, 0))],
        out_specs=out_specs,
        out_shape=out_shape,
        compiler_params=_cparams("arbitrary"),
        name="inproj_rope",
    )(x, w_in, cos, sin)


def _win_attn_kernel(q_ref, kp_ref, kc_ref, vp_ref, vc_ref, o_ref, lse_ref):
    j = pl.program_id(2)
    tq = q_ref.shape[1]
    sub = N_KEYS - 1
    assert kp_ref.shape[1] == sub and tq % sub == 0
    k_all = jnp.concatenate([kp_ref[0], kc_ref[0]], axis=0)
    v_all = jnp.concatenate([vp_ref[0], vc_ref[0]], axis=0)
    row = lax.broadcasted_iota(jnp.int32, (sub, 2 * sub), 0)
    col = lax.broadcasted_iota(jnp.int32, (sub, 2 * sub), 1)
    band = (col >= row) & (col <= row + sub)
    lane = lax.broadcasted_iota(jnp.int32, (sub, GROUP_WIDTH), 1)
    lane_l = lax.broadcasted_iota(jnp.int32, (sub, 128), 1)
    for i in range(tq // sub):
        q = q_ref[0, i * sub:(i + 1) * sub, :]
        kk = k_all[i * sub:(i + 2) * sub]
        vv = v_all[i * sub:(i + 2) * sub]
        valid = band & ((j > 0) | (col >= sub)) if i == 0 else band
        o_acc = jnp.zeros((sub, GROUP_WIDTH), F32)
        lse_acc = jnp.zeros((sub, 128), F32)
        for h in range(HEADS_PER_GROUP):
            qh = jnp.where(lane // HEAD_DIM == h, q, jnp.zeros_like(q))
            s = jnp.where(valid, _dot_nt(qh, kk), NEG)
            m = jnp.max(s, axis=1, keepdims=True)
            p = jnp.exp(s - m)
            l = jnp.sum(p, axis=1, keepdims=True)
            o = _dot(p.astype(BF16), vv) / l
            o_acc = jnp.where(lane // HEAD_DIM == h, o, o_acc)
            lse_acc = jnp.where(lane_l // 32 == h, m + jnp.log(l), lse_acc)
        o_ref[0, i * sub:(i + 1) * sub, :] = o_acc.astype(o_ref.dtype)
        lse_ref[0, i * sub:(i + 1) * sub, :] = lse_acc


def _win_attn(q, k, v, *, tq=512):
    b, dil, ts, _ = q.shape
    tq = min(tq, ts)
    sub = N_KEYS - 1
    cur = pl.BlockSpec((None, 1, tq, GROUP_WIDTH), lambda bi, r, j: (bi, r, j, 0))
    prev = pl.BlockSpec((None, 1, sub, GROUP_WIDTH),
                        lambda bi, r, j: (bi, r, jnp.maximum(j * (tq // sub) - 1, 0), 0))
    return pl.pallas_call(
        _win_attn_kernel,
        grid=(b, dil, ts // tq),
        in_specs=[cur, prev, cur, prev, cur],
        out_specs=[pl.BlockSpec((None, 1, tq, GROUP_WIDTH), lambda bi, r, j: (bi, r, j, 0)),
                   pl.BlockSpec((None, 1, tq, 128), lambda bi, r, j: (bi, r, j, 0))],
        out_shape=[jax.ShapeDtypeStruct((b, dil, ts, GROUP_WIDTH), BF16),
                   jax.ShapeDtypeStruct((b, dil, ts, 128), F32)],
        compiler_params=_cparams("arbitrary", "arbitrary", "arbitrary"),
        name="window_attention",
    )(q, k, k, v, v)


def _dec_attn_kernel(q_ref, k_ref, v_ref, c0_ref, c1_ref, c2_ref, att_ref, o0_ref, o1_ref, o2_ref,
                     e0_ref, e1_ref, e2_ref):
    t_new = q_ref.shape[1]
    q = q_ref[0]
    k_new = k_ref[0]
    v_new = v_ref[0]
    lane = lax.broadcasted_iota(jnp.int32, (8, GROUP_WIDTH), 1)
    sub = lax.broadcasted_iota(jnp.int32, (8, GROUP_WIDTH), 0)
    head_sel = (lane // HEAD_DIM) == sub
    outs = [[None] * t_new for _ in ATT_GROUPS]
    lses = [[None] * t_new for _ in ATT_GROUPS]
    for g, ((window, dil), c_ref, o_ref, e_ref) in enumerate(
            zip(ATT_GROUPS, (c0_ref, c1_ref, c2_ref), (o0_ref, o1_ref, o2_ref), (e0_ref, e1_ref, e2_ref))):
        w = c_ref.shape[1]
        cols = slice(g * GROUP_WIDTH, (g + 1) * GROUP_WIDTH)
        kv_new = jnp.concatenate([k_new[:, cols], v_new[:, cols]], axis=1)
        n_chunk = e_ref.shape[0]
        for c in range(n_chunk):
            e_ref[c, 0:w, :] = c_ref[0, :, c * 128:(c + 1) * 128]
            e_ref[c, w:w + t_new, :] = kv_new[:, c * 128:(c + 1) * 128]
            o_ref[0, :, c * 128:(c + 1) * 128] = e_ref[c, t_new:w + t_new, :]
        for t in range(t_new):
            past = jnp.concatenate(
                [e_ref[c, pl.ds(t, N_KEYS - 1, stride=dil), :] for c in range(n_chunk)], axis=1)
            qm = jnp.where(head_sel, jnp.broadcast_to(q[t:t + 1, cols], (8, GROUP_WIDTH)), 0.0)
            s = _dot_nt(qm.astype(BF16), past[:, 0:GROUP_WIDTH].astype(BF16))
            k_self = kv_new[t:t + 1, 0:GROUP_WIDTH]
            v_self = kv_new[t:t + 1, GROUP_WIDTH:]
            s_self = jnp.sum(qm * k_self, axis=1, keepdims=True)
            m = jnp.maximum(jnp.max(s, axis=1, keepdims=True), s_self)
            p = jnp.exp(s - m)
            p_self = jnp.exp(s_self - m)
            l = jnp.sum(p, axis=1, keepdims=True) + p_self
            o = _dot(p.astype(BF16), past[:, GROUP_WIDTH:].astype(BF16))
            o = (o + p_self * v_self) / l
            outs[g][t] = o
            lses[g][t] = m + jnp.log(l)
    for t in range(t_new):
        top = jnp.maximum(jnp.maximum(lses[0][t], lses[1][t]), lses[2][t])
        es = [jnp.exp(lses[g][t] - top) for g in range(len(ATT_GROUPS))]
        tot = es[0] + es[1] + es[2]
        for g in range(len(ATT_GROUPS)):
            weighted = jnp.where(head_sel, outs[g][t] * (es[g] / tot), 0.0)
            att_ref[0, t:t + 1, g * GROUP_WIDTH:(g + 1) * GROUP_WIDTH] = jnp.sum(weighted, axis=0, keepdims=True)


def _dec_attn(q, k, v, caches):
    b, t_new, _ = q.shape
    tok = pl.BlockSpec((1, t_new, ATT_WIDTH), lambda i: (i, 0, 0))
    cspec = [pl.BlockSpec((1,) + c.shape[1:], lambda i: (i, 0, 0)) for c in caches]
    return pl.pallas_call(
        _dec_attn_kernel,
        grid=(b,),
        in_specs=[tok, tok, tok] + cspec,
        out_specs=[tok] + cspec,
        out_shape=[jax.ShapeDtypeStruct(q.shape, F32)] + [jax.ShapeDtypeStruct(c.shape, F32) for c in caches],
        scratch_shapes=[pltpu.VMEM((c.shape[2] // 128, c.shape[1] + t_new, 128), F32) for c in caches],
        compiler_params=_cparams("arbitrary"),
        name="decode_attention",
    )(q, k, v, *caches)


def _mlstm_kernel(m3_ref, conv0_ref, c0_ref, n0_ref, m0_ref, *rest):
    consts, (hm_ref, convo_ref, co_ref, no_ref, mo_ref, cbuf, c_s, n_s, m_s) = rest[:10], rest[10:]
    for b in range(m3_ref.shape[0]):
        _mlstm_chunk(m3_ref.at[b], conv0_ref.at[b], c0_ref.at[b], n0_ref.at[b], m0_ref.at[b], *consts,
                     hm_ref.at[b], convo_ref.at[b], co_ref.at[b], no_ref.at[b], mo_ref.at[b],
                     cbuf.at[b], c_s.at[b], n_s.at[b], m_s.at[b])


def _mlstm_chunk(m3_ref, conv0_ref, c0_ref, n0_ref, m0_ref, convw_ref, convb_ref, wq_ref, wk_ref,
                 wif_ref, wift_ref, bif_ref, bift_ref, normw_ref, skip_ref,
                 hm_ref, convo_ref, co_ref, no_ref, mo_ref, cbuf, c_s, n_s, m_s):
    j = pl.program_id(1)
    L = m3_ref.shape[0]
    W = MLSTM_WIDTH
    D = HEAD_DIM

    @pl.when(j == 0)
    def _():
        cbuf[0:8, :] = jnp.zeros((8, W), F32)
        cbuf[8 - (CONV_WIDTH - 1):8, :] = conv0_ref[...]
        c_s[...] = c0_ref[...]
        n_s[...] = n0_ref[...]
        m_s[...] = m0_ref[...]

    blk = m3_ref[...]
    c_in = blk[:, 0:W]
    v_m = blk[:, W:2 * W]
    z = blk[:, 2 * W:3 * W]
    cbuf[8:8 + L, :] = c_in
    acc = jnp.zeros((L, W), F32) + convb_ref[...]
    for tap in range(CONV_WIDTH):
        off = 8 - (CONV_WIDTH - 1) + tap
        acc = acc + cbuf[off:off + L, :] * convw_ref[tap:tap + 1, :]
    xc = _silu(acc)
    convo_ref[...] = cbuf[8 + L - (CONV_WIDTH - 1):8 + L, :]
    cbuf[0:8, :] = cbuf[L:L + 8, :]

    xcb = xc.astype(BF16)
    q_m = _dot(xcb, wq_ref[...])
    k_m = _dot(xcb, wk_ref[...])
    gate_in = jnp.concatenate([q_m, k_m, v_m], axis=1)
    g_col = _dot_hi(gate_in, wif_ref[...]) + bif_ref[...]
    g_row = _dot_nt(wift_ref[...], gate_in, precision=lax.Precision.HIGHEST) + bift_ref[...]
    i_col, lf_col = g_col[:, 0:MLSTM_HEADS], _log_sigmoid(g_col[:, MLSTM_HEADS:])
    i_row, lf_row = g_row[0:MLSTM_HEADS, :], _log_sigmoid(g_row[MLSTM_HEADS:, :])
    rr = lax.broadcasted_iota(jnp.int32, (L, L), 0)
    cc = lax.broadcasted_iota(jnp.int32, (L, L), 1)
    causal = cc <= rr
    tri = causal.astype(F32)
    b_col = _dot_hi(tri, lf_col)
    b_row = _dot_nt(lf_row, tri, precision=lax.Precision.HIGHEST)
    ks = k_m * (D ** -0.5)
    qb = q_m.astype(BF16)
    kb = ks.astype(BF16)
    vb = v_m.astype(BF16)
    lane = lax.broadcasted_iota(jnp.int32, (L, W), 1)
    m_prev_all = m_s[...]
    h_all = jnp.zeros((L, W), F32)
    m_new_list = []
    for h in range(MLSTM_HEADS):
        hs = slice(h * D, (h + 1) * D)
        m_prev = m_prev_all[:, h:h + 1]
        bc = b_col[:, h:h + 1]
        br = b_row[h:h + 1, :]
        ir = i_row[h:h + 1, :]
        ic = i_col[:, h:h + 1]
        d_intra = jnp.where(causal, bc - br + ir, -jnp.inf)
        a_inter = bc + m_prev
        m_t = jnp.maximum(a_inter, jnp.max(d_intra, axis=1, keepdims=True))
        qh, kh, vh = qb[:, hs], kb[:, hs], vb[:, hs]
        s = _dot_nt(qh, kh) * jnp.exp(d_intra - m_t)
        w_inter = jnp.exp(a_inter - m_t)
        c_h = c_s[h * D:(h + 1) * D, :]
        n_h = n_s[h:h + 1, :]
        num = _dot(s.astype(BF16), vh) + w_inter * _dot_nt(qh, c_h.astype(BF16))
        den = jnp.sum(s, axis=1, keepdims=True) + w_inter * jnp.sum(q_m[:, hs] * n_h, axis=1, keepdims=True)
        hh = num / jnp.maximum(jnp.abs(den), jnp.exp(-m_t))
        mu = jnp.mean(hh, axis=1, keepdims=True)
        hc = hh - mu
        var = jnp.mean(hc * hc, axis=1, keepdims=True)
        hn = hc * lax.rsqrt(var + LN_EPS)
        h_all = jnp.where(lane // D == h, jnp.concatenate([hn] * MLSTM_HEADS, axis=1), h_all)
        g_tot = bc[L - 1:L, :]
        a_end = g_tot + m_prev
        d_end_c = g_tot - bc + ic
        m_new = jnp.maximum(a_end, jnp.max(d_end_c, axis=0, keepdims=True))
        w_s = jnp.exp(d_end_c - m_new)
        decay = jnp.exp(a_end - m_new)
        wv = (v_m[:, hs] * w_s).astype(BF16)
        c_s[h * D:(h + 1) * D, :] = decay * c_h + _dot_tn(wv, kh)
        n_s[h:h + 1, :] = decay * n_h + jnp.sum(w_s * ks[:, hs], axis=0, keepdims=True)
        m_new_list.append(m_new)
    m_s[...] = jnp.concatenate(m_new_list, axis=1)
    hm = (h_all * normw_ref[...] + skip_ref[...] * xc) * _silu(z)
    hm_ref[...] = hm.astype(hm_ref.dtype)
    co_ref[...] = c_s[...]
    no_ref[...] = n_s[...]
    mo_ref[...] = m_s[...]


def _mlstm(m3, conv0, c0, n0, m0, conv_w, conv_b, wq_bd, wk_bd, w_if, b_if, norm_w, skip, *, chunk, bb):
    b, t, _ = m3.shape
    W = MLSTM_WIDTH
    full = lambda a: pl.BlockSpec(a.shape, lambda bi, j: (0,) * a.ndim)
    per_b = lambda a: pl.BlockSpec((bb,) + a.shape[1:], lambda bi, j: (bi,) + (0,) * (a.ndim - 1))
    consts = [conv_w, conv_b.reshape(1, W), wq_bd, wk_bd, w_if, w_if.T, b_if.reshape(1, -1),
              b_if.reshape(-1, 1), norm_w.reshape(1, W), skip.reshape(1, W)]
    states = [conv0, c0, n0, m0]
    return pl.pallas_call(
        _mlstm_kernel,
        grid=(b // bb, t // chunk),
        in_specs=[pl.BlockSpec((bb, chunk, 3 * W), lambda bi, j: (bi, j, 0))] + [per_b(s) for s in states]
                 + [full(c) for c in consts],
        out_specs=[pl.BlockSpec((bb, chunk, W), lambda bi, j: (bi, j, 0))] + [per_b(s) for s in states],
        out_shape=[jax.ShapeDtypeStruct((b, t, W), BF16)] + [jax.ShapeDtypeStruct(s.shape, F32) for s in states],
        scratch_shapes=[pltpu.VMEM((bb, chunk + 8, W), F32), pltpu.VMEM((bb,) + c0.shape[1:], F32),
                        pltpu.VMEM((bb,) + n0.shape[1:], F32), pltpu.VMEM((bb,) + m0.shape[1:], F32)],
        compiler_params=_cparams("arbitrary", "arbitrary"),
        name="mlstm",
    )(m3, *states, *consts)


def _interleave(ref, scr):
    dil, rows, width = ref.shape
    if dil == 1:
        return ref[0].astype(F32)
    for r in range(dil):
        for c in range(width // 128):
            scr[c, pl.ds(r, rows, stride=dil), :] = ref[r, :, c * 128:(c + 1) * 128].astype(F32)
    return jnp.concatenate([scr[c] for c in range(width // 128)], axis=1)


def _combine(os, lses):
    tm = os[0].shape[0]
    lane = lax.broadcasted_iota(jnp.int32, (tm, GROUP_WIDTH), 1)

    def spread(l2):
        out = jnp.zeros((tm, GROUP_WIDTH), F32)
        for h in range(HEADS_PER_GROUP):
            out = jnp.where(lane // HEAD_DIM == h, l2[:, 32 * h:32 * h + 1], out)
        return out

    ls = [spread(l) for l in lses]
    top = jnp.maximum(jnp.maximum(ls[0], ls[1]), ls[2])
    es = [jnp.exp(l - top) for l in ls]
    tot = es[0] + es[1] + es[2]
    return [(o * (e / tot)).astype(BF16) for o, e in zip(os, es)]


def _outproj_kernel(combine, alpha, *refs):
    if combine:
        (o0, o1, o2, l0, l1, l2, hm_ref, x_ref, wo_ref, g_ref, b_ref, wq_ref, x1_ref, qc_ref, scr) = refs
        att = _combine([_interleave(o, scr) for o in (o0, o1, o2)], [_interleave(l, scr) for l in (l0, l1, l2)])
    else:
        (a_ref, hm_ref, x_ref, wo_ref, g_ref, b_ref, wq_ref, x1_ref, qc_ref) = refs
        att = [a_ref[:, g * GROUP_WIDTH:(g + 1) * GROUP_WIDTH].astype(BF16) for g in range(len(ATT_GROUPS))]
    mix = _dot(hm_ref[...].astype(BF16), wo_ref[ATT_WIDTH:, :])
    for g, a in enumerate(att):
        mix = mix + _dot(a, wo_ref[g * GROUP_WIDTH:(g + 1) * GROUP_WIDTH, :])
    x1 = _layer_norm(alpha * x_ref[...] + mix, g_ref[...], b_ref[...])
    x1_ref[...] = x1
    qc_ref[...] = (_dot(x1.astype(BF16), wq_ref[...]) * ((x1.shape[1] // MEM_HEADS) ** -0.5)).astype(qc_ref.dtype)


def _outproj(att_parts, hm, x, w_out, ln_g, ln_b, w_cq, alpha, *, tm):
    n, d = x.shape
    combine = len(att_parts) > 1
    row = lambda a: pl.BlockSpec((tm, a.shape[1]), lambda i: (i, 0))
    full = lambda a: pl.BlockSpec(a.shape, lambda i: (0, 0))

    def split(a):
        _, dil, ts, w = a.shape
        tiles_per_seq = ts * dil // tm
        return pl.BlockSpec((None, dil, tm // dil, w), lambda i: (i // tiles_per_seq, 0, i % tiles_per_seq, 0))

    consts = [w_out, ln_g.reshape(1, d), ln_b.reshape(1, d), w_cq]
    return pl.pallas_call(
        functools.partial(_outproj_kernel, combine, alpha),
        grid=(n // tm,),
        in_specs=[split(a) if combine else row(a) for a in att_parts] + [row(hm), row(x)] + [full(c) for c in consts],
        out_specs=[pl.BlockSpec((tm, d), lambda i: (i, 0))] * 2,
        out_shape=[jax.ShapeDtypeStruct((n, d), F32), jax.ShapeDtypeStruct((n, d), BF16)],
        scratch_shapes=[pltpu.VMEM((GROUP_WIDTH // 128, tm, 128), F32)] if combine else [],
        compiler_params=_cparams("arbitrary"),
        name="outproj_ln1",
    )(*att_parts, hm, x, *consts)


def _memkv_kernel(mem_ref, wk_ref, wv_ref, kv_ref):
    d = mem_ref.shape[2]
    mem = mem_ref[0].astype(BF16)
    kv_ref[0, :, 0:d] = _dot(mem, wk_ref[...])
    kv_ref[0, :, d:] = _dot(mem, wv_ref[...])


def _memkv(mem, w_ck, w_cv):
    b, m, d = mem.shape
    return pl.pallas_call(
        _memkv_kernel,
        grid=(b,),
        in_specs=[pl.BlockSpec((1, m, d), lambda i: (i, 0, 0)),
                  pl.BlockSpec(w_ck.shape, lambda i: (0, 0)), pl.BlockSpec(w_cv.shape, lambda i: (0, 0))],
        out_specs=pl.BlockSpec((1, m, 2 * d), lambda i: (i, 0, 0)),
        out_shape=jax.ShapeDtypeStruct((b, m, 2 * d), F32),
        compiler_params=_cparams("arbitrary"),
        name="memory_kv",
    )(mem, w_ck, w_cv)


def _xattn_kernel(q_ref, kv_ref, o_ref):
    d = q_ref.shape[2]
    hd = d // MEM_HEADS
    q = q_ref[0]
    for h in range(MEM_HEADS):
        k = kv_ref[0, :, h * hd:(h + 1) * hd].astype(BF16)
        v = kv_ref[0, :, d + h * hd:d + (h + 1) * hd].astype(BF16)
        s = _dot_nt(q[:, h * hd:(h + 1) * hd], k)
        p = jnp.exp(s - jnp.max(s, axis=1, keepdims=True))
        p = p / jnp.sum(p, axis=1, keepdims=True)
        o_ref[0, :, h * hd:(h + 1) * hd] = _dot(p.astype(BF16), v).astype(o_ref.dtype)


def _xattn(qc, kv, *, tq):
    b, t, d = qc.shape
    m = kv.shape[1]
    return pl.pallas_call(
        _xattn_kernel,
        grid=(b, t // tq),
        in_specs=[pl.BlockSpec((1, tq, d), lambda bi, j: (bi, j, 0)),
                  pl.BlockSpec((1, m, 2 * d), lambda bi, j: (bi, 0, 0))],
        out_specs=pl.BlockSpec((1, tq, d), lambda bi, j: (bi, j, 0)),
        out_shape=jax.ShapeDtypeStruct((b, t, d), BF16),
        compiler_params=_cparams("arbitrary", "arbitrary"),
        name="cross_attention",
    )(qc, kv)


def _xout_kernel(alpha, o_ref, x_ref, w_ref, g_ref, b_ref, y_ref):
    y_ref[...] = _layer_norm(alpha * x_ref[...] + _dot(o_ref[...], w_ref[...]), g_ref[...], b_ref[...])


def _xout(oc, x1, w_co, ln_g, ln_b, alpha, *, tm):
    n, d = x1.shape
    row = pl.BlockSpec((tm, d), lambda i: (i, 0))
    full = lambda a: pl.BlockSpec(a.shape, lambda i: (0, 0))
    consts = [w_co, ln_g.reshape(1, d), ln_b.reshape(1, d)]
    return pl.pallas_call(
        functools.partial(_xout_kernel, alpha),
        grid=(n // tm,),
        in_specs=[row, row] + [full(c) for c in consts],
        out_specs=row,
        out_shape=jax.ShapeDtypeStruct((n, d), F32),
        compiler_params=_cparams("arbitrary"),
        name="cross_out_ln2",
    )(oc, x1, *consts)


def _first_index_of_max(vals, idx, big):
    mx = jnp.max(vals, axis=0, keepdims=True)
    return mx, jnp.min(jnp.where(vals == mx, idx, big), axis=0, keepdims=True)


def _route(xb, wrt_ref, br_ref):
    tm = xb.shape[0]
    n_e = wrt_ref.shape[0]
    per_group = n_e // N_EXPERT_GROUPS
    logits = _dot_nt(wrt_ref[...], xb)
    scores = _sigmoid(logits)
    biased = scores + br_ref[...]
    e_idx = lax.broadcasted_iota(jnp.int32, (n_e, tm), 0).astype(F32)
    g_scores = []
    for g in range(N_EXPERT_GROUPS):
        sub = biased[g * per_group:(g + 1) * per_group, :]
        sidx = lax.broadcasted_iota(jnp.int32, (per_group, tm), 0).astype(F32)
        m1, a1 = _first_index_of_max(sub, sidx, per_group)
        m2 = jnp.max(jnp.where(sidx == a1, -jnp.inf, sub), axis=0, keepdims=True)
        g_scores.append(m1 + m2)
    gs = jnp.concatenate(g_scores, axis=0)
    g_idx = lax.broadcasted_iota(jnp.int32, (N_EXPERT_GROUPS, tm), 0).astype(F32)
    g_sel = jnp.zeros((N_EXPERT_GROUPS, tm), F32)
    work = gs
    for _ in range(TOPK_GROUPS):
        _, a = _first_index_of_max(work, g_idx, N_EXPERT_GROUPS)
        hit = g_idx == a
        g_sel = jnp.where(hit, 1.0, g_sel)
        work = jnp.where(hit, -jnp.inf, work)
    e_mask = jnp.concatenate(
        [jnp.broadcast_to(g_sel[g:g + 1, :], (per_group, tm)) for g in range(N_EXPERT_GROUPS)], axis=0)
    work = jnp.where(e_mask > 0.5, biased, -jnp.inf)
    sel = jnp.zeros((n_e, tm), F32)
    picks = []
    for _ in range(TOP_K):
        cand = jnp.where(sel > 0.5, -jnp.inf, work)
        mx = jnp.max(cand, axis=0, keepdims=True)
        a = jnp.min(jnp.where((cand == mx) & (sel < 0.5), e_idx, float(n_e)), axis=0, keepdims=True)
        pick = jnp.where(e_idx == a, 1.0, 0.0)
        picks.append(pick)
        sel = sel + pick
    w_sel = sel * scores
    gates_t = w_sel / jnp.sum(w_sel, axis=0, keepdims=True) * ROUTED_SCALE
    return gates_t, sel, picks


ROW_CHUNK = 16
TILE_CHUNKS = 16
TILE_ROWS = ROW_CHUNK * TILE_CHUNKS
TOKEN_BLOCK = 256


def _slab_rows(tb, n_e):
    return -(-(TOP_K * tb + n_e * (ROW_CHUNK - 1)) // 128) * 128


def _dispatch_kernel(x_ref, wrt_ref, br_ref, xs_ref, pwt_ref, meta_ref):
    tb = x_ref.shape[0]
    n_e = wrt_ref.shape[0]
    slab = xs_ref.shape[1]
    xb = x_ref[...].astype(BF16)
    gates_t, sel, picks = _route(xb, wrt_ref, br_ref)
    cnt = jnp.sum(sel, axis=1, keepdims=True)
    padded = jnp.floor((cnt + (ROW_CHUNK - 1)) * (1.0 / ROW_CHUNK)) * ROW_CHUNK
    padded_b = jnp.broadcast_to(padded, (n_e, 128))
    er = lax.broadcasted_iota(jnp.int32, (n_e, n_e), 0)
    ec = lax.broadcasted_iota(jnp.int32, (n_e, n_e), 1)
    off_b = _dot((ec < er).astype(BF16), padded_b.astype(BF16))
    tr = lax.broadcasted_iota(jnp.int32, (tb, tb), 0)
    tc = lax.broadcasted_iota(jnp.int32, (tb, tb), 1)
    rank = _dot(sel.astype(BF16), (tr < tc).astype(BF16))
    dest = off_b[:, 0:1] + rank
    dest_k = [jnp.sum(p * dest, axis=0, keepdims=True) for p in picks]
    gate_k = [jnp.sum(p * gates_t, axis=0, keepdims=True) for p in picks]
    meta_ref[0, 0] = padded_b
    meta_ref[0, 1] = off_b

    step = 512
    for r0 in range(0, slab, step):
        rows = lax.broadcasted_iota(jnp.int32, (step, tb), 0).astype(F32) + float(r0)
        onehot = jnp.zeros((step, tb), F32)
        for dk in dest_k:
            onehot = onehot + jnp.where(rows == dk, 1.0, 0.0)
        xs_ref[0, r0:r0 + step, :] = _dot(onehot.astype(BF16), xb).astype(xs_ref.dtype)

    stacked = jnp.concatenate(dest_k + [jnp.zeros((8 - TOP_K, tb), F32)] + gate_k
                              + [jnp.zeros((128 - 8 - TOP_K, tb), F32)], axis=0)
    cols = jnp.transpose(stacked)
    for c0 in range(0, slab, step):
        lanes = lax.broadcasted_iota(jnp.int32, (tb, step), 1).astype(F32) + float(c0)
        w = jnp.zeros((tb, step), F32)
        for k in range(TOP_K):
            w = w + jnp.where(lanes == cols[:, k:k + 1], cols[:, 8 + k:9 + k], 0.0)
        pwt_ref[0, :, c0:c0 + step] = w.astype(pwt_ref.dtype)


def _dispatch(x, w_router_t, b_router):
    n, d = x.shape
    n_e = w_router_t.shape[0]
    tb = TOKEN_BLOCK
    nblk = n // tb
    slab = _slab_rows(tb, n_e)
    assert slab % 512 == 0
    return pl.pallas_call(
        _dispatch_kernel,
        grid=(nblk,),
        in_specs=[pl.BlockSpec((tb, d), lambda i: (i, 0)),
                  pl.BlockSpec(w_router_t.shape, lambda i: (0, 0)),
                  pl.BlockSpec((n_e, 1), lambda i: (0, 0))],
        out_specs=[pl.BlockSpec((1, slab, d), lambda i: (i, 0, 0)),
                   pl.BlockSpec((1, tb, slab), lambda i: (i, 0, 0)),
                   pl.BlockSpec((1, 2, n_e, 128), lambda i: (i, 0, 0, 0))],
        out_shape=[jax.ShapeDtypeStruct((nblk, slab, d), BF16),
                   jax.ShapeDtypeStruct((nblk, tb, slab), BF16),
                   jax.ShapeDtypeStruct((nblk, 2, n_e, 128), F32)],
        compiler_params=_cparams("arbitrary"),
        name="moe_dispatch",
    )(x, w_router_t, b_router.reshape(n_e, 1))


def _schedule_kernel(n_e, nblk, t_max, zero_row, pieces_ref, first_ref, te_ref, nv_ref, row_ref, nt_ref):
    def per_expert(e, carry):
        pos, tile = carry

        def per_block(blk, p):
            idx = blk * n_e + e
            r0 = first_ref[idx]

            def per_piece(j, pp):
                row_ref[pp] = r0 + j * ROW_CHUNK
                return pp + 1
            return lax.fori_loop(0, pieces_ref[idx], per_piece, p)

        end = lax.fori_loop(0, nblk, per_block, pos)
        cnt = end - pos
        n_t = (cnt + (TILE_CHUNKS - 1)) // TILE_CHUNKS
        padded_end = pos + n_t * TILE_CHUNKS

        def pad(p, c):
            row_ref[p] = zero_row
            return c
        lax.fori_loop(end, padded_end, pad, 0)

        def per_tile(i, c):
            te_ref[tile + i] = e
            nv_ref[tile + i] = jnp.minimum(TILE_CHUNKS, cnt - i * TILE_CHUNKS)
            return c
        lax.fori_loop(0, n_t, per_tile, 0)
        return padded_end, tile + n_t

    pos, tile = lax.fori_loop(0, n_e, per_expert, (jnp.int32(0), jnp.int32(0)))
    nt_ref[0] = tile
    last_e = te_ref[jnp.maximum(tile - 1, 0)]

    def idle_tile(t, c):
        te_ref[t] = last_e
        nv_ref[t] = 0
        return c
    lax.fori_loop(tile, t_max, idle_tile, 0)

    def idle_piece(p, c):
        row_ref[p] = zero_row
        return c
    lax.fori_loop(pos, t_max * TILE_CHUNKS, idle_piece, 0)


def _tile_schedule(meta, slab):
    nblk, _, n_e, _ = meta.shape
    pieces = (meta[:, 0, :, 0].astype(jnp.int32) // ROW_CHUNK).reshape(-1)
    first_row = (meta[:, 1, :, 0].astype(jnp.int32)
                 + (jnp.arange(nblk, dtype=jnp.int32) * slab)[:, None]).reshape(-1)
    t_max = -(-(TOP_K * nblk * TOKEN_BLOCK + nblk * n_e * (ROW_CHUNK - 1)) // TILE_ROWS) + n_e + 2
    assert slab - ROW_CHUNK >= TOP_K * TOKEN_BLOCK + n_e * (ROW_CHUNK - 1)
    smem = pl.BlockSpec(memory_space=pltpu.SMEM)
    i32 = lambda n: jax.ShapeDtypeStruct((n,), jnp.int32)
    return pl.pallas_call(
        functools.partial(_schedule_kernel, n_e, nblk, t_max, slab - ROW_CHUNK),
        in_specs=[smem, smem],
        out_specs=[smem, smem, smem, smem],
        out_shape=[i32(t_max), i32(t_max), i32(t_max * TILE_CHUNKS), i32(1)],
        name="moe_schedule",
    )(pieces, first_row)


def _grouped_kernel(te_ref, nv_ref, row_ref, nt_ref, xs_hbm, wg_ref, wu_ref, wd_ref, ys_hbm,
                    lhs, obuf, wg_b, wu_b, wd_b, sem_in, sem_out):
    t = pl.program_id(0)
    n_tiles = nt_ref[0]

    def piece(ref, tile, s):
        row = pl.multiple_of(row_ref[tile * TILE_CHUNKS + s], ROW_CHUNK)
        return ref.at[pl.ds(row, ROW_CHUNK)]

    def local(buf, slot, s):
        start = s * ROW_CHUNK if isinstance(s, int) else pl.multiple_of(s * ROW_CHUNK, ROW_CHUNK)
        return buf.at[slot, pl.ds(start, ROW_CHUNK)]

    def gather(tile, s):
        slot = tile % 2
        return pltpu.make_async_copy(piece(xs_hbm, tile, s), local(lhs, slot, s), sem_in.at[slot])

    def scatter(tile, s):
        slot = tile % 2
        return pltpu.make_async_copy(local(obuf, slot, s), piece(ys_hbm, tile, s), sem_out.at[slot])

    def gathered_tile(slot):
        return pltpu.make_async_copy(xs_hbm.at[pl.ds(0, TILE_ROWS)], lhs.at[slot], sem_in.at[slot])

    def scattered_tile(slot):
        return pltpu.make_async_copy(obuf.at[slot], ys_hbm.at[pl.ds(0, TILE_ROWS)], sem_out.at[slot])

    def for_valid_pieces(tile, fn):
        def body(s, carry):
            fn(tile, s)
            return carry
        lax.fori_loop(0, nv_ref[tile], body, 0)

    @pl.when(t == 0)
    def _():
        for s in range(TILE_CHUNKS):
            gather(0, s).start()

    @pl.when((t >= 2) & (t - 2 < n_tiles))
    def _():
        full = nv_ref[t - 2] == TILE_CHUNKS

        @pl.when(full)
        def _():
            scattered_tile(t % 2).wait()

        @pl.when(jnp.logical_not(full))
        def _():
            for_valid_pieces(t - 2, lambda tile, s: scatter(tile, s).wait())

    @pl.when((t < n_tiles) & ((t == 0) | (te_ref[t] != te_ref[jnp.maximum(t - 1, 0)])))
    def _():
        wg_b[...] = wg_ref[0].astype(BF16)
        wu_b[...] = wu_ref[0].astype(BF16)
        wd_b[...] = wd_ref[0].astype(BF16)

    @pl.when(t < n_tiles)
    def _():
        slot = t % 2
        gathered_tile(slot).wait()
        for s in range(TILE_CHUNKS):
            gather(t + 1, s).start()
        x = lhs[slot]
        h = _silu(_dot(x, wg_b[...])) * _dot(x, wu_b[...])
        obuf[slot] = _dot(h.astype(BF16), wd_b[...]).astype(obuf.dtype)

    @pl.when(t == n_tiles)
    def _():
        gathered_tile(t % 2).wait()

    @pl.when(t < n_tiles)
    def _():
        full = nv_ref[t] == TILE_CHUNKS

        @pl.when(full)
        def _():
            for s in range(TILE_CHUNKS):
                scatter(t, s).start()

        @pl.when(jnp.logical_not(full))
        def _():
            for_valid_pieces(t, lambda tile, s: scatter(tile, s).start())


def _grouped_experts(xs, tile_e, n_valid, rows_tbl, n_tiles, w_gate, w_up, w_down):
    rows, d = xs.shape
    n_e, _, ff = w_gate.shape
    t_max = tile_e.shape[0]
    grid_spec = pltpu.PrefetchScalarGridSpec(
        num_scalar_prefetch=4,
        grid=(t_max,),
        in_specs=[pl.BlockSpec(memory_space=pl.ANY),
                  pl.BlockSpec((1, d, ff), lambda t, te, nv, sr, nt: (te[t], 0, 0)),
                  pl.BlockSpec((1, d, ff), lambda t, te, nv, sr, nt: (te[t], 0, 0)),
                  pl.BlockSpec((1, ff, d), lambda t, te, nv, sr, nt: (te[t], 0, 0))],
        out_specs=pl.BlockSpec(memory_space=pl.ANY),
        scratch_shapes=[pltpu.VMEM((2, TILE_ROWS, d), BF16), pltpu.VMEM((2, TILE_ROWS, d), BF16),
                        pltpu.VMEM((d, ff), BF16), pltpu.VMEM((d, ff), BF16), pltpu.VMEM((ff, d), BF16),
                        pltpu.SemaphoreType.DMA((2,)), pltpu.SemaphoreType.DMA((2,))],
    )
    return pl.pallas_call(
        _grouped_kernel,
        grid_spec=grid_spec,
        out_shape=jax.ShapeDtypeStruct((rows, d), BF16),
        input_output_aliases={4: 0},
        compiler_params=_cparams("arbitrary"),
        name="moe_grouped_experts",
    )(tile_e, n_valid, rows_tbl, n_tiles, xs, w_gate, w_up, w_down)


def _moe_out_kernel(alpha, pwt_ref, ys_ref, x_ref, sg_ref, su_ref, sd_ref, g_ref, b_ref, y_ref):
    routed = _dot(pwt_ref[0], ys_ref[0])
    x = x_ref[...]
    xb = x.astype(BF16)
    hs = _silu(_dot(xb, sg_ref[...])) * _dot(xb, su_ref[...])
    shared = _dot(hs.astype(BF16), sd_ref[...])
    y_ref[...] = _layer_norm(alpha * x + (routed + shared), g_ref[...], b_ref[...])


def _moe_out(pwt, ys, x, ws_gate, ws_up, ws_down, ln_g, ln_b, alpha):
    n, d = x.shape
    nblk, tb, slab = pwt.shape
    consts = [ws_gate, ws_up, ws_down, ln_g.reshape(1, d), ln_b.reshape(1, d)]
    return pl.pallas_call(
        functools.partial(_moe_out_kernel, alpha),
        grid=(nblk,),
        in_specs=[pl.BlockSpec((1, tb, slab), lambda i: (i, 0, 0)),
                  pl.BlockSpec((1, slab, d), lambda i: (i, 0, 0)),
                  pl.BlockSpec((tb, d), lambda i: (i, 0))]
                 + [pl.BlockSpec(c.shape, lambda i: (0, 0)) for c in consts],
        out_specs=pl.BlockSpec((tb, d), lambda i: (i, 0)),
        out_shape=jax.ShapeDtypeStruct((n, d), F32),
        compiler_params=_cparams("arbitrary"),
        name="moe_combine",
    )(pwt, ys.reshape(nblk, slab, d), x, *consts)


def _moe(x, w_router_t, b_router, w_gate, w_up, w_down, ws_gate, ws_up, ws_down, ln_g, ln_b, alpha):
    xs, pwt, meta = _dispatch(x, w_router_t, b_router)
    nblk, slab, d = xs.shape
    tile_e, n_valid, src, n_tiles = _tile_schedule(meta, slab)
    ys = _grouped_experts(xs.reshape(nblk * slab, d), tile_e, n_valid, src, n_tiles, w_gate, w_up, w_down)
    return _moe_out(pwt, ys, x, ws_gate, ws_up, ws_down, ln_g, ln_b, alpha)


def _rope_tables(pos):
    half = HEAD_DIM // 2
    inv_freq = ROPE_THETA ** (-jnp.arange(half, dtype=F32) / half)
    ang = pos.astype(F32)[:, None] * inv_freq[None, :]
    cos, sin = jnp.cos(ang), jnp.sin(ang)
    return jnp.tile(jnp.concatenate([cos, cos], axis=1), (1, 2)), jnp.tile(jnp.concatenate([-sin, sin], axis=1), (1, 2))


def _block_diag(w):
    h, d, _ = w.shape
    out = jnp.zeros((h * d, h * d), w.dtype)
    for i in range(h):
        out = out.at[i * d:(i + 1) * d, i * d:(i + 1) * d].set(w[i])
    return out


def _pick(n, pref):
    return pref if n % pref == 0 else n


def kernel(x_prompt, x_sample, mem_prompt, cache_win128, cache_win512, cache_win2048, cache_mem_kv, state_conv, state_C, state_n, state_m, w_in, conv_w, conv_b, wq_m, wk_m, w_if, b_if, mh_norm_w, skip_m, w_out, ln1_g, ln1_b, w_cq, w_ck, w_cv, w_co, ln2_g, ln2_b, w_router, b_router, w_gate, w_up, w_down, ws_gate, ws_up, ws_down, ln3_g, ln3_b):
    depth = w_in.shape[0]
    assert depth == 1
    alpha = float((2 * depth) ** 0.25)
    bp, seq, d = x_prompt.shape
    bs, dec, _ = x_sample.shape
    assert seq % ATT_GROUPS[-1][0] == 0
    for c, (window, _) in zip((cache_win128, cache_win512, cache_win2048), ATT_GROUPS):
        assert c.shape[2] == window

    l = 0
    bf = lambda a: a.astype(BF16)
    w_in_b, w_out_b = bf(w_in[l]), bf(w_out[l])
    w_cq_b, w_ck_b, w_cv_b, w_co_b = bf(w_cq[l]), bf(w_ck[l]), bf(w_cv[l]), bf(w_co[l])
    wq_bd, wk_bd = bf(_block_diag(wq_m[l])), bf(_block_diag(wk_m[l]))
    w_router_t = bf(w_router[l].T)
    ws_gate_b, ws_up_b, ws_down_b = bf(ws_gate[l]), bf(ws_up[l]), bf(ws_down[l])

    def tail_of_layer(x1, qc, kv, batch, t, tm, tq):
        n = batch * t
        oc = _xattn(qc.reshape(batch, t, d), kv, tq=tq).reshape(n, d)
        x2 = _xout(oc, x1, w_co_b, ln2_g[l], ln2_b[l], alpha, tm=tm)
        return _moe(x2, w_router_t, b_router[l], w_gate[l], w_up[l], w_down[l], ws_gate_b, ws_up_b, ws_down_b,
                    ln3_g[l], ln3_b[l], alpha)

    def mlstm(m3, batch, t, states, chunk, bb):
        return _mlstm(m3.reshape(batch, t, -1), *states, conv_w[l], conv_b[l], wq_bd, wk_bd, w_if[l], b_if[l],
                      mh_norm_w[l], skip_m[l], chunk=chunk, bb=bb)

    np_ = bp * seq
    xp = x_prompt.reshape(np_, d)
    cos_p, sin_p = _rope_tables(jnp.arange(seq))
    *qkv, m3, t128, t512, t2048 = _inproj_prompt(xp, w_in_b, cos_p, sin_p, seq=seq, tm=512)
    parts, lses = [], []
    for g in range(len(ATT_GROUPS)):
        o, lse = _win_attn(*qkv[3 * g:3 * g + 3])
        parts.append(o)
        lses.append(lse)
    zeros_p = [jnp.zeros((bp, CONV_WIDTH - 1, MLSTM_WIDTH), F32), jnp.zeros((bp, MLSTM_WIDTH, HEAD_DIM), F32),
               jnp.zeros((bp, MLSTM_HEADS, HEAD_DIM), F32), jnp.zeros((bp, 1, MLSTM_HEADS), F32)]
    hm_p, p_conv, p_c, p_n, p_m = mlstm(m3, bp, seq, zeros_p, 256, bp)
    x1, qc = _outproj(parts + lses, hm_p.reshape(np_, -1), xp, w_out_b, ln1_g[l], ln1_b[l], w_cq_b, alpha, tm=256)
    kv_p = _memkv(mem_prompt, w_ck_b, w_cv_b)
    y_p = tail_of_layer(x1, qc, kv_p, bp, seq, 256, 512)

    ns = bs * dec
    xs = x_sample.reshape(ns, d)
    cos_s, sin_s = _rope_tables(jnp.tile(PAST_LEN + jnp.arange(dec), bs))
    qs, ks, vs, m3s = _inproj(xs, w_in_b, cos_s, sin_s, tm=ns)
    caches = [c[l].reshape(bs, c.shape[2], 2 * GROUP_WIDTH) for c in (cache_win128, cache_win512, cache_win2048)]
    att_s, s128, s512, s2048 = _dec_attn(qs.reshape(bs, dec, -1), ks.reshape(bs, dec, -1), vs.reshape(bs, dec, -1),
                                         caches)
    states_s = [state_conv[l], state_C[l].reshape(bs, MLSTM_WIDTH, HEAD_DIM), state_n[l],
                state_m[l].reshape(bs, 1, MLSTM_HEADS)]
    hm_s, s_conv, s_c, s_n, s_m = mlstm(m3s, bs, dec, states_s, dec, 4 if bs % 4 == 0 else 1)
    x1s, qcs = _outproj([att_s.reshape(ns, -1)], hm_s.reshape(ns, -1), xs, w_out_b, ln1_g[l], ln1_b[l], w_cq_b,
                        alpha, tm=ns)
    kv_s = cache_mem_kv[l].reshape(bs, cache_mem_kv.shape[2], 2 * d)
    y_s = tail_of_layer(x1s, qcs, kv_s, bs, dec, ns, dec)

    win_shape = lambda a, b_: a.reshape(1, b_, a.shape[1], 2, HEADS_PER_GROUP, HEAD_DIM)
    return (y_p.reshape(bp, seq, d), y_s.reshape(bs, dec, d),
            win_shape(t128, bp), win_shape(t512, bp), win_shape(t2048, bp),
            kv_p.reshape(1, bp, mem_prompt.shape[1], 2, MEM_HEADS, d // MEM_HEADS),
            p_conv[None], p_c.reshape(1, bp, MLSTM_HEADS, HEAD_DIM, HEAD_DIM), p_n[None],
            p_m.reshape(1, bp, MLSTM_HEADS),
            win_shape(s128, bs), win_shape(s512, bs), win_shape(s2048, bs),
            s_conv[None], s_c.reshape(1, bs, MLSTM_HEADS, HEAD_DIM, HEAD_DIM), s_n[None],
            s_m.reshape(1, bs, MLSTM_HEADS))
```

```python
import functools
import math

import jax
import jax.numpy as jnp
from jax import lax
from jax.experimental import pallas as pl
from jax.experimental.pallas import tpu as pltpu

F32 = jnp.float32
BF16 = jnp.bfloat16

HEAD_DIM = 64
ATT_GROUPS = ((128, 1), (512, 4), (2048, 16))
HEADS_PER_GROUP = 4
GROUP_WIDTH = HEADS_PER_GROUP * HEAD_DIM
ATT_WIDTH = GROUP_WIDTH * len(ATT_GROUPS)
N_KEYS = 129
ROPE_THETA = 10000.0
PAST_LEN = 8192
MLSTM_HEADS = 4
MLSTM_WIDTH = MLSTM_HEADS * HEAD_DIM
CONV_WIDTH = 4
MEM_HEADS = 4
N_EXPERT_GROUPS = 8
TOPK_GROUPS = 4
TOP_K = 6
ROUTED_SCALE = 2.5
LN_EPS = 1e-5
NEG = -1e30
VMEM_LIMIT = 56 * 1024 * 1024


def _cparams(*sem):
    return pltpu.CompilerParams(dimension_semantics=sem, vmem_limit_bytes=VMEM_LIMIT)


def _dot(a, b):
    return jnp.dot(a, b, preferred_element_type=F32)


def _dot_nt(a, b, precision=None):
    return lax.dot_general(a, b, (((1,), (1,)), ((), ())), precision=precision,
                           preferred_element_type=F32)


def _dot_tn(a, b):
    return lax.dot_general(a, b, (((0,), (0,)), ((), ())), preferred_element_type=F32)


def _dot_hi(a, b):
    return jnp.dot(a, b, precision=lax.Precision.HIGHEST, preferred_element_type=F32)


def _layer_norm(x, g, b):
    mu = jnp.mean(x, axis=-1, keepdims=True)
    xc = x - mu
    var = jnp.mean(xc * xc, axis=-1, keepdims=True)
    return xc * lax.rsqrt(var + LN_EPS) * g + b


def _sigmoid(x):
    return 1.0 / (1.0 + jnp.exp(-x))


def _silu(x):
    return x * _sigmoid(x)


def _log_sigmoid(x):
    return jnp.minimum(x, 0.0) - jnp.log(1.0 + jnp.exp(-jnp.abs(x)))


def _project_qkv(x_ref, w_ref, cos_ref, sin_ref):
    tm = x_ref.shape[0]
    x = x_ref[...].astype(BF16)
    cos = jnp.concatenate([cos_ref[...]] * (ATT_WIDTH // 128), axis=1)
    sin = jnp.concatenate([sin_ref[...]] * (ATT_WIDTH // 128), axis=1)
    lane = lax.broadcasted_iota(jnp.int32, (tm, ATT_WIDTH), 1)
    first_half = (lane % HEAD_DIM) < (HEAD_DIM // 2)

    def rope(t):
        fwd = pltpu.roll(t, ATT_WIDTH - HEAD_DIM // 2, 1)
        bwd = pltpu.roll(t, HEAD_DIM // 2, 1)
        return t * cos + jnp.where(first_half, fwd, bwd) * sin

    q = rope(_dot(x, w_ref[:, 0:ATT_WIDTH])) * (HEAD_DIM ** -0.5)
    k = rope(_dot(x, w_ref[:, ATT_WIDTH:2 * ATT_WIDTH]))
    v = _dot(x, w_ref[:, 2 * ATT_WIDTH:3 * ATT_WIDTH])
    return q, k, v, _dot(x, w_ref[:, 3 * ATT_WIDTH:])


def _inproj_kernel(x_ref, w_ref, cos_ref, sin_ref, q_ref, k_ref, v_ref, m_ref):
    q, k, v, m = _project_qkv(x_ref, w_ref, cos_ref, sin_ref)
    q_ref[...] = q.astype(q_ref.dtype)
    k_ref[...] = k.astype(k_ref.dtype)
    v_ref[...] = v.astype(v_ref.dtype)
    m_ref[...] = m


def _inproj_prompt_kernel(x_ref, w_ref, cos_ref, sin_ref, *refs):
    n_g = len(ATT_GROUPS)
    qkv_refs, m_ref, tails, scr = refs[:3 * n_g], refs[3 * n_g], refs[3 * n_g + 1:4 * n_g + 1], refs[4 * n_g + 1]
    tm = x_ref.shape[0]
    q, k, v, m = _project_qkv(x_ref, w_ref, cos_ref, sin_ref)
    m_ref[...] = m
    for g, (_, dil) in enumerate(ATT_GROUPS):
        cols = slice(g * GROUP_WIDTH, (g + 1) * GROUP_WIDTH)
        for a, val in enumerate((q, k, v)):
            o_ref = qkv_refs[3 * g + a]
            if dil == 1:
                o_ref[0, 0] = val[:, cols].astype(o_ref.dtype)
                continue
            for c in range(GROUP_WIDTH // 128):
                scr[c] = val[:, g * GROUP_WIDTH + c * 128:g * GROUP_WIDTH + (c + 1) * 128]
            for r in range(dil):
                for c in range(GROUP_WIDTH // 128):
                    o_ref[0, r, :, c * 128:(c + 1) * 128] = scr[c, pl.ds(r, tm // dil, stride=dil), :].astype(o_ref.dtype)
        t_ref = tails[g]
        rows = t_ref.shape[1]
        t_ref[0, :, 0:GROUP_WIDTH] = k[tm - rows:, cols]
        t_ref[0, :, GROUP_WIDTH:] = v[tm - rows:, cols]


def _inproj_prompt(x, w_in, cos, sin, *, seq, tm):
    n, d = x.shape
    tiles_per_seq = seq // tm
    batch = n // seq
    out_shape, out_specs = [], []
    for _, dil in ATT_GROUPS:
        for _ in range(3):
            out_shape.append(jax.ShapeDtypeStruct((batch, dil, seq // dil, GROUP_WIDTH), BF16))
            out_specs.append(pl.BlockSpec((1, dil, tm // dil, GROUP_WIDTH),
                                          lambda i: (i // tiles_per_seq, 0, i % tiles_per_seq, 0)))
    m_width = w_in.shape[1] - 3 * ATT_WIDTH
    out_shape.append(jax.ShapeDtypeStruct((n, m_width), F32))
    out_specs.append(pl.BlockSpec((tm, m_width), lambda i: (i, 0)))
    for window, _ in ATT_GROUPS:
        rows = min(window, tm)
        first = tiles_per_seq - window // rows if window > rows else tiles_per_seq - 1
        out_shape.append(jax.ShapeDtypeStruct((batch, min(window, seq), 2 * GROUP_WIDTH), F32))
        out_specs.append(pl.BlockSpec(
            (1, rows, 2 * GROUP_WIDTH),
            lambda i, first=first: (i // tiles_per_seq, jnp.maximum(i % tiles_per_seq - first, 0), 0)))
    return pl.pallas_call(
        _inproj_prompt_kernel,
        grid=(n // tm,),
        in_specs=[pl.BlockSpec((tm, d), lambda i: (i, 0)),
                  pl.BlockSpec(w_in.shape, lambda i: (0, 0)),
                  pl.BlockSpec((tm, 128), lambda i: (i % tiles_per_seq, 0)),
                  pl.BlockSpec((tm, 128), lambda i: (i % tiles_per_seq, 0))],
        out_specs=out_specs,
        out_shape=out_shape,
        scratch_shapes=[pltpu.VMEM((GROUP_WIDTH // 128, tm, 128), F32)],
        compiler_params=_cparams("arbitrary"),
        name="inproj_rope_prompt",
    )(x, w_in, cos, sin)


def _inproj(x, w_in, cos, sin, *, tm):
    n, d = x.shape
    out_shape = [jax.ShapeDtypeStruct((n, ATT_WIDTH), F32)] * 3 + [
        jax.ShapeDtypeStruct((n, w_in.shape[1] - 3 * ATT_WIDTH), F32)]
    row_spec = lambda w: pl.BlockSpec((tm, w), lambda i: (i, 0))
    out_specs = [row_spec(ATT_WIDTH)] * 3 + [row_spec(w_in.shape[1] - 3 * ATT_WIDTH)]
    tiles_per_seq = cos.shape[0] // tm
    return pl.pallas_call(
        _inproj_kernel,
        grid=(n // tm,),
        in_specs=[row_spec(d),
                  pl.BlockSpec(w_in.shape, lambda i: (0, 0)),
                  pl.BlockSpec((tm, 128), lambda i: (i % tiles_per_seq, 0)),
                  pl.BlockSpec((tm, 128), lambda i: (i % tiles_per_seq, 0))],
        out_specs=out_specs,
        out_shape=out_shape,
        compiler_params=_cparams("arbitrary"),
        name="inproj_rope",
    )(x, w_in, cos, sin)


def _win_attn_kernel(q_ref, kp_ref, kc_ref, vp_ref, vc_ref, o_ref, lse_ref):
    j = pl.program_id(2)
    tq = q_ref.shape[1]
    sub = N_KEYS - 1
    assert kp_ref.shape[1] == sub and tq % sub == 0
    k_all = jnp.concatenate([kp_ref[0], kc_ref[0]], axis=0)
    v_all = jnp.concatenate([vp_ref[0], vc_ref[0]], axis=0)
    row = lax.broadcasted_iota(jnp.int32, (sub, 2 * sub), 0)
    col = lax.broadcasted_iota(jnp.int32, (sub, 2 * sub), 1)
    band = (col >= row) & (col <= row + sub)
    lane = lax.broadcasted_iota(jnp.int32, (sub, GROUP_WIDTH), 1)
    lane_l = lax.broadcasted_iota(jnp.int32, (sub, 128), 1)
    for i in range(tq // sub):
        q = q_ref[0, i * sub:(i + 1) * sub, :]
        kk = k_all[i * sub:(i + 2) * sub]
        vv = v_all[i * sub:(i + 2) * sub]
        valid = band & ((j > 0) | (col >= sub)) if i == 0 else band
        o_acc = jnp.zeros((sub, GROUP_WIDTH), F32)
        lse_acc = jnp.zeros((sub, 128), F32)
        for h in range(HEADS_PER_GROUP):
            qh = jnp.where(lane // HEAD_DIM == h, q, jnp.zeros_like(q))
            s = jnp.where(valid, _dot_nt(qh, kk), NEG)
            m = jnp.max(s, axis=1, keepdims=True)
            p = jnp.exp(s - m)
            l = jnp.sum(p, axis=1, keepdims=True)
            o = _dot(p.astype(BF16), vv) / l
            o_acc = jnp.where(lane // HEAD_DIM == h, o, o_acc)
            lse_acc = jnp.where(lane_l // 32 == h, m + jnp.log(l), lse_acc)
        o_ref[0, i * sub:(i + 1) * sub, :] = o_acc.astype(o_ref.dtype)
        lse_ref[0, i * sub:(i + 1) * sub, :] = lse_acc


def _win_attn(q, k, v, *, tq=512):
    b, dil, ts, _ = q.shape
    tq = min(tq, ts)
    sub = N_KEYS - 1
    cur = pl.BlockSpec((None, 1, tq, GROUP_WIDTH), lambda bi, r, j: (bi, r, j, 0))
    prev = pl.BlockSpec((None, 1, sub, GROUP_WIDTH),
                        lambda bi, r, j: (bi, r, jnp.maximum(j * (tq // sub) - 1, 0), 0))
    return pl.pallas_call(
        _win_attn_kernel,
        grid=(b, dil, ts // tq),
        in_specs=[cur, prev, cur, prev, cur],
        out_specs=[pl.BlockSpec((None, 1, tq, GROUP_WIDTH), lambda bi, r, j: (bi, r, j, 0)),
                   pl.BlockSpec((None, 1, tq, 128), lambda bi, r, j: (bi, r, j, 0))],
        out_shape=[jax.ShapeDtypeStruct((b, dil, ts, GROUP_WIDTH), BF16),
                   jax.ShapeDtypeStruct((b, dil, ts, 128), F32)],
        compiler_params=_cparams("arbitrary", "arbitrary", "arbitrary"),
        name="window_attention",
    )(q, k, k, v, v)


def _dec_attn_kernel(q_ref, k_ref, v_ref, c0_ref, c1_ref, c2_ref, att_ref, o0_ref, o1_ref, o2_ref,
                     e0_ref, e1_ref, e2_ref):
    t_new = q_ref.shape[1]
    q = q_ref[0]
    k_new = k_ref[0]
    v_new = v_ref[0]
    lane = lax.broadcasted_iota(jnp.int32, (8, GROUP_WIDTH), 1)
    sub = lax.broadcasted_iota(jnp.int32, (8, GROUP_WIDTH), 0)
    head_sel = (lane // HEAD_DIM) == sub
    outs = [[None] * t_new for _ in ATT_GROUPS]
    lses = [[None] * t_new for _ in ATT_GROUPS]
    for g, ((window, dil), c_ref, o_ref, e_ref) in enumerate(
            zip(ATT_GROUPS, (c0_ref, c1_ref, c2_ref), (o0_ref, o1_ref, o2_ref), (e0_ref, e1_ref, e2_ref))):
        w = c_ref.shape[1]
        cols = slice(g * GROUP_WIDTH, (g + 1) * GROUP_WIDTH)
        kv_new = jnp.concatenate([k_new[:, cols], v_new[:, cols]], axis=1)
        n_chunk = e_ref.shape[0]
        for c in range(n_chunk):
            e_ref[c, 0:w, :] = c_ref[0, :, c * 128:(c + 1) * 128]
            e_ref[c, w:w + t_new, :] = kv_new[:, c * 128:(c + 1) * 128]
            o_ref[0, :, c * 128:(c + 1) * 128] = e_ref[c, t_new:w + t_new, :]
        for t in range(t_new):
            past = jnp.concatenate(
                [e_ref[c, pl.ds(t, N_KEYS - 1, stride=dil), :] for c in range(n_chunk)], axis=1)
            qm = jnp.where(head_sel, jnp.broadcast_to(q[t:t + 1, cols], (8, GROUP_WIDTH)), 0.0)
            s = _dot_nt(qm.astype(BF16), past[:, 0:GROUP_WIDTH].astype(BF16))
            k_self = kv_new[t:t + 1, 0:GROUP_WIDTH]
            v_self = kv_new[t:t + 1, GROUP_WIDTH:]
            s_self = jnp.sum(qm * k_self, axis=1, keepdims=True)
            m = jnp.maximum(jnp.max(s, axis=1, keepdims=True), s_self)
            p = jnp.exp(s - m)
            p_self = jnp.exp(s_self - m)
            l = jnp.sum(p, axis=1, keepdims=True) + p_self
            o = _dot(p.astype(BF16), past[:, GROUP_WIDTH:].astype(BF16))
            o = (o + p_self * v_self) / l
            outs[g][t] = o
            lses[g][t] = m + jnp.log(l)
    for t in range(t_new):
        top = jnp.maximum(jnp.maximum(lses[0][t], lses[1][t]), lses[2][t])
        es = [jnp.exp(lses[g][t] - top) for g in range(len(ATT_GROUPS))]
        tot = es[0] + es[1] + es[2]
        for g in range(len(ATT_GROUPS)):
            weighted = jnp.where(head_sel, outs[g][t] * (es[g] / tot), 0.0)
            att_ref[0, t:t + 1, g * GROUP_WIDTH:(g + 1) * GROUP_WIDTH] = jnp.sum(weighted, axis=0, keepdims=True)


def _dec_attn(q, k, v, caches):
    b, t_new, _ = q.shape
    tok = pl.BlockSpec((1, t_new, ATT_WIDTH), lambda i: (i, 0, 0))
    cspec = [pl.BlockSpec((1,) + c.shape[1:], lambda i: (i, 0, 0)) for c in caches]
    return pl.pallas_call(
        _dec_attn_kernel,
        grid=(b,),
        in_specs=[tok, tok, tok] + cspec,
        out_specs=[tok] + cspec,
        out_shape=[jax.ShapeDtypeStruct(q.shape, F32)] + [jax.ShapeDtypeStruct(c.shape, F32) for c in caches],
        scratch_shapes=[pltpu.VMEM((c.shape[2] // 128, c.shape[1] + t_new, 128), F32) for c in caches],
        compiler_params=_cparams("arbitrary"),
        name="decode_attention",
    )(q, k, v, *caches)


def _mlstm_kernel(m3_ref, conv0_ref, c0_ref, n0_ref, m0_ref, *rest):
    consts, (hm_ref, convo_ref, co_ref, no_ref, mo_ref, cbuf, c_s, n_s, m_s) = rest[:10], rest[10:]
    for b in range(m3_ref.shape[0]):
        _mlstm_chunk(m3_ref.at[b], conv0_ref.at[b], c0_ref.at[b], n0_ref.at[b], m0_ref.at[b], *consts,
                     hm_ref.at[b], convo_ref.at[b], co_ref.at[b], no_ref.at[b], mo_ref.at[b],
                     cbuf.at[b], c_s.at[b], n_s.at[b], m_s.at[b])


def _mlstm_chunk(m3_ref, conv0_ref, c0_ref, n0_ref, m0_ref, convw_ref, convb_ref, wq_ref, wk_ref,
                 wif_ref, wift_ref, bif_ref, bift_ref, normw_ref, skip_ref,
                 hm_ref, convo_ref, co_ref, no_ref, mo_ref, cbuf, c_s, n_s, m_s):
    j = pl.program_id(1)
    L = m3_ref.shape[0]
    W = MLSTM_WIDTH
    D = HEAD_DIM

    @pl.when(j == 0)
    def _():
        cbuf[0:8, :] = jnp.zeros((8, W), F32)
        cbuf[8 - (CONV_WIDTH - 1):8, :] = conv0_ref[...]
        c_s[...] = c0_ref[...]
        n_s[...] = n0_ref[...]
        m_s[...] = m0_ref[...]

    blk = m3_ref[...]
    c_in = blk[:, 0:W]
    v_m = blk[:, W:2 * W]
    z = blk[:, 2 * W:3 * W]
    cbuf[8:8 + L, :] = c_in
    acc = jnp.zeros((L, W), F32) + convb_ref[...]
    for tap in range(CONV_WIDTH):
        off = 8 - (CONV_WIDTH - 1) + tap
        acc = acc + cbuf[off:off + L, :] * convw_ref[tap:tap + 1, :]
    xc = _silu(acc)
    convo_ref[...] = cbuf[8 + L - (CONV_WIDTH - 1):8 + L, :]
    cbuf[0:8, :] = cbuf[L:L + 8, :]

    xcb = xc.astype(BF16)
    q_m = _dot(xcb, wq_ref[...])
    k_m = _dot(xcb, wk_ref[...])
    gate_in = jnp.concatenate([q_m, k_m, v_m], axis=1)
    g_col = _dot_hi(gate_in, wif_ref[...]) + bif_ref[...]
    g_row = _dot_nt(wift_ref[...], gate_in, precision=lax.Precision.HIGHEST) + bift_ref[...]
    i_col, lf_col = g_col[:, 0:MLSTM_HEADS], _log_sigmoid(g_col[:, MLSTM_HEADS:])
    i_row, lf_row = g_row[0:MLSTM_HEADS, :], _log_sigmoid(g_row[MLSTM_HEADS:, :])
    rr = lax.broadcasted_iota(jnp.int32, (L, L), 0)
    cc = lax.broadcasted_iota(jnp.int32, (L, L), 1)
    causal = cc <= rr
    tri = causal.astype(F32)
    b_col = _dot_hi(tri, lf_col)
    b_row = _dot_nt(lf_row, tri, precision=lax.Precision.HIGHEST)
    ks = k_m * (D ** -0.5)
    qb = q_m.astype(BF16)
    kb = ks.astype(BF16)
    vb = v_m.astype(BF16)
    lane = lax.broadcasted_iota(jnp.int32, (L, W), 1)
    m_prev_all = m_s[...]
    h_all = jnp.zeros((L, W), F32)
    m_new_list = []
    for h in range(MLSTM_HEADS):
        hs = slice(h * D, (h + 1) * D)
        m_prev = m_prev_all[:, h:h + 1]
        bc = b_col[:, h:h + 1]
        br = b_row[h:h + 1, :]
        ir = i_row[h:h + 1, :]
        ic = i_col[:, h:h + 1]
        d_intra = jnp.where(causal, bc - br + ir, -jnp.inf)
        a_inter = bc + m_prev
        m_t = jnp.maximum(a_inter, jnp.max(d_intra, axis=1, keepdims=True))
        qh, kh, vh = qb[:, hs], kb[:, hs], vb[:, hs]
        s = _dot_nt(qh, kh) * jnp.exp(d_intra - m_t)
        w_inter = jnp.exp(a_inter - m_t)
        c_h = c_s[h * D:(h + 1) * D, :]
        n_h = n_s[h:h + 1, :]
        num = _dot(s.astype(BF16), vh) + w_inter * _dot_nt(qh, c_h.astype(BF16))
        den = jnp.sum(s, axis=1, keepdims=True) + w_inter * jnp.sum(q_m[:, hs] * n_h, axis=1, keepdims=True)
        hh = num / jnp.maximum(jnp.abs(den), jnp.exp(-m_t))
        mu = jnp.mean(hh, axis=1, keepdims=True)
        hc = hh - mu
        var = jnp.mean(hc * hc, axis=1, keepdims=True)
        hn = hc * lax.rsqrt(var + LN_EPS)
        h_all = jnp.where(lane // D == h, jnp.concatenate([hn] * MLSTM_HEADS, axis=1), h_all)
        g_tot = bc[L - 1:L, :]
        a_end = g_tot + m_prev
        d_end_c = g_tot - bc + ic
        m_new = jnp.maximum(a_end, jnp.max(d_end_c, axis=0, keepdims=True))
        w_s = jnp.exp(d_end_c - m_new)
        decay = jnp.exp(a_end - m_new)
        wv = (v_m[:, hs] * w_s).astype(BF16)
        c_s[h * D:(h + 1) * D, :] = decay * c_h + _dot_tn(wv, kh)
        n_s[h:h + 1, :] = decay * n_h + jnp.sum(w_s * ks[:, hs], axis=0, keepdims=True)
        m_new_list.append(m_new)
    m_s[...] = jnp.concatenate(m_new_list, axis=1)
    hm = (h_all * normw_ref[...] + skip_ref[...] * xc) * _silu(z)
    hm_ref[...] = hm.astype(hm_ref.dtype)
    co_ref[...] = c_s[...]
    no_ref[...] = n_s[...]
    mo_ref[...] = m_s[...]


def _mlstm(m3, conv0, c0, n0, m0, conv_w, conv_b, wq_bd, wk_bd, w_if, b_if, norm_w, skip, *, chunk, bb):
    b, t, _ = m3.shape
    W = MLSTM_WIDTH
    full = lambda a: pl.BlockSpec(a.shape, lambda bi, j: (0,) * a.ndim)
    per_b = lambda a: pl.BlockSpec((bb,) + a.shape[1:], lambda bi, j: (bi,) + (0,) * (a.ndim - 1))
    consts = [conv_w, conv_b.reshape(1, W), wq_bd, wk_bd, w_if, w_if.T, b_if.reshape(1, -1),
              b_if.reshape(-1, 1), norm_w.reshape(1, W), skip.reshape(1, W)]
    states = [conv0, c0, n0, m0]
    return pl.pallas_call(
        _mlstm_kernel,
        grid=(b // bb, t // chunk),
        in_specs=[pl.BlockSpec((bb, chunk, 3 * W), lambda bi, j: (bi, j, 0))] + [per_b(s) for s in states]
                 + [full(c) for c in consts],
        out_specs=[pl.BlockSpec((bb, chunk, W), lambda bi, j: (bi, j, 0))] + [per_b(s) for s in states],
        out_shape=[jax.ShapeDtypeStruct((b, t, W), BF16)] + [jax.ShapeDtypeStruct(s.shape, F32) for s in states],
        scratch_shapes=[pltpu.VMEM((bb, chunk + 8, W), F32), pltpu.VMEM((bb,) + c0.shape[1:], F32),
                        pltpu.VMEM((bb,) + n0.shape[1:], F32), pltpu.VMEM((bb,) + m0.shape[1:], F32)],
        compiler_params=_cparams("arbitrary", "arbitrary"),
        name="mlstm",
    )(m3, *states, *consts)


def _interleave(ref, scr):
    dil, rows, width = ref.shape
    if dil == 1:
        return ref[0].astype(F32)
    for r in range(dil):
        for c in range(width // 128):
            scr[c, pl.ds(r, rows, stride=dil), :] = ref[r, :, c * 128:(c + 1) * 128].astype(F32)
    return jnp.concatenate([scr[c] for c in range(width // 128)], axis=1)


def _combine(os, lses):
    tm = os[0].shape[0]
    lane = lax.broadcasted_iota(jnp.int32, (tm, GROUP_WIDTH), 1)

    def spread(l2):
        out = jnp.zeros((tm, GROUP_WIDTH), F32)
        for h in range(HEADS_PER_GROUP):
            out = jnp.where(lane // HEAD_DIM == h, l2[:, 32 * h:32 * h + 1], out)
        return out

    ls = [spread(l) for l in lses]
    top = jnp.maximum(jnp.maximum(ls[0], ls[1]), ls[2])
    es = [jnp.exp(l - top) for l in ls]
    tot = es[0] + es[1] + es[2]
    return [(o * (e / tot)).astype(BF16) for o, e in zip(os, es)]


def _outproj_kernel(combine, alpha, *refs):
    if combine:
        (o0, o1, o2, l0, l1, l2, hm_ref, x_ref, wo_ref, g_ref, b_ref, wq_ref, x1_ref, qc_ref, scr) = refs
        att = _combine([_interleave(o, scr) for o in (o0, o1, o2)], [_interleave(l, scr) for l in (l0, l1, l2)])
    else:
        (a_ref, hm_ref, x_ref, wo_ref, g_ref, b_ref, wq_ref, x1_ref, qc_ref) = refs
        att = [a_ref[:, g * GROUP_WIDTH:(g + 1) * GROUP_WIDTH].astype(BF16) for g in range(len(ATT_GROUPS))]
    mix = _dot(hm_ref[...].astype(BF16), wo_ref[ATT_WIDTH:, :])
    for g, a in enumerate(att):
        mix = mix + _dot(a, wo_ref[g * GROUP_WIDTH:(g + 1) * GROUP_WIDTH, :])
    x1 = _layer_norm(alpha * x_ref[...] + mix, g_ref[...], b_ref[...])
    x1_ref[...] = x1
    qc_ref[...] = (_dot(x1.astype(BF16), wq_ref[...]) * ((x1.shape[1] // MEM_HEADS) ** -0.5)).astype(qc_ref.dtype)


def _outproj(att_parts, hm, x, w_out, ln_g, ln_b, w_cq, alpha, *, tm):
    n, d = x.shape
    combine = len(att_parts) > 1
    row = lambda a: pl.BlockSpec((tm, a.shape[1]), lambda i: (i, 0))
    full = lambda a: pl.BlockSpec(a.shape, lambda i: (0, 0))

    def split(a):
        _, dil, ts, w = a.shape
        tiles_per_seq = ts * dil // tm
        return pl.BlockSpec((None, dil, tm // dil, w), lambda i: (i // tiles_per_seq, 0, i % tiles_per_seq, 0))

    consts = [w_out, ln_g.reshape(1, d), ln_b.reshape(1, d), w_cq]
    return pl.pallas_call(
        functools.partial(_outproj_kernel, combine, alpha),
        grid=(n // tm,),
        in_specs=[split(a) if combine else row(a) for a in att_parts] + [row(hm), row(x)] + [full(c) for c in consts],
        out_specs=[pl.BlockSpec((tm, d), lambda i: (i, 0))] * 2,
        out_shape=[jax.ShapeDtypeStruct((n, d), F32), jax.ShapeDtypeStruct((n, d), BF16)],
        scratch_shapes=[pltpu.VMEM((GROUP_WIDTH // 128, tm, 128), F32)] if combine else [],
        compiler_params=_cparams("arbitrary"),
        name="outproj_ln1",
    )(*att_parts, hm, x, *consts)


def _memkv_kernel(mem_ref, wk_ref, wv_ref, kv_ref):
    d = mem_ref.shape[2]
    mem = mem_ref[0].astype(BF16)
    kv_ref[0, :, 0:d] = _dot(mem, wk_ref[...])
    kv_ref[0, :, d:] = _dot(mem, wv_ref[...])


def _memkv(mem, w_ck, w_cv):
    b, m, d = mem.shape
    return pl.pallas_call(
        _memkv_kernel,
        grid=(b,),
        in_specs=[pl.BlockSpec((1, m, d), lambda i: (i, 0, 0)),
                  pl.BlockSpec(w_ck.shape, lambda i: (0, 0)), pl.BlockSpec(w_cv.shape, lambda i: (0, 0))],
        out_specs=pl.BlockSpec((1, m, 2 * d), lambda i: (i, 0, 0)),
        out_shape=jax.ShapeDtypeStruct((b, m, 2 * d), F32),
        compiler_params=_cparams("arbitrary"),
        name="memory_kv",
    )(mem, w_ck, w_cv)


def _xattn_kernel(q_ref, kv_ref, o_ref):
    d = q_ref.shape[2]
    hd = d // MEM_HEADS
    q = q_ref[0]
    for h in range(MEM_HEADS):
        k = kv_ref[0, :, h * hd:(h + 1) * hd].astype(BF16)
        v = kv_ref[0, :, d + h * hd:d + (h + 1) * hd].astype(BF16)
        s = _dot_nt(q[:, h * hd:(h + 1) * hd], k)
        p = jnp.exp(s - jnp.max(s, axis=1, keepdims=True))
        p = p / jnp.sum(p, axis=1, keepdims=True)
        o_ref[0, :, h * hd:(h + 1) * hd] = _dot(p.astype(BF16), v).astype(o_ref.dtype)


def _xattn(qc, kv, *, tq):
    b, t, d = qc.shape
    m = kv.shape[1]
    return pl.pallas_call(
        _xattn_kernel,
        grid=(b, t // tq),
        in_specs=[pl.BlockSpec((1, tq, d), lambda bi, j: (bi, j, 0)),
                  pl.BlockSpec((1, m, 2 * d), lambda bi, j: (bi, 0, 0))],
        out_specs=pl.BlockSpec((1, tq, d), lambda bi, j: (bi, j, 0)),
        out_shape=jax.ShapeDtypeStruct((b, t, d), BF16),
        compiler_params=_cparams("arbitrary", "arbitrary"),
        name="cross_attention",
    )(qc, kv)


def _xout_kernel(alpha, o_ref, x_ref, w_ref, g_ref, b_ref, y_ref):
    y_ref[...] = _layer_norm(alpha * x_ref[...] + _dot(o_ref[...], w_ref[...]), g_ref[...], b_ref[...])


def _xout(oc, x1, w_co, ln_g, ln_b, alpha, *, tm):
    n, d = x1.shape
    row = pl.BlockSpec((tm, d), lambda i: (i, 0))
    full = lambda a: pl.BlockSpec(a.shape, lambda i: (0, 0))
    consts = [w_co, ln_g.reshape(1, d), ln_b.reshape(1, d)]
    return pl.pallas_call(
        functools.partial(_xout_kernel, alpha),
        grid=(n // tm,),
        in_specs=[row, row] + [full(c) for c in consts],
        out_specs=row,
        out_shape=jax.ShapeDtypeStruct((n, d), F32),
        compiler_params=_cparams("arbitrary"),
        name="cross_out_ln2",
    )(oc, x1, *consts)


def _first_index_of_max(vals, idx, big):
    mx = jnp.max(vals, axis=0, keepdims=True)
    return mx, jnp.min(jnp.where(vals == mx, idx, big), axis=0, keepdims=True)


def _route(xb, wrt_ref, br_ref):
    tm = xb.shape[0]
    n_e = wrt_ref.shape[0]
    per_group = n_e // N_EXPERT_GROUPS
    logits = _dot_nt(wrt_ref[...], xb)
    scores = _sigmoid(logits)
    biased = scores + br_ref[...]
    e_idx = lax.broadcasted_iota(jnp.int32, (n_e, tm), 0).astype(F32)
    g_scores = []
    for g in range(N_EXPERT_GROUPS):
        sub = biased[g * per_group:(g + 1) * per_group, :]
        sidx = lax.broadcasted_iota(jnp.int32, (per_group, tm), 0).astype(F32)
        m1, a1 = _first_index_of_max(sub, sidx, per_group)
        m2 = jnp.max(jnp.where(sidx == a1, -jnp.inf, sub), axis=0, keepdims=True)
        g_scores.append(m1 + m2)
    gs = jnp.concatenate(g_scores, axis=0)
    g_idx = lax.broadcasted_iota(jnp.int32, (N_EXPERT_GROUPS, tm), 0).astype(F32)
    g_sel = jnp.zeros((N_EXPERT_GROUPS, tm), F32)
    work = gs
    for _ in range(TOPK_GROUPS):
        _, a = _first_index_of_max(work, g_idx, N_EXPERT_GROUPS)
        hit = g_idx == a
        g_sel = jnp.where(hit, 1.0, g_sel)
        work = jnp.where(hit, -jnp.inf, work)
    e_mask = jnp.concatenate(
        [jnp.broadcast_to(g_sel[g:g + 1, :], (per_group, tm)) for g in range(N_EXPERT_GROUPS)], axis=0)
    work = jnp.where(e_mask > 0.5, biased, -jnp.inf)
    sel = jnp.zeros((n_e, tm), F32)
    picks = []
    for _ in range(TOP_K):
        cand = jnp.where(sel > 0.5, -jnp.inf, work)
        mx = jnp.max(cand, axis=0, keepdims=True)
        a = jnp.min(jnp.where((cand == mx) & (sel < 0.5), e_idx, float(n_e)), axis=0, keepdims=True)
        pick = jnp.where(e_idx == a, 1.0, 0.0)
        picks.append(pick)
        sel = sel + pick
    w_sel = sel * scores
    gates_t = w_sel / jnp.sum(w_sel, axis=0, keepdims=True) * ROUTED_SCALE
    return gates_t, sel, picks


ROW_CHUNK = 16
TILE_CHUNKS = 16
TILE_ROWS = ROW_CHUNK * TILE_CHUNKS
TOKEN_BLOCK = 256
GATHER_AHEAD = 3
TILE_SLOTS = GATHER_AHEAD + 1
SPECULATIVE_PIECES = 4


def _slab_rows(tb, n_e):
    return -(-(TOP_K * tb + n_e * (ROW_CHUNK - 1)) // 128) * 128


def _block_of(refs, first_blocks):
    i = pl.program_id(0)
    x = refs[0][...]
    for ref, first in zip(refs[1:], first_blocks[1:]):
        x = jnp.where(i >= first, ref[...], x)
    return x


def _multi_specs(arrays, tb):
    firsts, specs, start = [], [], 0
    for a in arrays:
        nb = a.shape[0] // tb
        firsts.append(start)
        specs.append(pl.BlockSpec((tb, a.shape[1]),
                                  lambda i, start=start, nb=nb: (jnp.clip(i - start, 0, nb - 1), 0)))
        start += nb
    return firsts, specs, start


def _dispatch_kernel(first_blocks, *refs):
    n_x = len(first_blocks)
    x_refs, (wrt_ref, br_ref, xs_ref, pwt_ref, meta_ref) = refs[:n_x], refs[n_x:]
    tb = x_refs[0].shape[0]
    n_e = wrt_ref.shape[0]
    slab = xs_ref.shape[1]
    xb = _block_of(x_refs, first_blocks).astype(BF16)
    gates_t, sel, picks = _route(xb, wrt_ref, br_ref)
    cnt = jnp.sum(sel, axis=1, keepdims=True)
    padded = jnp.floor((cnt + (ROW_CHUNK - 1)) * (1.0 / ROW_CHUNK)) * ROW_CHUNK
    padded_b = jnp.broadcast_to(padded, (n_e, 128))
    er = lax.broadcasted_iota(jnp.int32, (n_e, n_e), 0)
    ec = lax.broadcasted_iota(jnp.int32, (n_e, n_e), 1)
    off_b = _dot((ec < er).astype(BF16), padded_b.astype(BF16))
    tr = lax.broadcasted_iota(jnp.int32, (tb, tb), 0)
    tc = lax.broadcasted_iota(jnp.int32, (tb, tb), 1)
    rank = _dot(sel.astype(BF16), (tr < tc).astype(BF16))
    dest = off_b[:, 0:1] + rank
    dest_k = [jnp.sum(p * dest, axis=0, keepdims=True) for p in picks]
    gate_k = [jnp.sum(p * gates_t, axis=0, keepdims=True) for p in picks]
    meta_ref[0, 0] = padded_b
    meta_ref[0, 1] = off_b

    stacked = jnp.concatenate(dest_k + [jnp.zeros((8 - TOP_K, tb), F32)] + gate_k
                              + [jnp.zeros((128 - 8 - TOP_K, tb), F32)], axis=0)
    cols = jnp.transpose(stacked)
    step = 512
    for c0 in range(0, slab, step):
        lanes = lax.broadcasted_iota(jnp.int32, (tb, step), 1).astype(F32) + float(c0)
        w = jnp.zeros((tb, step), F32)
        for k in range(TOP_K):
            w = jnp.where(lanes == cols[:, k:k + 1], cols[:, 8 + k:9 + k], w)
        pwt_ref[0, :, c0:c0 + step] = w.astype(pwt_ref.dtype)
        onehot_t = jnp.where(w != 0.0, 1.0, 0.0).astype(BF16)
        xs_ref[0, c0:c0 + step, :] = _dot_tn(onehot_t, xb).astype(xs_ref.dtype)


def _dispatch(xs_in, w_router_t, b_router):
    d = xs_in[0].shape[1]
    n_e = w_router_t.shape[0]
    tb = TOKEN_BLOCK
    firsts, x_specs, nblk = _multi_specs(xs_in, tb)
    slab = _slab_rows(tb, n_e)
    assert slab % 512 == 0
    return pl.pallas_call(
        functools.partial(_dispatch_kernel, tuple(firsts)),
        grid=(nblk,),
        in_specs=x_specs + [pl.BlockSpec(w_router_t.shape, lambda i: (0, 0)),
                            pl.BlockSpec((n_e, 1), lambda i: (0, 0))],
        out_specs=[pl.BlockSpec((1, slab, d), lambda i: (i, 0, 0)),
                   pl.BlockSpec((1, tb, slab), lambda i: (i, 0, 0)),
                   pl.BlockSpec((1, 2, n_e, 128), lambda i: (i, 0, 0, 0))],
        out_shape=[jax.ShapeDtypeStruct((nblk, slab, d), BF16),
                   jax.ShapeDtypeStruct((nblk, tb, slab), BF16),
                   jax.ShapeDtypeStruct((nblk, 2, n_e, 128), F32)],
        compiler_params=_cparams("arbitrary"),
        name="moe_dispatch",
    )(*xs_in, w_router_t, b_router.reshape(n_e, 1))


def _schedule_kernel(n_e, nblk, t_max, zero_row, pieces_ref, first_ref, te_ref, nv_ref, row_ref, nt_ref):
    def per_expert(e, carry):
        pos, tile = carry

        def per_block(blk, p):
            idx = blk * n_e + e
            r0 = first_ref[idx]
            c = pieces_ref[idx]
            for j in range(SPECULATIVE_PIECES):
                row_ref[p + j] = r0 + j * ROW_CHUNK

            def per_piece(j, carry):
                row_ref[p + j] = r0 + j * ROW_CHUNK
                return carry
            lax.fori_loop(SPECULATIVE_PIECES, c, per_piece, 0)
            return p + c

        end = lax.fori_loop(0, nblk, per_block, pos)
        cnt = end - pos
        n_t = (cnt + (TILE_CHUNKS - 1)) // TILE_CHUNKS
        padded_end = pos + n_t * TILE_CHUNKS

        def pad(p, c):
            row_ref[p] = zero_row
            return c
        lax.fori_loop(end, padded_end, pad, 0)

        def per_tile(i, c):
            te_ref[tile + i] = e
            nv_ref[tile + i] = jnp.minimum(TILE_CHUNKS, cnt - i * TILE_CHUNKS)
            return c
        lax.fori_loop(0, n_t, per_tile, 0)
        return padded_end, tile + n_t

    pos, tile = lax.fori_loop(0, n_e, per_expert, (jnp.int32(0), jnp.int32(0)))
    nt_ref[0] = tile
    last_e = te_ref[jnp.maximum(tile - 1, 0)]

    def idle_tile(t, c):
        te_ref[t] = last_e
        nv_ref[t] = 0
        return c
    lax.fori_loop(tile, t_max, idle_tile, 0)

    def idle_piece(p, c):
        row_ref[p] = zero_row
        return c
    lax.fori_loop(pos, t_max * TILE_CHUNKS, idle_piece, 0)


def _tile_schedule(meta, slab):
    nblk, _, n_e, _ = meta.shape
    pieces = (meta[:, 0, :, 0].astype(jnp.int32) // ROW_CHUNK).reshape(-1)
    first_row = (meta[:, 1, :, 0].astype(jnp.int32)
                 + (jnp.arange(nblk, dtype=jnp.int32) * slab)[:, None]).reshape(-1)
    t_max = -(-(TOP_K * nblk * TOKEN_BLOCK + nblk * n_e * (ROW_CHUNK - 1)) // TILE_ROWS) + n_e + TILE_SLOTS
    assert slab - ROW_CHUNK >= TOP_K * TOKEN_BLOCK + n_e * (ROW_CHUNK - 1)
    smem = pl.BlockSpec(memory_space=pltpu.SMEM)
    i32 = lambda n: jax.ShapeDtypeStruct((n,), jnp.int32)
    return pl.pallas_call(
        functools.partial(_schedule_kernel, n_e, nblk, t_max, slab - ROW_CHUNK),
        in_specs=[smem, smem],
        out_specs=[smem, smem, smem, smem],
        out_shape=[i32(t_max), i32(t_max), i32(t_max * TILE_CHUNKS), i32(1)],
        name="moe_schedule",
    )(pieces, first_row)


def _grouped_kernel(te_ref, nv_ref, row_ref, nt_ref, xs_hbm, wg_ref, wu_ref, wd_ref, ys_hbm,
                    lhs, obuf, wg_b, wu_b, wd_b, sem_in, sem_out):
    t = pl.program_id(0)
    n_tiles = nt_ref[0]

    def piece(ref, tile, s):
        row = pl.multiple_of(row_ref[tile * TILE_CHUNKS + s], ROW_CHUNK)
        return ref.at[pl.ds(row, ROW_CHUNK)]

    def local(buf, slot, s):
        start = s * ROW_CHUNK if isinstance(s, int) else pl.multiple_of(s * ROW_CHUNK, ROW_CHUNK)
        return buf.at[slot, pl.ds(start, ROW_CHUNK)]

    def gather(tile, s):
        slot = tile % TILE_SLOTS
        return pltpu.make_async_copy(piece(xs_hbm, tile, s), local(lhs, slot, s), sem_in.at[slot])

    def scatter(tile, s):
        slot = tile % TILE_SLOTS
        return pltpu.make_async_copy(local(obuf, slot, s), piece(ys_hbm, tile, s), sem_out.at[slot])

    def gathered_tile(slot):
        return pltpu.make_async_copy(xs_hbm.at[pl.ds(0, TILE_ROWS)], lhs.at[slot], sem_in.at[slot])

    def scattered_tile(slot):
        return pltpu.make_async_copy(obuf.at[slot], ys_hbm.at[pl.ds(0, TILE_ROWS)], sem_out.at[slot])

    def for_valid_pieces(tile, fn):
        def body(s, carry):
            fn(tile, s)
            return carry
        lax.fori_loop(0, nv_ref[tile], body, 0)

    @pl.when(t == 0)
    def _():
        for tile in range(GATHER_AHEAD):
            for s in range(TILE_CHUNKS):
                gather(tile, s).start()

    @pl.when((t >= TILE_SLOTS) & (t - TILE_SLOTS < n_tiles))
    def _():
        full = nv_ref[t - TILE_SLOTS] == TILE_CHUNKS

        @pl.when(full)
        def _():
            scattered_tile(t % TILE_SLOTS).wait()

        @pl.when(jnp.logical_not(full))
        def _():
            for_valid_pieces(t - TILE_SLOTS, lambda tile, s: scatter(tile, s).wait())

    @pl.when((t < n_tiles) & ((t == 0) | (te_ref[t] != te_ref[jnp.maximum(t - 1, 0)])))
    def _():
        wg_b[...] = wg_ref[0].astype(BF16)
        wu_b[...] = wu_ref[0].astype(BF16)
        wd_b[...] = wd_ref[0].astype(BF16)

    @pl.when(t < n_tiles)
    def _():
        slot = t % TILE_SLOTS
        gathered_tile(slot).wait()
        for s in range(TILE_CHUNKS):
            gather(t + GATHER_AHEAD, s).start()
        x = lhs[slot]
        h = _silu(_dot(x, wg_b[...])) * _dot(x, wu_b[...])
        obuf[slot] = _dot(h.astype(BF16), wd_b[...]).astype(obuf.dtype)

    @pl.when((t >= n_tiles) & (t < n_tiles + GATHER_AHEAD))
    def _():
        gathered_tile(t % TILE_SLOTS).wait()

    @pl.when(t < n_tiles)
    def _():
        full = nv_ref[t] == TILE_CHUNKS

        @pl.when(full)
        def _():
            for s in range(TILE_CHUNKS):
                scatter(t, s).start()

        @pl.when(jnp.logical_not(full))
        def _():
            for_valid_pieces(t, lambda tile, s: scatter(tile, s).start())


def _grouped_experts(xs, tile_e, n_valid, rows_tbl, n_tiles, w_gate, w_up, w_down):
    rows, d = xs.shape
    n_e, _, ff = w_gate.shape
    t_max = tile_e.shape[0]
    grid_spec = pltpu.PrefetchScalarGridSpec(
        num_scalar_prefetch=4,
        grid=(t_max,),
        in_specs=[pl.BlockSpec(memory_space=pl.ANY),
                  pl.BlockSpec((1, d, ff), lambda t, te, nv, sr, nt: (te[t], 0, 0)),
                  pl.BlockSpec((1, d, ff), lambda t, te, nv, sr, nt: (te[t], 0, 0)),
                  pl.BlockSpec((1, ff, d), lambda t, te, nv, sr, nt: (te[t], 0, 0))],
        out_specs=pl.BlockSpec(memory_space=pl.ANY),
        scratch_shapes=[pltpu.VMEM((TILE_SLOTS, TILE_ROWS, d), BF16), pltpu.VMEM((TILE_SLOTS, TILE_ROWS, d), BF16),
                        pltpu.VMEM((d, ff), BF16), pltpu.VMEM((d, ff), BF16), pltpu.VMEM((ff, d), BF16),
                        pltpu.SemaphoreType.DMA((TILE_SLOTS,)), pltpu.SemaphoreType.DMA((TILE_SLOTS,))],
    )
    return pl.pallas_call(
        _grouped_kernel,
        grid_spec=grid_spec,
        out_shape=jax.ShapeDtypeStruct((rows, d), BF16),
        input_output_aliases={4: 0},
        compiler_params=_cparams("arbitrary"),
        name="moe_grouped_experts",
    )(tile_e, n_valid, rows_tbl, n_tiles, xs, w_gate, w_up, w_down)


def _moe_out_kernel(alpha, first_blocks, pwt_ref, ys_ref, *refs):
    n_x = len(first_blocks)
    x_refs, (sg_ref, su_ref, sd_ref, g_ref, b_ref), y_refs = refs[:n_x], refs[n_x:n_x + 5], refs[n_x + 5:]
    i = pl.program_id(0)
    routed = _dot(pwt_ref[0], ys_ref[0])
    x = _block_of(x_refs, first_blocks)
    xb = x.astype(BF16)
    hs = _silu(_dot(xb, sg_ref[...])) * _dot(xb, su_ref[...])
    shared = _dot(hs.astype(BF16), sd_ref[...])
    y = _layer_norm(alpha * x + (routed + shared), g_ref[...], b_ref[...])
    bounds = list(first_blocks[1:]) + [pl.num_programs(0)]
    for y_ref, lo, hi in zip(y_refs, first_blocks, bounds):
        @pl.when((i >= lo) & (i < hi))
        def _(y_ref=y_ref):
            y_ref[...] = y


def _moe_out(pwt, ys, xs_in, ws_gate, ws_up, ws_down, ln_g, ln_b, alpha):
    d = xs_in[0].shape[1]
    nblk, tb, slab = pwt.shape
    firsts, x_specs, _ = _multi_specs(xs_in, tb)
    consts = [ws_gate, ws_up, ws_down, ln_g.reshape(1, d), ln_b.reshape(1, d)]
    return pl.pallas_call(
        functools.partial(_moe_out_kernel, alpha, tuple(firsts)),
        grid=(nblk,),
        in_specs=[pl.BlockSpec((1, tb, slab), lambda i: (i, 0, 0)),
                  pl.BlockSpec((1, slab, d), lambda i: (i, 0, 0))] + x_specs
                 + [pl.BlockSpec(c.shape, lambda i: (0, 0)) for c in consts],
        out_specs=x_specs,
        out_shape=[jax.ShapeDtypeStruct(x.shape, F32) for x in xs_in],
        compiler_params=_cparams("arbitrary"),
        name="moe_combine",
    )(pwt, ys.reshape(nblk, slab, d), *xs_in, *consts)


def _moe(xs_in, w_router_t, b_router, w_gate, w_up, w_down, ws_gate, ws_up, ws_down, ln_g, ln_b, alpha):
    xs, pwt, meta = _dispatch(xs_in, w_router_t, b_router)
    nblk, slab, d = xs.shape
    tile_e, n_valid, rows_tbl, n_tiles = _tile_schedule(meta, slab)
    ys = _grouped_experts(xs.reshape(nblk * slab, d), tile_e, n_valid, rows_tbl, n_tiles, w_gate, w_up, w_down)
    return _moe_out(pwt, ys, xs_in, ws_gate, ws_up, ws_down, ln_g, ln_b, alpha)


def _rope_tables(pos):
    half = HEAD_DIM // 2
    inv_freq = ROPE_THETA ** (-jnp.arange(half, dtype=F32) / half)
    ang = pos.astype(F32)[:, None] * inv_freq[None, :]
    cos, sin = jnp.cos(ang), jnp.sin(ang)
    return jnp.tile(jnp.concatenate([cos, cos], axis=1), (1, 2)), jnp.tile(jnp.concatenate([-sin, sin], axis=1), (1, 2))


def _block_diag(w):
    h, d, _ = w.shape
    out = jnp.zeros((h * d, h * d), w.dtype)
    for i in range(h):
        out = out.at[i * d:(i + 1) * d, i * d:(i + 1) * d].set(w[i])
    return out


def _pick(n, pref):
    return pref if n % pref == 0 else n


def kernel(x_prompt, x_sample, mem_prompt, cache_win128, cache_win512, cache_win2048, cache_mem_kv, state_conv, state_C, state_n, state_m, w_in, conv_w, conv_b, wq_m, wk_m, w_if, b_if, mh_norm_w, skip_m, w_out, ln1_g, ln1_b, w_cq, w_ck, w_cv, w_co, ln2_g, ln2_b, w_router, b_router, w_gate, w_up, w_down, ws_gate, ws_up, ws_down, ln3_g, ln3_b):
    depth = w_in.shape[0]
    assert depth == 1
    alpha = float((2 * depth) ** 0.25)
    bp, seq, d = x_prompt.shape
    bs, dec, _ = x_sample.shape
    assert seq % ATT_GROUPS[-1][0] == 0
    for c, (window, _) in zip((cache_win128, cache_win512, cache_win2048), ATT_GROUPS):
        assert c.shape[2] == window

    l = 0
    bf = lambda a: a.astype(BF16)
    w_in_b, w_out_b = bf(w_in[l]), bf(w_out[l])
    w_cq_b, w_ck_b, w_cv_b, w_co_b = bf(w_cq[l]), bf(w_ck[l]), bf(w_cv[l]), bf(w_co[l])
    wq_bd, wk_bd = bf(_block_diag(wq_m[l])), bf(_block_diag(wk_m[l]))
    w_router_t = bf(w_router[l].T)
    ws_gate_b, ws_up_b, ws_down_b = bf(ws_gate[l]), bf(ws_up[l]), bf(ws_down[l])

    def tail_of_layer(x1, qc, kv, batch, t, tm, tq):
        n = batch * t
        oc = _xattn(qc.reshape(batch, t, d), kv, tq=tq).reshape(n, d)
        return _xout(oc, x1, w_co_b, ln2_g[l], ln2_b[l], alpha, tm=tm)

    def mlstm(m3, batch, t, states, chunk, bb):
        return _mlstm(m3.reshape(batch, t, -1), *states, conv_w[l], conv_b[l], wq_bd, wk_bd, w_if[l], b_if[l],
                      mh_norm_w[l], skip_m[l], chunk=chunk, bb=bb)

    np_ = bp * seq
    xp = x_prompt.reshape(np_, d)
    cos_p, sin_p = _rope_tables(jnp.arange(seq))
    *qkv, m3, t128, t512, t2048 = _inproj_prompt(xp, w_in_b, cos_p, sin_p, seq=seq, tm=512)
    parts, lses = [], []
    for g in range(len(ATT_GROUPS)):
        o, lse = _win_attn(*qkv[3 * g:3 * g + 3])
        parts.append(o)
        lses.append(lse)
    zeros_p = [jnp.zeros((bp, CONV_WIDTH - 1, MLSTM_WIDTH), F32), jnp.zeros((bp, MLSTM_WIDTH, HEAD_DIM), F32),
               jnp.zeros((bp, MLSTM_HEADS, HEAD_DIM), F32), jnp.zeros((bp, 1, MLSTM_HEADS), F32)]
    hm_p, p_conv, p_c, p_n, p_m = mlstm(m3, bp, seq, zeros_p, 256, bp)
    x1, qc = _outproj(parts + lses, hm_p.reshape(np_, -1), xp, w_out_b, ln1_g[l], ln1_b[l], w_cq_b, alpha, tm=256)
    kv_p = _memkv(mem_prompt, w_ck_b, w_cv_b)
    x2_p = tail_of_layer(x1, qc, kv_p, bp, seq, 256, 512)

    ns = bs * dec
    xs = x_sample.reshape(ns, d)
    cos_s, sin_s = _rope_tables(jnp.tile(PAST_LEN + jnp.arange(dec), bs))
    qs, ks, vs, m3s = _inproj(xs, w_in_b, cos_s, sin_s, tm=ns)
    caches = [c[l].reshape(bs, c.shape[2], 2 * GROUP_WIDTH) for c in (cache_win128, cache_win512, cache_win2048)]
    att_s, s128, s512, s2048 = _dec_attn(qs.reshape(bs, dec, -1), ks.reshape(bs, dec, -1), vs.reshape(bs, dec, -1),
                                         caches)
    states_s = [state_conv[l], state_C[l].reshape(bs, MLSTM_WIDTH, HEAD_DIM), state_n[l],
                state_m[l].reshape(bs, 1, MLSTM_HEADS)]
    hm_s, s_conv, s_c, s_n, s_m = mlstm(m3s, bs, dec, states_s, dec, 4 if bs % 4 == 0 else 1)
    x1s, qcs = _outproj([att_s.reshape(ns, -1)], hm_s.reshape(ns, -1), xs, w_out_b, ln1_g[l], ln1_b[l], w_cq_b,
                        alpha, tm=ns)
    kv_s = cache_mem_kv[l].reshape(bs, cache_mem_kv.shape[2], 2 * d)
    x2_s = tail_of_layer(x1s, qcs, kv_s, bs, dec, ns, dec)

    y_p, y_s = _moe([x2_p, x2_s], w_router_t, b_router[l], w_gate[l], w_up[l], w_down[l],
                    ws_gate_b, ws_up_b, ws_down_b, ln3_g[l], ln3_b[l], alpha)

    win_shape = lambda a, b_: a.reshape(1, b_, a.shape[1], 2, HEADS_PER_GROUP, HEAD_DIM)
    return (y_p.reshape(bp, seq, d), y_s.reshape(bs, dec, d),
            win_shape(t128, bp), win_shape(t512, bp), win_shape(t2048, bp),
            kv_p.reshape(1, bp, mem_prompt.shape[1], 2, MEM_HEADS, d // MEM_HEADS),
            p_conv[None], p_c.reshape(1, bp, MLSTM_HEADS, HEAD_DIM, HEAD_DIM), p_n[None],
            p_m.reshape(1, bp, MLSTM_HEADS),
            win_shape(s128, bs), win_shape(s512, bs), win_shape(s2048, bs),
            s_conv[None], s_c.reshape(1, bs, MLSTM_HEADS, HEAD_DIM, HEAD_DIM), s_n[None],
            s_m.reshape(1, bs, MLSTM_HEADS))
```

```python
import functools
import math

import jax
import jax.numpy as jnp
from jax import lax
from jax.experimental import pallas as pl
from jax.experimental.pallas import tpu as pltpu

F32 = jnp.float32
BF16 = jnp.bfloat16

HEAD_DIM = 64
ATT_GROUPS = ((128, 1), (512, 4), (2048, 16))
HEADS_PER_GROUP = 4
GROUP_WIDTH = HEADS_PER_GROUP * HEAD_DIM
ATT_WIDTH = GROUP_WIDTH * len(ATT_GROUPS)
N_KEYS = 129
ROPE_THETA = 10000.0
PAST_LEN = 8192
MLSTM_HEADS = 4
MLSTM_WIDTH = MLSTM_HEADS * HEAD_DIM
CONV_WIDTH = 4
MEM_HEADS = 4
N_EXPERT_GROUPS = 8
TOPK_GROUPS = 4
TOP_K = 6
ROUTED_SCALE = 2.5
LN_EPS = 1e-5
NEG = -1e30
VMEM_LIMIT = 56 * 1024 * 1024


def _cparams(*sem):
    return pltpu.CompilerParams(dimension_semantics=sem, vmem_limit_bytes=VMEM_LIMIT)


def _dot(a, b):
    return jnp.dot(a, b, preferred_element_type=F32)


def _dot_nt(a, b, precision=None):
    return lax.dot_general(a, b, (((1,), (1,)), ((), ())), precision=precision,
                           preferred_element_type=F32)


def _dot_tn(a, b):
    return lax.dot_general(a, b, (((0,), (0,)), ((), ())), preferred_element_type=F32)


def _dot_hi(a, b):
    return jnp.dot(a, b, precision=lax.Precision.HIGHEST, preferred_element_type=F32)


def _layer_norm(x, g, b):
    mu = jnp.mean(x, axis=-1, keepdims=True)
    xc = x - mu
    var = jnp.mean(xc * xc, axis=-1, keepdims=True)
    return xc * lax.rsqrt(var + LN_EPS) * g + b


def _sigmoid(x):
    return 1.0 / (1.0 + jnp.exp(-x))


def _silu(x):
    return x * _sigmoid(x)


def _log_sigmoid(x):
    return jnp.minimum(x, 0.0) - jnp.log(1.0 + jnp.exp(-jnp.abs(x)))


def _project_qkv(x_ref, w_ref, cos_ref, sin_ref):
    tm = x_ref.shape[0]
    x = x_ref[...].astype(BF16)
    cos = jnp.concatenate([cos_ref[...]] * (ATT_WIDTH // 128), axis=1)
    sin = jnp.concatenate([sin_ref[...]] * (ATT_WIDTH // 128), axis=1)
    lane = lax.broadcasted_iota(jnp.int32, (tm, ATT_WIDTH), 1)
    first_half = (lane % HEAD_DIM) < (HEAD_DIM // 2)

    def rope(t):
        fwd = pltpu.roll(t, ATT_WIDTH - HEAD_DIM // 2, 1)
        bwd = pltpu.roll(t, HEAD_DIM // 2, 1)
        return t * cos + jnp.where(first_half, fwd, bwd) * sin

    q = rope(_dot(x, w_ref[:, 0:ATT_WIDTH])) * (HEAD_DIM ** -0.5)
    k = rope(_dot(x, w_ref[:, ATT_WIDTH:2 * ATT_WIDTH]))
    v = _dot(x, w_ref[:, 2 * ATT_WIDTH:3 * ATT_WIDTH])
    return q, k, v, _dot(x, w_ref[:, 3 * ATT_WIDTH:])


def _inproj_kernel(x_ref, w_ref, cos_ref, sin_ref, q_ref, k_ref, v_ref, m_ref):
    q, k, v, m = _project_qkv(x_ref, w_ref, cos_ref, sin_ref)
    q_ref[...] = q.astype(q_ref.dtype)
    k_ref[...] = k.astype(k_ref.dtype)
    v_ref[...] = v.astype(v_ref.dtype)
    m_ref[...] = m


def _inproj_prompt_kernel(x_ref, w_ref, cos_ref, sin_ref, *refs):
    n_g = len(ATT_GROUPS)
    qkv_refs, m_ref, tails, scr = refs[:3 * n_g], refs[3 * n_g], refs[3 * n_g + 1:4 * n_g + 1], refs[4 * n_g + 1]
    tm = x_ref.shape[0]
    q, k, v, m = _project_qkv(x_ref, w_ref, cos_ref, sin_ref)
    m_ref[...] = m
    for g, (_, dil) in enumerate(ATT_GROUPS):
        cols = slice(g * GROUP_WIDTH, (g + 1) * GROUP_WIDTH)
        for a, val in enumerate((q, k, v)):
            o_ref = qkv_refs[3 * g + a]
            if dil == 1:
                o_ref[0, 0] = val[:, cols].astype(o_ref.dtype)
                continue
            for c in range(GROUP_WIDTH // 128):
                scr[c] = val[:, g * GROUP_WIDTH + c * 128:g * GROUP_WIDTH + (c + 1) * 128]
            for r in range(dil):
                for c in range(GROUP_WIDTH // 128):
                    o_ref[0, r, :, c * 128:(c + 1) * 128] = scr[c, pl.ds(r, tm // dil, stride=dil), :].astype(o_ref.dtype)
        t_ref = tails[g]
        rows = t_ref.shape[1]
        t_ref[0, :, 0:GROUP_WIDTH] = k[tm - rows:, cols]
        t_ref[0, :, GROUP_WIDTH:] = v[tm - rows:, cols]


def _inproj_prompt(x, w_in, cos, sin, *, seq, tm):
    n, d = x.shape
    tiles_per_seq = seq // tm
    batch = n // seq
    out_shape, out_specs = [], []
    for _, dil in ATT_GROUPS:
        for _ in range(3):
            out_shape.append(jax.ShapeDtypeStruct((batch, dil, seq // dil, GROUP_WIDTH), BF16))
            out_specs.append(pl.BlockSpec((1, dil, tm // dil, GROUP_WIDTH),
                                          lambda i: (i // tiles_per_seq, 0, i % tiles_per_seq, 0)))
    m_width = w_in.shape[1] - 3 * ATT_WIDTH
    out_shape.append(jax.ShapeDtypeStruct((n, m_width), F32))
    out_specs.append(pl.BlockSpec((tm, m_width), lambda i: (i, 0)))
    for window, _ in ATT_GROUPS:
        rows = min(window, tm)
        first = tiles_per_seq - window // rows if window > rows else tiles_per_seq - 1
        out_shape.append(jax.ShapeDtypeStruct((batch, min(window, seq), 2 * GROUP_WIDTH), F32))
        out_specs.append(pl.BlockSpec(
            (1, rows, 2 * GROUP_WIDTH),
            lambda i, first=first: (i // tiles_per_seq, jnp.maximum(i % tiles_per_seq - first, 0), 0)))
    return pl.pallas_call(
        _inproj_prompt_kernel,
        grid=(n // tm,),
        in_specs=[pl.BlockSpec((tm, d), lambda i: (i, 0)),
                  pl.BlockSpec(w_in.shape, lambda i: (0, 0)),
                  pl.BlockSpec((tm, 128), lambda i: (i % tiles_per_seq, 0)),
                  pl.BlockSpec((tm, 128), lambda i: (i % tiles_per_seq, 0))],
        out_specs=out_specs,
        out_shape=out_shape,
        scratch_shapes=[pltpu.VMEM((GROUP_WIDTH // 128, tm, 128), F32)],
        compiler_params=_cparams("arbitrary"),
        name="inproj_rope_prompt",
    )(x, w_in, cos, sin)


def _inproj(x, w_in, cos, sin, *, tm):
    n, d = x.shape
    out_shape = [jax.ShapeDtypeStruct((n, ATT_WIDTH), F32)] * 3 + [
        jax.ShapeDtypeStruct((n, w_in.shape[1] - 3 * ATT_WIDTH), F32)]
    row_spec = lambda w: pl.BlockSpec((tm, w), lambda i: (i, 0))
    out_specs = [row_spec(ATT_WIDTH)] * 3 + [row_spec(w_in.shape[1] - 3 * ATT_WIDTH)]
    tiles_per_seq = cos.shape[0] // tm
    return pl.pallas_call(
        _inproj_kernel,
        grid=(n // tm,),
        in_specs=[row_spec(d),
                  pl.BlockSpec(w_in.shape, lambda i: (0, 0)),
                  pl.BlockSpec((tm, 128), lambda i: (i % tiles_per_seq, 0)),
                  pl.BlockSpec((tm, 128), lambda i: (i % tiles_per_seq, 0))],
        out_specs=out_specs,
        out_shape=out_shape,
        compiler_params=_cparams("arbitrary"),
        name="inproj_rope",
    )(x, w_in, cos, sin)


def _win_attn_kernel(q_ref, kp_ref, kc_ref, vp_ref, vc_ref, o_ref, lse_ref):
    j = pl.program_id(2)
    tq = q_ref.shape[1]
    sub = N_KEYS - 1
    assert kp_ref.shape[1] == sub and tq % sub == 0
    k_all = jnp.concatenate([kp_ref[0], kc_ref[0]], axis=0)
    v_all = jnp.concatenate([vp_ref[0], vc_ref[0]], axis=0)
    n_h = HEADS_PER_GROUP
    row = lax.broadcasted_iota(jnp.int32, (n_h * sub, 2 * sub), 0) % sub
    col = lax.broadcasted_iota(jnp.int32, (n_h * sub, 2 * sub), 1)
    band = (col >= row) & (col <= row + sub)
    head_rows = lax.broadcasted_iota(jnp.int32, (n_h * sub, GROUP_WIDTH), 0) // sub
    head_lanes = lax.broadcasted_iota(jnp.int32, (n_h * sub, GROUP_WIDTH), 1) // HEAD_DIM
    own = head_rows == head_lanes
    lane_l = lax.broadcasted_iota(jnp.int32, (sub, 128), 1)
    for i in range(tq // sub):
        q = q_ref[0, i * sub:(i + 1) * sub, :]
        kk = k_all[i * sub:(i + 2) * sub]
        vv = v_all[i * sub:(i + 2) * sub]
        valid = band & ((j > 0) | (col >= sub)) if i == 0 else band
        q4 = jnp.concatenate([q] * n_h, axis=0)
        q4 = jnp.where(own, q4, jnp.zeros_like(q4))
        s = jnp.where(valid, _dot_nt(q4, kk), NEG)
        m = jnp.max(s, axis=1, keepdims=True)
        p = jnp.exp(s - m)
        l = jnp.sum(p, axis=1, keepdims=True)
        o4 = jnp.where(own, _dot(p.astype(BF16), vv) / l, 0.0)
        lse4 = m + jnp.log(l)
        o_acc = o4[0:sub]
        lse_acc = jnp.zeros((sub, 128), F32)
        for h in range(n_h):
            if h:
                o_acc = o_acc + o4[h * sub:(h + 1) * sub]
            lse_acc = jnp.where(lane_l // 32 == h, lse4[h * sub:(h + 1) * sub], lse_acc)
        o_ref[0, i * sub:(i + 1) * sub, :] = o_acc.astype(o_ref.dtype)
        lse_ref[0, i * sub:(i + 1) * sub, :] = lse_acc


def _win_attn(q, k, v, *, tq=512):
    b, dil, ts, _ = q.shape
    tq = min(tq, ts)
    sub = N_KEYS - 1
    cur = pl.BlockSpec((None, 1, tq, GROUP_WIDTH), lambda bi, r, j: (bi, r, j, 0))
    prev = pl.BlockSpec((None, 1, sub, GROUP_WIDTH),
                        lambda bi, r, j: (bi, r, jnp.maximum(j * (tq // sub) - 1, 0), 0))
    return pl.pallas_call(
        _win_attn_kernel,
        grid=(b, dil, ts // tq),
        in_specs=[cur, prev, cur, prev, cur],
        out_specs=[pl.BlockSpec((None, 1, tq, GROUP_WIDTH), lambda bi, r, j: (bi, r, j, 0)),
                   pl.BlockSpec((None, 1, tq, 128), lambda bi, r, j: (bi, r, j, 0))],
        out_shape=[jax.ShapeDtypeStruct((b, dil, ts, GROUP_WIDTH), BF16),
                   jax.ShapeDtypeStruct((b, dil, ts, 128), F32)],
        compiler_params=_cparams("arbitrary", "arbitrary", "arbitrary"),
        name="window_attention",
    )(q, k, k, v, v)


def _dec_attn_kernel(q_ref, k_ref, v_ref, c0_ref, c1_ref, c2_ref, att_ref, o0_ref, o1_ref, o2_ref,
                     e0_ref, e1_ref, e2_ref):
    t_new = q_ref.shape[1]
    q = q_ref[0]
    k_new = k_ref[0]
    v_new = v_ref[0]
    lane = lax.broadcasted_iota(jnp.int32, (8, GROUP_WIDTH), 1)
    sub = lax.broadcasted_iota(jnp.int32, (8, GROUP_WIDTH), 0)
    head_sel = (lane // HEAD_DIM) == sub
    outs = [[None] * t_new for _ in ATT_GROUPS]
    lses = [[None] * t_new for _ in ATT_GROUPS]
    for g, ((window, dil), c_ref, o_ref, e_ref) in enumerate(
            zip(ATT_GROUPS, (c0_ref, c1_ref, c2_ref), (o0_ref, o1_ref, o2_ref), (e0_ref, e1_ref, e2_ref))):
        w = c_ref.shape[1]
        cols = slice(g * GROUP_WIDTH, (g + 1) * GROUP_WIDTH)
        kv_new = jnp.concatenate([k_new[:, cols], v_new[:, cols]], axis=1)
        n_chunk = e_ref.shape[0]
        for c in range(n_chunk):
            e_ref[c, 0:w, :] = c_ref[0, :, c * 128:(c + 1) * 128]
            e_ref[c, w:w + t_new, :] = kv_new[:, c * 128:(c + 1) * 128]
            o_ref[0, :, c * 128:(c + 1) * 128] = e_ref[c, t_new:w + t_new, :]
        for t in range(t_new):
            past = jnp.concatenate(
                [e_ref[c, pl.ds(t, N_KEYS - 1, stride=dil), :] for c in range(n_chunk)], axis=1)
            qm = jnp.where(head_sel, jnp.broadcast_to(q[t:t + 1, cols], (8, GROUP_WIDTH)), 0.0)
            s = _dot_nt(qm.astype(BF16), past[:, 0:GROUP_WIDTH].astype(BF16))
            k_self = kv_new[t:t + 1, 0:GROUP_WIDTH]
            v_self = kv_new[t:t + 1, GROUP_WIDTH:]
            s_self = jnp.sum(qm * k_self, axis=1, keepdims=True)
            m = jnp.maximum(jnp.max(s, axis=1, keepdims=True), s_self)
            p = jnp.exp(s - m)
            p_self = jnp.exp(s_self - m)
            l = jnp.sum(p, axis=1, keepdims=True) + p_self
            o = _dot(p.astype(BF16), past[:, GROUP_WIDTH:].astype(BF16))
            o = (o + p_self * v_self) / l
            outs[g][t] = o
            lses[g][t] = m + jnp.log(l)
    for t in range(t_new):
        top = jnp.maximum(jnp.maximum(lses[0][t], lses[1][t]), lses[2][t])
        es = [jnp.exp(lses[g][t] - top) for g in range(len(ATT_GROUPS))]
        tot = es[0] + es[1] + es[2]
        for g in range(len(ATT_GROUPS)):
            weighted = jnp.where(head_sel, outs[g][t] * (es[g] / tot), 0.0)
            att_ref[0, t:t + 1, g * GROUP_WIDTH:(g + 1) * GROUP_WIDTH] = jnp.sum(weighted, axis=0, keepdims=True)


def _dec_attn(q, k, v, caches):
    b, t_new, _ = q.shape
    tok = pl.BlockSpec((1, t_new, ATT_WIDTH), lambda i: (i, 0, 0))
    cspec = [pl.BlockSpec((1,) + c.shape[1:], lambda i: (i, 0, 0)) for c in caches]
    return pl.pallas_call(
        _dec_attn_kernel,
        grid=(b,),
        in_specs=[tok, tok, tok] + cspec,
        out_specs=[tok] + cspec,
        out_shape=[jax.ShapeDtypeStruct(q.shape, F32)] + [jax.ShapeDtypeStruct(c.shape, F32) for c in caches],
        scratch_shapes=[pltpu.VMEM((c.shape[2] // 128, c.shape[1] + t_new, 128), F32) for c in caches],
        compiler_params=_cparams("arbitrary"),
        name="decode_attention",
    )(q, k, v, *caches)


def _mlstm_kernel(m3_ref, conv0_ref, c0_ref, n0_ref, m0_ref, *rest):
    consts, (hm_ref, convo_ref, co_ref, no_ref, mo_ref, cbuf, c_s, n_s, m_s) = rest[:10], rest[10:]
    for b in range(m3_ref.shape[0]):
        _mlstm_chunk(m3_ref.at[b], conv0_ref.at[b], c0_ref.at[b], n0_ref.at[b], m0_ref.at[b], *consts,
                     hm_ref.at[b], convo_ref.at[b], co_ref.at[b], no_ref.at[b], mo_ref.at[b],
                     cbuf.at[b], c_s.at[b], n_s.at[b], m_s.at[b])


def _mlstm_chunk(m3_ref, conv0_ref, c0_ref, n0_ref, m0_ref, convw_ref, convb_ref, wq_ref, wk_ref,
                 wif_ref, wift_ref, bif_ref, bift_ref, normw_ref, skip_ref,
                 hm_ref, convo_ref, co_ref, no_ref, mo_ref, cbuf, c_s, n_s, m_s):
    j = pl.program_id(1)
    L = m3_ref.shape[0]
    W = MLSTM_WIDTH
    D = HEAD_DIM

    @pl.when(j == 0)
    def _():
        cbuf[0:8, :] = jnp.zeros((8, W), F32)
        cbuf[8 - (CONV_WIDTH - 1):8, :] = conv0_ref[...]
        c_s[...] = c0_ref[...]
        n_s[...] = n0_ref[...]
        m_s[...] = m0_ref[...]

    blk = m3_ref[...]
    c_in = blk[:, 0:W]
    v_m = blk[:, W:2 * W]
    z = blk[:, 2 * W:3 * W]
    cbuf[8:8 + L, :] = c_in
    acc = jnp.zeros((L, W), F32) + convb_ref[...]
    for tap in range(CONV_WIDTH):
        off = 8 - (CONV_WIDTH - 1) + tap
        acc = acc + cbuf[off:off + L, :] * convw_ref[tap:tap + 1, :]
    xc = _silu(acc)
    convo_ref[...] = cbuf[8 + L - (CONV_WIDTH - 1):8 + L, :]
    cbuf[0:8, :] = cbuf[L:L + 8, :]

    xcb = xc.astype(BF16)
    q_m = _dot(xcb, wq_ref[...])
    k_m = _dot(xcb, wk_ref[...])
    gate_in = jnp.concatenate([q_m, k_m, v_m], axis=1)
    g_col = _dot_hi(gate_in, wif_ref[...]) + bif_ref[...]
    g_row = _dot_nt(wift_ref[...], gate_in, precision=lax.Precision.HIGHEST) + bift_ref[...]
    i_col, lf_col = g_col[:, 0:MLSTM_HEADS], _log_sigmoid(g_col[:, MLSTM_HEADS:])
    i_row, lf_row = g_row[0:MLSTM_HEADS, :], _log_sigmoid(g_row[MLSTM_HEADS:, :])
    rr = lax.broadcasted_iota(jnp.int32, (L, L), 0)
    cc = lax.broadcasted_iota(jnp.int32, (L, L), 1)
    causal = cc <= rr
    tri = causal.astype(F32)
    b_col = _dot_hi(tri, lf_col)
    b_row = _dot_nt(lf_row, tri, precision=lax.Precision.HIGHEST)
    ks = k_m * (D ** -0.5)
    qb = q_m.astype(BF16)
    kb = ks.astype(BF16)
    vb = v_m.astype(BF16)
    lane = lax.broadcasted_iota(jnp.int32, (L, W), 1)
    m_prev_all = m_s[...]
    h_all = jnp.zeros((L, W), F32)
    m_new_list = []
    for h in range(MLSTM_HEADS):
        hs = slice(h * D, (h + 1) * D)
        m_prev = m_prev_all[:, h:h + 1]
        bc = b_col[:, h:h + 1]
        br = b_row[h:h + 1, :]
        ir = i_row[h:h + 1, :]
        ic = i_col[:, h:h + 1]
        d_intra = jnp.where(causal, bc - br + ir, -jnp.inf)
        a_inter = bc + m_prev
        m_t = jnp.maximum(a_inter, jnp.max(d_intra, axis=1, keepdims=True))
        qh, kh, vh = qb[:, hs], kb[:, hs], vb[:, hs]
        s = _dot_nt(qh, kh) * jnp.exp(d_intra - m_t)
        w_inter = jnp.exp(a_inter - m_t)
        c_h = c_s[h * D:(h + 1) * D, :]
        n_h = n_s[h:h + 1, :]
        num = _dot(s.astype(BF16), vh) + w_inter * _dot_nt(qh, c_h.astype(BF16))
        den = jnp.sum(s, axis=1, keepdims=True) + w_inter * jnp.sum(q_m[:, hs] * n_h, axis=1, keepdims=True)
        hh = num / jnp.maximum(jnp.abs(den), jnp.exp(-m_t))
        mu = jnp.mean(hh, axis=1, keepdims=True)
        hc = hh - mu
        var = jnp.mean(hc * hc, axis=1, keepdims=True)
        hn = hc * lax.rsqrt(var + LN_EPS)
        h_all = jnp.where(lane // D == h, jnp.concatenate([hn] * MLSTM_HEADS, axis=1), h_all)
        g_tot = bc[L - 1:L, :]
        a_end = g_tot + m_prev
        d_end_c = g_tot - bc + ic
        m_new = jnp.maximum(a_end, jnp.max(d_end_c, axis=0, keepdims=True))
        w_s = jnp.exp(d_end_c - m_new)
        decay = jnp.exp(a_end - m_new)
        wv = (v_m[:, hs] * w_s).astype(BF16)
        c_s[h * D:(h + 1) * D, :] = decay * c_h + _dot_tn(wv, kh)
        n_s[h:h + 1, :] = decay * n_h + jnp.sum(w_s * ks[:, hs], axis=0, keepdims=True)
        m_new_list.append(m_new)
    m_s[...] = jnp.concatenate(m_new_list, axis=1)
    hm = (h_all * normw_ref[...] + skip_ref[...] * xc) * _silu(z)
    hm_ref[...] = hm.astype(hm_ref.dtype)
    co_ref[...] = c_s[...]
    no_ref[...] = n_s[...]
    mo_ref[...] = m_s[...]


def _mlstm(m3, conv0, c0, n0, m0, conv_w, conv_b, wq_bd, wk_bd, w_if, b_if, norm_w, skip, *, chunk, bb):
    b, t, _ = m3.shape
    W = MLSTM_WIDTH
    full = lambda a: pl.BlockSpec(a.shape, lambda bi, j: (0,) * a.ndim)
    per_b = lambda a: pl.BlockSpec((bb,) + a.shape[1:], lambda bi, j: (bi,) + (0,) * (a.ndim - 1))
    consts = [conv_w, conv_b.reshape(1, W), wq_bd, wk_bd, w_if, w_if.T, b_if.reshape(1, -1),
              b_if.reshape(-1, 1), norm_w.reshape(1, W), skip.reshape(1, W)]
    states = [conv0, c0, n0, m0]
    return pl.pallas_call(
        _mlstm_kernel,
        grid=(b // bb, t // chunk),
        in_specs=[pl.BlockSpec((bb, chunk, 3 * W), lambda bi, j: (bi, j, 0))] + [per_b(s) for s in states]
                 + [full(c) for c in consts],
        out_specs=[pl.BlockSpec((bb, chunk, W), lambda bi, j: (bi, j, 0))] + [per_b(s) for s in states],
        out_shape=[jax.ShapeDtypeStruct((b, t, W), BF16)] + [jax.ShapeDtypeStruct(s.shape, F32) for s in states],
        scratch_shapes=[pltpu.VMEM((bb, chunk + 8, W), F32), pltpu.VMEM((bb,) + c0.shape[1:], F32),
                        pltpu.VMEM((bb,) + n0.shape[1:], F32), pltpu.VMEM((bb,) + m0.shape[1:], F32)],
        compiler_params=_cparams("arbitrary", "arbitrary"),
        name="mlstm",
    )(m3, *states, *consts)


def _interleave(ref, scr):
    dil, rows, width = ref.shape
    if dil == 1:
        return ref[0].astype(F32)
    for r in range(dil):
        for c in range(width // 128):
            scr[c, pl.ds(r, rows, stride=dil), :] = ref[r, :, c * 128:(c + 1) * 128].astype(F32)
    return jnp.concatenate([scr[c] for c in range(width // 128)], axis=1)


def _combine(os, lses):
    tm = os[0].shape[0]
    lane = lax.broadcasted_iota(jnp.int32, (tm, GROUP_WIDTH), 1)

    def spread(l2):
        out = jnp.zeros((tm, GROUP_WIDTH), F32)
        for h in range(HEADS_PER_GROUP):
            out = jnp.where(lane // HEAD_DIM == h, l2[:, 32 * h:32 * h + 1], out)
        return out

    ls = [spread(l) for l in lses]
    top = jnp.maximum(jnp.maximum(ls[0], ls[1]), ls[2])
    es = [jnp.exp(l - top) for l in ls]
    tot = es[0] + es[1] + es[2]
    return [(o * (e / tot)).astype(BF16) for o, e in zip(os, es)]


def _outproj_kernel(combine, alpha, *refs):
    if combine:
        (o0, o1, o2, l0, l1, l2, hm_ref, x_ref, wo_ref, g_ref, b_ref, wq_ref, x1_ref, qc_ref, scr) = refs
        att = _combine([_interleave(o, scr) for o in (o0, o1, o2)], [_interleave(l, scr) for l in (l0, l1, l2)])
    else:
        (a_ref, hm_ref, x_ref, wo_ref, g_ref, b_ref, wq_ref, x1_ref, qc_ref) = refs
        att = [a_ref[:, g * GROUP_WIDTH:(g + 1) * GROUP_WIDTH].astype(BF16) for g in range(len(ATT_GROUPS))]
    mix = _dot(hm_ref[...].astype(BF16), wo_ref[ATT_WIDTH:, :])
    for g, a in enumerate(att):
        mix = mix + _dot(a, wo_ref[g * GROUP_WIDTH:(g + 1) * GROUP_WIDTH, :])
    x1 = _layer_norm(alpha * x_ref[...] + mix, g_ref[...], b_ref[...])
    x1_ref[...] = x1
    qc_ref[...] = (_dot(x1.astype(BF16), wq_ref[...]) * ((x1.shape[1] // MEM_HEADS) ** -0.5)).astype(qc_ref.dtype)


def _outproj(att_parts, hm, x, w_out, ln_g, ln_b, w_cq, alpha, *, tm):
    n, d = x.shape
    combine = len(att_parts) > 1
    row = lambda a: pl.BlockSpec((tm, a.shape[1]), lambda i: (i, 0))
    full = lambda a: pl.BlockSpec(a.shape, lambda i: (0, 0))

    def split(a):
        _, dil, ts, w = a.shape
        tiles_per_seq = ts * dil // tm
        return pl.BlockSpec((None, dil, tm // dil, w), lambda i: (i // tiles_per_seq, 0, i % tiles_per_seq, 0))

    consts = [w_out, ln_g.reshape(1, d), ln_b.reshape(1, d), w_cq]
    return pl.pallas_call(
        functools.partial(_outproj_kernel, combine, alpha),
        grid=(n // tm,),
        in_specs=[split(a) if combine else row(a) for a in att_parts] + [row(hm), row(x)] + [full(c) for c in consts],
        out_specs=[pl.BlockSpec((tm, d), lambda i: (i, 0))] * 2,
        out_shape=[jax.ShapeDtypeStruct((n, d), F32), jax.ShapeDtypeStruct((n, d), BF16)],
        scratch_shapes=[pltpu.VMEM((GROUP_WIDTH // 128, tm, 128), F32)] if combine else [],
        compiler_params=_cparams("arbitrary"),
        name="outproj_ln1",
    )(*att_parts, hm, x, *consts)


def _memkv_kernel(mem_ref, wk_ref, wv_ref, kv_ref):
    d = mem_ref.shape[2]
    mem = mem_ref[0].astype(BF16)
    kv_ref[0, :, 0:d] = _dot(mem, wk_ref[...])
    kv_ref[0, :, d:] = _dot(mem, wv_ref[...])


def _memkv(mem, w_ck, w_cv):
    b, m, d = mem.shape
    return pl.pallas_call(
        _memkv_kernel,
        grid=(b,),
        in_specs=[pl.BlockSpec((1, m, d), lambda i: (i, 0, 0)),
                  pl.BlockSpec(w_ck.shape, lambda i: (0, 0)), pl.BlockSpec(w_cv.shape, lambda i: (0, 0))],
        out_specs=pl.BlockSpec((1, m, 2 * d), lambda i: (i, 0, 0)),
        out_shape=jax.ShapeDtypeStruct((b, m, 2 * d), F32),
        compiler_params=_cparams("arbitrary"),
        name="memory_kv",
    )(mem, w_ck, w_cv)


def _xattn_kernel(q_ref, kv_ref, o_ref):
    d = q_ref.shape[2]
    hd = d // MEM_HEADS
    q = q_ref[0]
    for h in range(MEM_HEADS):
        k = kv_ref[0, :, h * hd:(h + 1) * hd].astype(BF16)
        v = kv_ref[0, :, d + h * hd:d + (h + 1) * hd].astype(BF16)
        s = _dot_nt(q[:, h * hd:(h + 1) * hd], k)
        p = jnp.exp(s - jnp.max(s, axis=1, keepdims=True))
        p = p / jnp.sum(p, axis=1, keepdims=True)
        o_ref[0, :, h * hd:(h + 1) * hd] = _dot(p.astype(BF16), v).astype(o_ref.dtype)


def _xattn(qc, kv, *, tq):
    b, t, d = qc.shape
    m = kv.shape[1]
    return pl.pallas_call(
        _xattn_kernel,
        grid=(b, t // tq),
        in_specs=[pl.BlockSpec((1, tq, d), lambda bi, j: (bi, j, 0)),
                  pl.BlockSpec((1, m, 2 * d), lambda bi, j: (bi, 0, 0))],
        out_specs=pl.BlockSpec((1, tq, d), lambda bi, j: (bi, j, 0)),
        out_shape=jax.ShapeDtypeStruct((b, t, d), BF16),
        compiler_params=_cparams("arbitrary", "arbitrary"),
        name="cross_attention",
    )(qc, kv)


def _xout_kernel(alpha, o_ref, x_ref, w_ref, g_ref, b_ref, y_ref):
    y_ref[...] = _layer_norm(alpha * x_ref[...] + _dot(o_ref[...], w_ref[...]), g_ref[...], b_ref[...])


def _xout(oc, x1, w_co, ln_g, ln_b, alpha, *, tm):
    n, d = x1.shape
    row = pl.BlockSpec((tm, d), lambda i: (i, 0))
    full = lambda a: pl.BlockSpec(a.shape, lambda i: (0, 0))
    consts = [w_co, ln_g.reshape(1, d), ln_b.reshape(1, d)]
    return pl.pallas_call(
        functools.partial(_xout_kernel, alpha),
        grid=(n // tm,),
        in_specs=[row, row] + [full(c) for c in consts],
        out_specs=row,
        out_shape=jax.ShapeDtypeStruct((n, d), F32),
        compiler_params=_cparams("arbitrary"),
        name="cross_out_ln2",
    )(oc, x1, *consts)


def _first_index_of_max(vals, idx, big):
    mx = jnp.max(vals, axis=0, keepdims=True)
    return mx, jnp.min(jnp.where(vals == mx, idx, big), axis=0, keepdims=True)


def _route(xb, wrt_ref, br_ref):
    tm = xb.shape[0]
    n_e = wrt_ref.shape[0]
    per_group = n_e // N_EXPERT_GROUPS
    logits = _dot_nt(wrt_ref[...], xb)
    scores = _sigmoid(logits)
    biased = scores + br_ref[...]
    e_idx = lax.broadcasted_iota(jnp.int32, (n_e, tm), 0).astype(F32)
    g_scores = []
    for g in range(N_EXPERT_GROUPS):
        sub = biased[g * per_group:(g + 1) * per_group, :]
        sidx = lax.broadcasted_iota(jnp.int32, (per_group, tm), 0).astype(F32)
        m1, a1 = _first_index_of_max(sub, sidx, per_group)
        m2 = jnp.max(jnp.where(sidx == a1, -jnp.inf, sub), axis=0, keepdims=True)
        g_scores.append(m1 + m2)
    gs = jnp.concatenate(g_scores, axis=0)
    g_idx = lax.broadcasted_iota(jnp.int32, (N_EXPERT_GROUPS, tm), 0).astype(F32)
    g_sel = jnp.zeros((N_EXPERT_GROUPS, tm), F32)
    work = gs
    for _ in range(TOPK_GROUPS):
        _, a = _first_index_of_max(work, g_idx, N_EXPERT_GROUPS)
        hit = g_idx == a
        g_sel = jnp.where(hit, 1.0, g_sel)
        work = jnp.where(hit, -jnp.inf, work)
    e_mask = jnp.concatenate(
        [jnp.broadcast_to(g_sel[g:g + 1, :], (per_group, tm)) for g in range(N_EXPERT_GROUPS)], axis=0)
    work = jnp.where(e_mask > 0.5, biased, -jnp.inf)
    sel = jnp.zeros((n_e, tm), F32)
    picks = []
    for _ in range(TOP_K):
        cand = jnp.where(sel > 0.5, -jnp.inf, work)
        mx = jnp.max(cand, axis=0, keepdims=True)
        a = jnp.min(jnp.where((cand == mx) & (sel < 0.5), e_idx, float(n_e)), axis=0, keepdims=True)
        pick = jnp.where(e_idx == a, 1.0, 0.0)
        picks.append(pick)
        sel = sel + pick
    w_sel = sel * scores
    gates_t = w_sel / jnp.sum(w_sel, axis=0, keepdims=True) * ROUTED_SCALE
    return gates_t, sel, picks


ROW_CHUNK = 16
TILE_CHUNKS = 16
TILE_ROWS = ROW_CHUNK * TILE_CHUNKS
TOKEN_BLOCK = 256
GATHER_AHEAD = 3
TILE_SLOTS = GATHER_AHEAD + 1
SPECULATIVE_PIECES = 4


def _slab_rows(tb, n_e):
    return -(-(TOP_K * tb + n_e * (ROW_CHUNK - 1)) // 128) * 128


def _block_of(refs, first_blocks):
    i = pl.program_id(0)
    x = refs[0][...]
    for ref, first in zip(refs[1:], first_blocks[1:]):
        x = jnp.where(i >= first, ref[...], x)
    return x


def _multi_specs(arrays, tb):
    firsts, specs, start = [], [], 0
    for a in arrays:
        nb = a.shape[0] // tb
        firsts.append(start)
        specs.append(pl.BlockSpec((tb, a.shape[1]),
                                  lambda i, start=start, nb=nb: (jnp.clip(i - start, 0, nb - 1), 0)))
        start += nb
    return firsts, specs, start


def _dispatch_kernel(first_blocks, *refs):
    n_x = len(first_blocks)
    x_refs, (wrt_ref, br_ref, xs_ref, pwt_ref, meta_ref) = refs[:n_x], refs[n_x:]
    tb = x_refs[0].shape[0]
    n_e = wrt_ref.shape[0]
    slab = xs_ref.shape[1]
    xb = _block_of(x_refs, first_blocks).astype(BF16)
    gates_t, sel, picks = _route(xb, wrt_ref, br_ref)
    cnt = jnp.sum(sel, axis=1, keepdims=True)
    padded = jnp.floor((cnt + (ROW_CHUNK - 1)) * (1.0 / ROW_CHUNK)) * ROW_CHUNK
    padded_b = jnp.broadcast_to(padded, (n_e, 128))
    er = lax.broadcasted_iota(jnp.int32, (n_e, n_e), 0)
    ec = lax.broadcasted_iota(jnp.int32, (n_e, n_e), 1)
    off_b = _dot((ec < er).astype(BF16), padded_b.astype(BF16))
    tr = lax.broadcasted_iota(jnp.int32, (tb, tb), 0)
    tc = lax.broadcasted_iota(jnp.int32, (tb, tb), 1)
    rank = _dot(sel.astype(BF16), (tr < tc).astype(BF16))
    dest = off_b[:, 0:1] + rank
    dest_k = [jnp.sum(p * dest, axis=0, keepdims=True) for p in picks]
    gate_k = [jnp.sum(p * gates_t, axis=0, keepdims=True) for p in picks]
    meta_ref[0, 0] = padded_b
    meta_ref[0, 1] = off_b

    stacked = jnp.concatenate(dest_k + [jnp.zeros((8 - TOP_K, tb), F32)] + gate_k
                              + [jnp.zeros((128 - 8 - TOP_K, tb), F32)], axis=0)
    cols = jnp.transpose(stacked)
    step = 512
    for c0 in range(0, slab, step):
        lanes = lax.broadcasted_iota(jnp.int32, (tb, step), 1).astype(F32) + float(c0)
        w = jnp.zeros((tb, step), F32)
        for k in range(TOP_K):
            w = jnp.where(lanes == cols[:, k:k + 1], cols[:, 8 + k:9 + k], w)
        pwt_ref[0, :, c0:c0 + step] = w.astype(pwt_ref.dtype)
        onehot_t = jnp.where(w != 0.0, 1.0, 0.0).astype(BF16)
        xs_ref[0, c0:c0 + step, :] = _dot_tn(onehot_t, xb).astype(xs_ref.dtype)


def _dispatch(xs_in, w_router_t, b_router):
    d = xs_in[0].shape[1]
    n_e = w_router_t.shape[0]
    tb = TOKEN_BLOCK
    firsts, x_specs, nblk = _multi_specs(xs_in, tb)
    slab = _slab_rows(tb, n_e)
    assert slab % 512 == 0
    return pl.pallas_call(
        functools.partial(_dispatch_kernel, tuple(firsts)),
        grid=(nblk,),
        in_specs=x_specs + [pl.BlockSpec(w_router_t.shape, lambda i: (0, 0)),
                            pl.BlockSpec((n_e, 1), lambda i: (0, 0))],
        out_specs=[pl.BlockSpec((1, slab, d), lambda i: (i, 0, 0)),
                   pl.BlockSpec((1, tb, slab), lambda i: (i, 0, 0)),
                   pl.BlockSpec((1, 2, n_e, 128), lambda i: (i, 0, 0, 0))],
        out_shape=[jax.ShapeDtypeStruct((nblk, slab, d), BF16),
                   jax.ShapeDtypeStruct((nblk, tb, slab), BF16),
                   jax.ShapeDtypeStruct((nblk, 2, n_e, 128), F32)],
        compiler_params=_cparams("arbitrary"),
        name="moe_dispatch",
    )(*xs_in, w_router_t, b_router.reshape(n_e, 1))


def _schedule_kernel(n_e, nblk, t_max, zero_row, slot_ref, pieces_ref, first_ref, total_ref, tile0_ref,
                     te_ref, nv_ref, row_ref, nt_ref):
    def per_expert(e, carry):
        def per_block(blk, c):
            idx = e * nblk + blk
            p = slot_ref[idx]
            r0 = first_ref[idx]
            for j in range(SPECULATIVE_PIECES):
                row_ref[p + j] = r0 + j * ROW_CHUNK

            def per_piece(j, cc):
                row_ref[p + j] = r0 + j * ROW_CHUNK
                return cc
            lax.fori_loop(SPECULATIVE_PIECES, pieces_ref[idx], per_piece, 0)
            return c
        lax.fori_loop(0, nblk, per_block, 0)

        cnt = total_ref[e]
        tile = tile0_ref[e]
        n_t = (cnt + (TILE_CHUNKS - 1)) // TILE_CHUNKS
        start = tile * TILE_CHUNKS

        def pad(p, c):
            row_ref[p] = zero_row
            return c
        lax.fori_loop(start + cnt, start + n_t * TILE_CHUNKS, pad, 0)

        def per_tile(i, c):
            te_ref[tile + i] = e
            nv_ref[tile + i] = jnp.minimum(TILE_CHUNKS, cnt - i * TILE_CHUNKS)
            return c
        lax.fori_loop(0, n_t, per_tile, 0)
        return carry
    lax.fori_loop(0, n_e, per_expert, 0)

    last = n_e - 1
    tile = tile0_ref[last] + (total_ref[last] + (TILE_CHUNKS - 1)) // TILE_CHUNKS
    nt_ref[0] = tile
    last_e = te_ref[jnp.maximum(tile - 1, 0)]

    def idle_tile(t, c):
        te_ref[t] = last_e
        nv_ref[t] = 0
        return c
    lax.fori_loop(tile, t_max, idle_tile, 0)

    def idle_piece(p, c):
        row_ref[p] = zero_row
        return c
    lax.fori_loop(tile * TILE_CHUNKS, t_max * TILE_CHUNKS, idle_piece, 0)


def _tile_schedule(meta, slab):
    nblk, _, n_e, _ = meta.shape
    pieces = (meta[:, 0, :, 0].astype(jnp.int32) // ROW_CHUNK).T
    first_row = (meta[:, 1, :, 0].astype(jnp.int32) + (jnp.arange(nblk, dtype=jnp.int32) * slab)[:, None]).T
    before = jnp.cumsum(pieces, axis=1) - pieces
    total = jnp.sum(pieces, axis=1)
    tiles = (total + (TILE_CHUNKS - 1)) // TILE_CHUNKS
    tile0 = jnp.cumsum(tiles) - tiles
    slot = tile0[:, None] * TILE_CHUNKS + before
    t_max = -(-(TOP_K * nblk * TOKEN_BLOCK + nblk * n_e * (ROW_CHUNK - 1)) // TILE_ROWS) + n_e + TILE_SLOTS
    assert slab - ROW_CHUNK >= TOP_K * TOKEN_BLOCK + n_e * (ROW_CHUNK - 1)
    smem = pl.BlockSpec(memory_space=pltpu.SMEM)
    i32 = lambda n: jax.ShapeDtypeStruct((n,), jnp.int32)
    return pl.pallas_call(
        functools.partial(_schedule_kernel, n_e, nblk, t_max, slab - ROW_CHUNK),
        in_specs=[smem] * 5,
        out_specs=[smem, smem, smem, smem],
        out_shape=[i32(t_max), i32(t_max), i32(t_max * TILE_CHUNKS), i32(1)],
        name="moe_schedule",
    )(slot.reshape(-1), pieces.reshape(-1), first_row.reshape(-1), total, tile0)


def _grouped_kernel(te_ref, nv_ref, row_ref, nt_ref, xs_hbm, wg_ref, wu_ref, wd_ref, ys_hbm,
                    lhs, obuf, wg_b, wu_b, wd_b, sem_in, sem_out):
    t = pl.program_id(0)
    n_tiles = nt_ref[0]

    def piece(ref, tile, s):
        row = pl.multiple_of(row_ref[tile * TILE_CHUNKS + s], ROW_CHUNK)
        return ref.at[pl.ds(row, ROW_CHUNK)]

    def local(buf, slot, s):
        start = s * ROW_CHUNK if isinstance(s, int) else pl.multiple_of(s * ROW_CHUNK, ROW_CHUNK)
        return buf.at[slot, pl.ds(start, ROW_CHUNK)]

    def gather(tile, s):
        slot = tile % TILE_SLOTS
        return pltpu.make_async_copy(piece(xs_hbm, tile, s), local(lhs, slot, s), sem_in.at[slot])

    def scatter(tile, s):
        slot = tile % TILE_SLOTS
        return pltpu.make_async_copy(local(obuf, slot, s), piece(ys_hbm, tile, s), sem_out.at[slot])

    def gathered_tile(slot):
        return pltpu.make_async_copy(xs_hbm.at[pl.ds(0, TILE_ROWS)], lhs.at[slot], sem_in.at[slot])

    def scattered_tile(slot):
        return pltpu.make_async_copy(obuf.at[slot], ys_hbm.at[pl.ds(0, TILE_ROWS)], sem_out.at[slot])

    def for_valid_pieces(tile, fn):
        def body(s, carry):
            fn(tile, s)
            return carry
        lax.fori_loop(0, nv_ref[tile], body, 0)

    @pl.when(t == 0)
    def _():
        for tile in range(GATHER_AHEAD):
            for s in range(TILE_CHUNKS):
                gather(tile, s).start()

    @pl.when((t >= TILE_SLOTS) & (t - TILE_SLOTS < n_tiles))
    def _():
        full = nv_ref[t - TILE_SLOTS] == TILE_CHUNKS

        @pl.when(full)
        def _():
            scattered_tile(t % TILE_SLOTS).wait()

        @pl.when(jnp.logical_not(full))
        def _():
            for_valid_pieces(t - TILE_SLOTS, lambda tile, s: scatter(tile, s).wait())

    @pl.when((t < n_tiles) & ((t == 0) | (te_ref[t] != te_ref[jnp.maximum(t - 1, 0)])))
    def _():
        wg_b[...] = wg_ref[0].astype(BF16)
        wu_b[...] = wu_ref[0].astype(BF16)
        wd_b[...] = wd_ref[0].astype(BF16)

    @pl.when(t < n_tiles)
    def _():
        slot = t % TILE_SLOTS
        gathered_tile(slot).wait()
        for s in range(TILE_CHUNKS):
            gather(t + GATHER_AHEAD, s).start()
        x = lhs[slot]
        h = _silu(_dot(x, wg_b[...])) * _dot(x, wu_b[...])
        obuf[slot] = _dot(h.astype(BF16), wd_b[...]).astype(obuf.dtype)

    @pl.when((t >= n_tiles) & (t < n_tiles + GATHER_AHEAD))
    def _():
        gathered_tile(t % TILE_SLOTS).wait()

    @pl.when(t < n_tiles)
    def _():
        full = nv_ref[t] == TILE_CHUNKS

        @pl.when(full)
        def _():
            for s in range(TILE_CHUNKS):
                scatter(t, s).start()

        @pl.when(jnp.logical_not(full))
        def _():
            for_valid_pieces(t, lambda tile, s: scatter(tile, s).start())


def _grouped_experts(xs, tile_e, n_valid, rows_tbl, n_tiles, w_gate, w_up, w_down):
    rows, d = xs.shape
    n_e, _, ff = w_gate.shape
    t_max = tile_e.shape[0]
    grid_spec = pltpu.PrefetchScalarGridSpec(
        num_scalar_prefetch=4,
        grid=(t_max,),
        in_specs=[pl.BlockSpec(memory_space=pl.ANY),
                  pl.BlockSpec((1, d, ff), lambda t, te, nv, sr, nt: (te[t], 0, 0)),
                  pl.BlockSpec((1, d, ff), lambda t, te, nv, sr, nt: (te[t], 0, 0)),
                  pl.BlockSpec((1, ff, d), lambda t, te, nv, sr, nt: (te[t], 0, 0))],
        out_specs=pl.BlockSpec(memory_space=pl.ANY),
        scratch_shapes=[pltpu.VMEM((TILE_SLOTS, TILE_ROWS, d), BF16), pltpu.VMEM((TILE_SLOTS, TILE_ROWS, d), BF16),
                        pltpu.VMEM((d, ff), BF16), pltpu.VMEM((d, ff), BF16), pltpu.VMEM((ff, d), BF16),
                        pltpu.SemaphoreType.DMA((TILE_SLOTS,)), pltpu.SemaphoreType.DMA((TILE_SLOTS,))],
    )
    return pl.pallas_call(
        _grouped_kernel,
        grid_spec=grid_spec,
        out_shape=jax.ShapeDtypeStruct((rows, d), BF16),
        input_output_aliases={4: 0},
        compiler_params=_cparams("arbitrary"),
        name="moe_grouped_experts",
    )(tile_e, n_valid, rows_tbl, n_tiles, xs, w_gate, w_up, w_down)


def _moe_out_kernel(alpha, first_blocks, pwt_ref, ys_ref, *refs):
    n_x = len(first_blocks)
    x_refs, (sg_ref, su_ref, sd_ref, g_ref, b_ref), y_refs = refs[:n_x], refs[n_x:n_x + 5], refs[n_x + 5:]
    i = pl.program_id(0)
    routed = _dot(pwt_ref[0], ys_ref[0])
    x = _block_of(x_refs, first_blocks)
    xb = x.astype(BF16)
    hs = _silu(_dot(xb, sg_ref[...])) * _dot(xb, su_ref[...])
    shared = _dot(hs.astype(BF16), sd_ref[...])
    y = _layer_norm(alpha * x + (routed + shared), g_ref[...], b_ref[...])
    bounds = list(first_blocks[1:]) + [pl.num_programs(0)]
    for y_ref, lo, hi in zip(y_refs, first_blocks, bounds):
        @pl.when((i >= lo) & (i < hi))
        def _(y_ref=y_ref):
            y_ref[...] = y


def _moe_out(pwt, ys, xs_in, ws_gate, ws_up, ws_down, ln_g, ln_b, alpha):
    d = xs_in[0].shape[1]
    nblk, tb, slab = pwt.shape
    firsts, x_specs, _ = _multi_specs(xs_in, tb)
    consts = [ws_gate, ws_up, ws_down, ln_g.reshape(1, d), ln_b.reshape(1, d)]
    return pl.pallas_call(
        functools.partial(_moe_out_kernel, alpha, tuple(firsts)),
        grid=(nblk,),
        in_specs=[pl.BlockSpec((1, tb, slab), lambda i: (i, 0, 0)),
                  pl.BlockSpec((1, slab, d), lambda i: (i, 0, 0))] + x_specs
                 + [pl.BlockSpec(c.shape, lambda i: (0, 0)) for c in consts],
        out_specs=x_specs,
        out_shape=[jax.ShapeDtypeStruct(x.shape, F32) for x in xs_in],
        compiler_params=_cparams("arbitrary"),
        name="moe_combine",
    )(pwt, ys.reshape(nblk, slab, d), *xs_in, *consts)


def _moe(xs_in, w_router_t, b_router, w_gate, w_up, w_down, ws_gate, ws_up, ws_down, ln_g, ln_b, alpha):
    xs, pwt, meta = _dispatch(xs_in, w_router_t, b_router)
    nblk, slab, d = xs.shape
    tile_e, n_valid, rows_tbl, n_tiles = _tile_schedule(meta, slab)
    ys = _grouped_experts(xs.reshape(nblk * slab, d), tile_e, n_valid, rows_tbl, n_tiles, w_gate, w_up, w_down)
    return _moe_out(pwt, ys, xs_in, ws_gate, ws_up, ws_down, ln_g, ln_b, alpha)


def _rope_tables(pos):
    half = HEAD_DIM // 2
    inv_freq = ROPE_THETA ** (-jnp.arange(half, dtype=F32) / half)
    ang = pos.astype(F32)[:, None] * inv_freq[None, :]
    cos, sin = jnp.cos(ang), jnp.sin(ang)
    return jnp.tile(jnp.concatenate([cos, cos], axis=1), (1, 2)), jnp.tile(jnp.concatenate([-sin, sin], axis=1), (1, 2))


def _block_diag(w):
    h, d, _ = w.shape
    out = jnp.zeros((h * d, h * d), w.dtype)
    for i in range(h):
        out = out.at[i * d:(i + 1) * d, i * d:(i + 1) * d].set(w[i])
    return out


def _pick(n, pref):
    return pref if n % pref == 0 else n


def kernel(x_prompt, x_sample, mem_prompt, cache_win128, cache_win512, cache_win2048, cache_mem_kv, state_conv, state_C, state_n, state_m, w_in, conv_w, conv_b, wq_m, wk_m, w_if, b_if, mh_norm_w, skip_m, w_out, ln1_g, ln1_b, w_cq, w_ck, w_cv, w_co, ln2_g, ln2_b, w_router, b_router, w_gate, w_up, w_down, ws_gate, ws_up, ws_down, ln3_g, ln3_b):
    depth = w_in.shape[0]
    assert depth == 1
    alpha = float((2 * depth) ** 0.25)
    bp, seq, d = x_prompt.shape
    bs, dec, _ = x_sample.shape
    assert seq % ATT_GROUPS[-1][0] == 0
    for c, (window, _) in zip((cache_win128, cache_win512, cache_win2048), ATT_GROUPS):
        assert c.shape[2] == window

    l = 0
    bf = lambda a: a.astype(BF16)
    w_in_b, w_out_b = bf(w_in[l]), bf(w_out[l])
    w_cq_b, w_ck_b, w_cv_b, w_co_b = bf(w_cq[l]), bf(w_ck[l]), bf(w_cv[l]), bf(w_co[l])
    wq_bd, wk_bd = bf(_block_diag(wq_m[l])), bf(_block_diag(wk_m[l]))
    w_router_t = bf(w_router[l].T)
    ws_gate_b, ws_up_b, ws_down_b = bf(ws_gate[l]), bf(ws_up[l]), bf(ws_down[l])

    def tail_of_layer(x1, qc, kv, batch, t, tm, tq):
        n = batch * t
        oc = _xattn(qc.reshape(batch, t, d), kv, tq=tq).reshape(n, d)
        return _xout(oc, x1, w_co_b, ln2_g[l], ln2_b[l], alpha, tm=tm)

    def mlstm(m3, batch, t, states, chunk, bb):
        return _mlstm(m3.reshape(batch, t, -1), *states, conv_w[l], conv_b[l], wq_bd, wk_bd, w_if[l], b_if[l],
                      mh_norm_w[l], skip_m[l], chunk=chunk, bb=bb)

    np_ = bp * seq
    xp = x_prompt.reshape(np_, d)
    cos_p, sin_p = _rope_tables(jnp.arange(seq))
    *qkv, m3, t128, t512, t2048 = _inproj_prompt(xp, w_in_b, cos_p, sin_p, seq=seq, tm=512)
    parts, lses = [], []
    for g in range(len(ATT_GROUPS)):
        o, lse = _win_attn(*qkv[3 * g:3 * g + 3])
        parts.append(o)
        lses.append(lse)
    zeros_p = [jnp.zeros((bp, CONV_WIDTH - 1, MLSTM_WIDTH), F32), jnp.zeros((bp, MLSTM_WIDTH, HEAD_DIM), F32),
               jnp.zeros((bp, MLSTM_HEADS, HEAD_DIM), F32), jnp.zeros((bp, 1, MLSTM_HEADS), F32)]
    hm_p, p_conv, p_c, p_n, p_m = mlstm(m3, bp, seq, zeros_p, 256, bp)
    x1, qc = _outproj(parts + lses, hm_p.reshape(np_, -1), xp, w_out_b, ln1_g[l], ln1_b[l], w_cq_b, alpha, tm=256)
    kv_p = _memkv(mem_prompt, w_ck_b, w_cv_b)
    x2_p = tail_of_layer(x1, qc, kv_p, bp, seq, 256, 512)

    ns = bs * dec
    xs = x_sample.reshape(ns, d)
    cos_s, sin_s = _rope_tables(jnp.tile(PAST_LEN + jnp.arange(dec), bs))
    qs, ks, vs, m3s = _inproj(xs, w_in_b, cos_s, sin_s, tm=ns)
    caches = [c[l].reshape(bs, c.shape[2], 2 * GROUP_WIDTH) for c in (cache_win128, cache_win512, cache_win2048)]
    att_s, s128, s512, s2048 = _dec_attn(qs.reshape(bs, dec, -1), ks.reshape(bs, dec, -1), vs.reshape(bs, dec, -1),
                                         caches)
    states_s = [state_conv[l], state_C[l].reshape(bs, MLSTM_WIDTH, HEAD_DIM), state_n[l],
                state_m[l].reshape(bs, 1, MLSTM_HEADS)]
    hm_s, s_conv, s_c, s_n, s_m = mlstm(m3s, bs, dec, states_s, dec, 4 if bs % 4 == 0 else 1)
    x1s, qcs = _outproj([att_s.reshape(ns, -1)], hm_s.reshape(ns, -1), xs, w_out_b, ln1_g[l], ln1_b[l], w_cq_b,
                        alpha, tm=ns)
    kv_s = cache_mem_kv[l].reshape(bs, cache_mem_kv.shape[2], 2 * d)
    x2_s = tail_of_layer(x1s, qcs, kv_s, bs, dec, ns, dec)

    y_p, y_s = _moe([x2_p, x2_s], w_router_t, b_router[l], w_gate[l], w_up[l], w_down[l],
                    ws_gate_b, ws_up_b, ws_down_b, ln3_g[l], ln3_b[l], alpha)

    win_shape = lambda a, b_: a.reshape(1, b_, a.shape[1], 2, HEADS_PER_GROUP, HEAD_DIM)
    return (y_p.reshape(bp, seq, d), y_s.reshape(bs, dec, d),
            win_shape(t128, bp), win_shape(t512, bp), win_shape(t2048, bp),
            kv_p.reshape(1, bp, mem_prompt.shape[1], 2, MEM_HEADS, d // MEM_HEADS),
            p_conv[None], p_c.reshape(1, bp, MLSTM_HEADS, HEAD_DIM, HEAD_DIM), p_n[None],
            p_m.reshape(1, bp, MLSTM_HEADS),
            win_shape(s128, bs), win_shape(s512, bs), win_shape(s2048, bs),
            s_conv[None], s_c.reshape(1, bs, MLSTM_HEADS, HEAD_DIM, HEAD_DIM), s_n[None],
            s_m.reshape(1, bs, MLSTM_HEADS))
```

```python
import functools
import math

import jax
import jax.numpy as jnp
from jax import lax
from jax.experimental import pallas as pl
from jax.experimental.pallas import tpu as pltpu

F32 = jnp.float32
BF16 = jnp.bfloat16

HEAD_DIM = 64
ATT_GROUPS = ((128, 1), (512, 4), (2048, 16))
HEADS_PER_GROUP = 4
GROUP_WIDTH = HEADS_PER_GROUP * HEAD_DIM
ATT_WIDTH = GROUP_WIDTH * len(ATT_GROUPS)
N_KEYS = 129
ROPE_THETA = 10000.0
PAST_LEN = 8192
MLSTM_HEADS = 4
MLSTM_WIDTH = MLSTM_HEADS * HEAD_DIM
CONV_WIDTH = 4
MEM_HEADS = 4
N_EXPERT_GROUPS = 8
TOPK_GROUPS = 4
TOP_K = 6
ROUTED_SCALE = 2.5
LN_EPS = 1e-5
NEG = -1e30
VMEM_LIMIT = 56 * 1024 * 1024


def _cparams(*sem):
    return pltpu.CompilerParams(dimension_semantics=sem, vmem_limit_bytes=VMEM_LIMIT)


def _dot(a, b):
    return jnp.dot(a, b, preferred_element_type=F32)


def _dot_nt(a, b):
    return lax.dot_general(a, b, (((1,), (1,)), ((), ())), preferred_element_type=F32)


def _dot_tn(a, b):
    return lax.dot_general(a, b, (((0,), (0,)), ((), ())), preferred_element_type=F32)


def _split_bf16(a):
    hi = a.astype(BF16)
    return hi, (a - hi.astype(F32)).astype(BF16)


def _dot_split(a, b, dims, a_exact=False, b_exact=False):
    dn = (dims, ((), ()))
    dot = lambda x, y: lax.dot_general(x, y, dn, preferred_element_type=F32)
    a_hi, a_lo = _split_bf16(a)
    b_hi, b_lo = _split_bf16(b)
    out = dot(a_hi, b_hi)
    if not b_exact:
        out = out + dot(a_hi, b_lo)
    if not a_exact:
        out = out + dot(a_lo, b_hi)
    return out


def _layer_norm(x, g, b):
    mu = jnp.mean(x, axis=-1, keepdims=True)
    xc = x - mu
    var = jnp.mean(xc * xc, axis=-1, keepdims=True)
    return xc * lax.rsqrt(var + LN_EPS) * g + b


def _sigmoid(x):
    return 1.0 / (1.0 + jnp.exp(-x))


def _silu(x):
    return x * _sigmoid(x)


def _log_sigmoid(x):
    return jnp.minimum(x, 0.0) - jnp.log(1.0 + jnp.exp(-jnp.abs(x)))


def _project_qkv(x_ref, w_ref, cos_ref, sin_ref):
    tm = x_ref.shape[0]
    x = x_ref[...].astype(BF16)
    cos = jnp.concatenate([cos_ref[...]] * (ATT_WIDTH // 128), axis=1)
    sin = jnp.concatenate([sin_ref[...]] * (ATT_WIDTH // 128), axis=1)
    lane = lax.broadcasted_iota(jnp.int32, (tm, ATT_WIDTH), 1)
    first_half = (lane % HEAD_DIM) < (HEAD_DIM // 2)

    def rope(t):
        fwd = pltpu.roll(t, ATT_WIDTH - HEAD_DIM // 2, 1)
        bwd = pltpu.roll(t, HEAD_DIM // 2, 1)
        return t * cos + jnp.where(first_half, fwd, bwd) * sin

    q = rope(_dot(x, w_ref[:, 0:ATT_WIDTH])) * (HEAD_DIM ** -0.5)
    k = rope(_dot(x, w_ref[:, ATT_WIDTH:2 * ATT_WIDTH]))
    v = _dot(x, w_ref[:, 2 * ATT_WIDTH:3 * ATT_WIDTH])
    return q, k, v, _dot(x, w_ref[:, 3 * ATT_WIDTH:])


def _inproj_kernel(x_ref, w_ref, cos_ref, sin_ref, q_ref, k_ref, v_ref, m_ref):
    q, k, v, m = _project_qkv(x_ref, w_ref, cos_ref, sin_ref)
    q_ref[...] = q.astype(q_ref.dtype)
    k_ref[...] = k.astype(k_ref.dtype)
    v_ref[...] = v.astype(v_ref.dtype)
    m_ref[...] = m


def _inproj_prompt_kernel(x_ref, w_ref, cos_ref, sin_ref, *refs):
    n_g = len(ATT_GROUPS)
    qkv_refs, m_ref, tails, scr = refs[:3 * n_g], refs[3 * n_g], refs[3 * n_g + 1:4 * n_g + 1], refs[4 * n_g + 1]
    tm = x_ref.shape[0]
    q, k, v, m = _project_qkv(x_ref, w_ref, cos_ref, sin_ref)
    m_ref[...] = m
    for g, (_, dil) in enumerate(ATT_GROUPS):
        cols = slice(g * GROUP_WIDTH, (g + 1) * GROUP_WIDTH)
        for a, val in enumerate((q, k, v)):
            o_ref = qkv_refs[3 * g + a]
            if dil == 1:
                o_ref[0, 0] = val[:, cols].astype(o_ref.dtype)
                continue
            for c in range(GROUP_WIDTH // 128):
                scr[c] = val[:, g * GROUP_WIDTH + c * 128:g * GROUP_WIDTH + (c + 1) * 128]
            for r in range(dil):
                for c in range(GROUP_WIDTH // 128):
                    o_ref[0, r, :, c * 128:(c + 1) * 128] = scr[c, pl.ds(r, tm // dil, stride=dil), :].astype(o_ref.dtype)
        t_ref = tails[g]
        rows = t_ref.shape[1]
        t_ref[0, :, 0:GROUP_WIDTH] = k[tm - rows:, cols]
        t_ref[0, :, GROUP_WIDTH:] = v[tm - rows:, cols]


def _inproj_prompt(x, w_in, cos, sin, *, seq, tm):
    n, d = x.shape
    tiles_per_seq = seq // tm
    batch = n // seq
    out_shape, out_specs = [], []
    for _, dil in ATT_GROUPS:
        for _ in range(3):
            out_shape.append(jax.ShapeDtypeStruct((batch, dil, seq // dil, GROUP_WIDTH), BF16))
            out_specs.append(pl.BlockSpec((1, dil, tm // dil, GROUP_WIDTH),
                                          lambda i: (i // tiles_per_seq, 0, i % tiles_per_seq, 0)))
    m_width = w_in.shape[1] - 3 * ATT_WIDTH
    out_shape.append(jax.ShapeDtypeStruct((n, m_width), F32))
    out_specs.append(pl.BlockSpec((tm, m_width), lambda i: (i, 0)))
    for window, _ in ATT_GROUPS:
        rows = min(window, tm)
        first = tiles_per_seq - window // rows if window > rows else tiles_per_seq - 1
        out_shape.append(jax.ShapeDtypeStruct((batch, min(window, seq), 2 * GROUP_WIDTH), F32))
        out_specs.append(pl.BlockSpec(
            (1, rows, 2 * GROUP_WIDTH),
            lambda i, first=first: (i // tiles_per_seq, jnp.maximum(i % tiles_per_seq - first, 0), 0)))
    return pl.pallas_call(
        _inproj_prompt_kernel,
        grid=(n // tm,),
        in_specs=[pl.BlockSpec((tm, d), lambda i: (i, 0)),
                  pl.BlockSpec(w_in.shape, lambda i: (0, 0)),
                  pl.BlockSpec((tm, 128), lambda i: (i % tiles_per_seq, 0)),
                  pl.BlockSpec((tm, 128), lambda i: (i % tiles_per_seq, 0))],
        out_specs=out_specs,
        out_shape=out_shape,
        scratch_shapes=[pltpu.VMEM((GROUP_WIDTH // 128, tm, 128), F32)],
        compiler_params=_cparams("arbitrary"),
        name="inproj_rope_prompt",
    )(x, w_in, cos, sin)


def _inproj(x, w_in, cos, sin, *, tm):
    n, d = x.shape
    out_shape = [jax.ShapeDtypeStruct((n, ATT_WIDTH), F32)] * 3 + [
        jax.ShapeDtypeStruct((n, w_in.shape[1] - 3 * ATT_WIDTH), F32)]
    row_spec = lambda w: pl.BlockSpec((tm, w), lambda i: (i, 0))
    out_specs = [row_spec(ATT_WIDTH)] * 3 + [row_spec(w_in.shape[1] - 3 * ATT_WIDTH)]
    tiles_per_seq = cos.shape[0] // tm
    return pl.pallas_call(
        _inproj_kernel,
        grid=(n // tm,),
        in_specs=[row_spec(d),
                  pl.BlockSpec(w_in.shape, lambda i: (0, 0)),
                  pl.BlockSpec((tm, 128), lambda i: (i % tiles_per_seq, 0)),
                  pl.BlockSpec((tm, 128), lambda i: (i % tiles_per_seq, 0))],
        out_specs=out_specs,
        out_shape=out_shape,
        compiler_params=_cparams("arbitrary"),
        name="inproj_rope",
    )(x, w_in, cos, sin)


def _win_attn_kernel(q_ref, kp_ref, kc_ref, vp_ref, vc_ref, o_ref, lse_ref):
    j = pl.program_id(2)
    tq = q_ref.shape[1]
    sub = N_KEYS - 1
    assert kp_ref.shape[1] == sub and tq % sub == 0
    k_all = jnp.concatenate([kp_ref[0], kc_ref[0]], axis=0)
    v_all = jnp.concatenate([vp_ref[0], vc_ref[0]], axis=0)
    n_h = HEADS_PER_GROUP
    row = lax.broadcasted_iota(jnp.int32, (n_h * sub, 2 * sub), 0) % sub
    col = lax.broadcasted_iota(jnp.int32, (n_h * sub, 2 * sub), 1)
    band = (col >= row) & (col <= row + sub)
    head_rows = lax.broadcasted_iota(jnp.int32, (n_h * sub, GROUP_WIDTH), 0) // sub
    head_lanes = lax.broadcasted_iota(jnp.int32, (n_h * sub, GROUP_WIDTH), 1) // HEAD_DIM
    own = head_rows == head_lanes
    lane_l = lax.broadcasted_iota(jnp.int32, (sub, 128), 1)
    for i in range(tq // sub):
        q = q_ref[0, i * sub:(i + 1) * sub, :]
        kk = k_all[i * sub:(i + 2) * sub]
        vv = v_all[i * sub:(i + 2) * sub]
        valid = band & ((j > 0) | (col >= sub)) if i == 0 else band
        q4 = jnp.concatenate([q] * n_h, axis=0)
        q4 = jnp.where(own, q4, jnp.zeros_like(q4))
        s = jnp.where(valid, _dot_nt(q4, kk), NEG)
        m = jnp.max(s, axis=1, keepdims=True)
        p = jnp.exp(s - m)
        l = jnp.sum(p, axis=1, keepdims=True)
        o4 = jnp.where(own, _dot(p.astype(BF16), vv) / l, 0.0)
        lse4 = m + jnp.log(l)
        o_acc = o4[0:sub]
        lse_acc = jnp.zeros((sub, 128), F32)
        for h in range(n_h):
            if h:
                o_acc = o_acc + o4[h * sub:(h + 1) * sub]
            lse_acc = jnp.where(lane_l // 32 == h, lse4[h * sub:(h + 1) * sub], lse_acc)
        o_ref[0, i * sub:(i + 1) * sub, :] = o_acc.astype(o_ref.dtype)
        lse_ref[0, i * sub:(i + 1) * sub, :] = lse_acc


def _win_attn(q, k, v, *, tq=512):
    b, dil, ts, _ = q.shape
    tq = min(tq, ts)
    sub = N_KEYS - 1
    cur = pl.BlockSpec((None, 1, tq, GROUP_WIDTH), lambda bi, r, j: (bi, r, j, 0))
    prev = pl.BlockSpec((None, 1, sub, GROUP_WIDTH),
                        lambda bi, r, j: (bi, r, jnp.maximum(j * (tq // sub) - 1, 0), 0))
    return pl.pallas_call(
        _win_attn_kernel,
        grid=(b, dil, ts // tq),
        in_specs=[cur, prev, cur, prev, cur],
        out_specs=[pl.BlockSpec((None, 1, tq, GROUP_WIDTH), lambda bi, r, j: (bi, r, j, 0)),
                   pl.BlockSpec((None, 1, tq, 128), lambda bi, r, j: (bi, r, j, 0))],
        out_shape=[jax.ShapeDtypeStruct((b, dil, ts, GROUP_WIDTH), BF16),
                   jax.ShapeDtypeStruct((b, dil, ts, 128), F32)],
        compiler_params=_cparams("arbitrary", "arbitrary", "arbitrary"),
        name="window_attention",
    )(q, k, k, v, v)


def _dec_attn_kernel(q_ref, k_ref, v_ref, c0_ref, c1_ref, c2_ref, att_ref, o0_ref, o1_ref, o2_ref,
                     e0_ref, e1_ref, e2_ref):
    t_new = q_ref.shape[1]
    q = q_ref[0]
    k_new = k_ref[0]
    v_new = v_ref[0]
    lane = lax.broadcasted_iota(jnp.int32, (8, GROUP_WIDTH), 1)
    sub = lax.broadcasted_iota(jnp.int32, (8, GROUP_WIDTH), 0)
    head_sel = (lane // HEAD_DIM) == sub
    outs = [[None] * t_new for _ in ATT_GROUPS]
    lses = [[None] * t_new for _ in ATT_GROUPS]
    for g, ((window, dil), c_ref, o_ref, e_ref) in enumerate(
            zip(ATT_GROUPS, (c0_ref, c1_ref, c2_ref), (o0_ref, o1_ref, o2_ref), (e0_ref, e1_ref, e2_ref))):
        w = c_ref.shape[1]
        cols = slice(g * GROUP_WIDTH, (g + 1) * GROUP_WIDTH)
        kv_new = jnp.concatenate([k_new[:, cols], v_new[:, cols]], axis=1)
        n_chunk = e_ref.shape[0]
        for c in range(n_chunk):
            e_ref[c, 0:w, :] = c_ref[0, :, c * 128:(c + 1) * 128]
            e_ref[c, w:w + t_new, :] = kv_new[:, c * 128:(c + 1) * 128]
            o_ref[0, :, c * 128:(c + 1) * 128] = e_ref[c, t_new:w + t_new, :]
        for t in range(t_new):
            past = jnp.concatenate(
                [e_ref[c, pl.ds(t, N_KEYS - 1, stride=dil), :] for c in range(n_chunk)], axis=1)
            qm = jnp.where(head_sel, jnp.broadcast_to(q[t:t + 1, cols], (8, GROUP_WIDTH)), 0.0)
            s = _dot_nt(qm.astype(BF16), past[:, 0:GROUP_WIDTH].astype(BF16))
            k_self = kv_new[t:t + 1, 0:GROUP_WIDTH]
            v_self = kv_new[t:t + 1, GROUP_WIDTH:]
            s_self = jnp.sum(qm * k_self, axis=1, keepdims=True)
            m = jnp.maximum(jnp.max(s, axis=1, keepdims=True), s_self)
            p = jnp.exp(s - m)
            p_self = jnp.exp(s_self - m)
            l = jnp.sum(p, axis=1, keepdims=True) + p_self
            o = _dot(p.astype(BF16), past[:, GROUP_WIDTH:].astype(BF16))
            o = (o + p_self * v_self) / l
            outs[g][t] = o
            lses[g][t] = m + jnp.log(l)
    for t in range(t_new):
        top = jnp.maximum(jnp.maximum(lses[0][t], lses[1][t]), lses[2][t])
        es = [jnp.exp(lses[g][t] - top) for g in range(len(ATT_GROUPS))]
        tot = es[0] + es[1] + es[2]
        for g in range(len(ATT_GROUPS)):
            weighted = jnp.where(head_sel, outs[g][t] * (es[g] / tot), 0.0)
            att_ref[0, t:t + 1, g * GROUP_WIDTH:(g + 1) * GROUP_WIDTH] = jnp.sum(weighted, axis=0, keepdims=True)


def _dec_attn(q, k, v, caches):
    b, t_new, _ = q.shape
    tok = pl.BlockSpec((1, t_new, ATT_WIDTH), lambda i: (i, 0, 0))
    cspec = [pl.BlockSpec((1,) + c.shape[1:], lambda i: (i, 0, 0)) for c in caches]
    return pl.pallas_call(
        _dec_attn_kernel,
        grid=(b,),
        in_specs=[tok, tok, tok] + cspec,
        out_specs=[tok] + cspec,
        out_shape=[jax.ShapeDtypeStruct(q.shape, F32)] + [jax.ShapeDtypeStruct(c.shape, F32) for c in caches],
        scratch_shapes=[pltpu.VMEM((c.shape[2] // 128, c.shape[1] + t_new, 128), F32) for c in caches],
        compiler_params=_cparams("arbitrary"),
        name="decode_attention",
    )(q, k, v, *caches)


def _mlstm_kernel(m3_ref, conv0_ref, c0_ref, n0_ref, m0_ref, *rest):
    consts, (hm_ref, convo_ref, co_ref, no_ref, mo_ref, cbuf, c_s, n_s, m_s) = rest[:10], rest[10:]
    for b in range(m3_ref.shape[0]):
        _mlstm_chunk(m3_ref.at[b], conv0_ref.at[b], c0_ref.at[b], n0_ref.at[b], m0_ref.at[b], *consts,
                     hm_ref.at[b], convo_ref.at[b], co_ref.at[b], no_ref.at[b], mo_ref.at[b],
                     cbuf.at[b], c_s.at[b], n_s.at[b], m_s.at[b])


def _mlstm_chunk(m3_ref, conv0_ref, c0_ref, n0_ref, m0_ref, convw_ref, convb_ref, wq_ref, wk_ref,
                 wif_ref, wift_ref, bif_ref, bift_ref, normw_ref, skip_ref,
                 hm_ref, convo_ref, co_ref, no_ref, mo_ref, cbuf, c_s, n_s, m_s):
    j = pl.program_id(1)
    L = m3_ref.shape[0]
    W = MLSTM_WIDTH
    D = HEAD_DIM

    @pl.when(j == 0)
    def _():
        cbuf[0:8, :] = jnp.zeros((8, W), F32)
        cbuf[8 - (CONV_WIDTH - 1):8, :] = conv0_ref[...]
        c_s[...] = c0_ref[...]
        n_s[...] = n0_ref[...]
        m_s[...] = m0_ref[...]

    blk = m3_ref[...]
    c_in = blk[:, 0:W]
    v_m = blk[:, W:2 * W]
    z = blk[:, 2 * W:3 * W]
    cbuf[8:8 + L, :] = c_in
    acc = jnp.zeros((L, W), F32) + convb_ref[...]
    for tap in range(CONV_WIDTH):
        off = 8 - (CONV_WIDTH - 1) + tap
        acc = acc + cbuf[off:off + L, :] * convw_ref[tap:tap + 1, :]
    xc = _silu(acc)
    convo_ref[...] = cbuf[8 + L - (CONV_WIDTH - 1):8 + L, :]
    cbuf[0:8, :] = cbuf[L:L + 8, :]

    xcb = xc.astype(BF16)
    q_m = _dot(xcb, wq_ref[...])
    k_m = _dot(xcb, wk_ref[...])
    gate_in = jnp.concatenate([q_m, k_m, v_m], axis=1)
    nn, nt = ((1,), (0,)), ((1,), (1,))
    g_col = _dot_split(gate_in, wif_ref[...], nn) + bif_ref[...]
    g_row = _dot_split(wift_ref[...], gate_in, nt) + bift_ref[...]
    i_col, lf_col = g_col[:, 0:MLSTM_HEADS], _log_sigmoid(g_col[:, MLSTM_HEADS:])
    i_row, lf_row = g_row[0:MLSTM_HEADS, :], _log_sigmoid(g_row[MLSTM_HEADS:, :])
    rr = lax.broadcasted_iota(jnp.int32, (L, L), 0)
    cc = lax.broadcasted_iota(jnp.int32, (L, L), 1)
    causal = cc <= rr
    tri = causal.astype(F32)
    b_col = _dot_split(tri, lf_col, nn, a_exact=True)
    b_row = _dot_split(lf_row, tri, nt, b_exact=True)
    ks = k_m * (D ** -0.5)
    kb = ks.astype(BF16)
    vb = v_m.astype(BF16)
    n_h = MLSTM_HEADS
    heads = range(n_h)
    own = (lax.broadcasted_iota(jnp.int32, (n_h * L, W), 0) // L
           == lax.broadcasted_iota(jnp.int32, (n_h * L, W), 1) // D)
    q4 = jnp.where(own, jnp.concatenate([q_m] * n_h, axis=0), 0.0)
    q4b = q4.astype(BF16)
    m_prev = m_s[...]
    per_row = lambda a: jnp.concatenate([a[:, h:h + 1] for h in heads], axis=0)
    per_key = lambda a: jnp.concatenate([jnp.broadcast_to(a[h:h + 1, :], (L, L)) for h in heads], axis=0)
    bc = per_row(b_col)
    m_prev_r = jnp.concatenate([jnp.broadcast_to(m_prev[:, h:h + 1], (L, 1)) for h in heads], axis=0)
    causal4 = (lax.broadcasted_iota(jnp.int32, (n_h * L, L), 1)
               <= lax.broadcasted_iota(jnp.int32, (n_h * L, L), 0) % L)
    d_intra = jnp.where(causal4, bc - per_key(b_row) + per_key(i_row), -jnp.inf)
    a_inter = bc + m_prev_r
    m_t = jnp.maximum(a_inter, jnp.max(d_intra, axis=1, keepdims=True))
    s = _dot_nt(q4b, kb) * jnp.exp(d_intra - m_t)
    w_inter = jnp.exp(a_inter - m_t)
    c_all = c_s[...]
    c_bd = jnp.where(lax.broadcasted_iota(jnp.int32, (W, W), 0) // D == lax.broadcasted_iota(jnp.int32, (W, W), 1) // D,
                     jnp.concatenate([c_all] * n_h, axis=1), 0.0)
    n_all = n_s[...]
    n_flat = jnp.concatenate([n_all[h:h + 1, :] for h in heads], axis=1)
    num = _dot(s.astype(BF16), vb) + w_inter * _dot_nt(q4b, c_bd.astype(BF16))
    den = jnp.sum(s, axis=1, keepdims=True) + w_inter * jnp.sum(q4 * n_flat, axis=1, keepdims=True)
    hh = jnp.where(own, num / jnp.maximum(jnp.abs(den), jnp.exp(-m_t)), 0.0)
    mu = jnp.sum(hh, axis=1, keepdims=True) * (1.0 / D)
    hc = jnp.where(own, hh - mu, 0.0)
    var = jnp.sum(hc * hc, axis=1, keepdims=True) * (1.0 / D)
    hn = hc * lax.rsqrt(var + LN_EPS)
    h_all = hn[0:L]
    for h in range(1, n_h):
        h_all = h_all + hn[h * L:(h + 1) * L]
    g_tot = b_col[L - 1:L, :]
    a_end = g_tot + m_prev
    d_end = g_tot - b_col + i_col
    m_new = jnp.maximum(a_end, jnp.max(d_end, axis=0, keepdims=True))
    w_s = jnp.exp(d_end - m_new)
    decay = jnp.exp(a_end - m_new)
    w_s_l = jnp.concatenate([jnp.broadcast_to(w_s[:, h:h + 1], (L, D)) for h in heads], axis=1)
    upd = _dot_tn((v_m * w_s_l).astype(BF16), kb)
    decay_r = jnp.concatenate([jnp.broadcast_to(decay[:, h:h + 1], (D, 1)) for h in heads], axis=0)
    c_s[...] = decay_r * c_all + jnp.concatenate([upd[h * D:(h + 1) * D, h * D:(h + 1) * D] for h in heads], axis=0)
    k_sum = jnp.sum(w_s_l * ks, axis=0, keepdims=True)
    n_s[...] = (jnp.concatenate([jnp.broadcast_to(decay[:, h:h + 1], (1, D)) for h in heads], axis=0) * n_all
                + jnp.concatenate([k_sum[:, h * D:(h + 1) * D] for h in heads], axis=0))
    m_s[...] = m_new
    hm = (h_all * normw_ref[...] + skip_ref[...] * xc) * _silu(z)
    hm_ref[...] = hm.astype(hm_ref.dtype)
    co_ref[...] = c_s[...]
    no_ref[...] = n_s[...]
    mo_ref[...] = m_s[...]


def _mlstm(m3, conv0, c0, n0, m0, conv_w, conv_b, wq_bd, wk_bd, w_if, b_if, norm_w, skip, *, chunk, bb):
    b, t, _ = m3.shape
    W = MLSTM_WIDTH
    full = lambda a: pl.BlockSpec(a.shape, lambda bi, j: (0,) * a.ndim)
    per_b = lambda a: pl.BlockSpec((bb,) + a.shape[1:], lambda bi, j: (bi,) + (0,) * (a.ndim - 1))
    consts = [conv_w, conv_b.reshape(1, W), wq_bd, wk_bd, w_if, w_if.T, b_if.reshape(1, -1),
              b_if.reshape(-1, 1), norm_w.reshape(1, W), skip.reshape(1, W)]
    states = [conv0, c0, n0, m0]
    return pl.pallas_call(
        _mlstm_kernel,
        grid=(b // bb, t // chunk),
        in_specs=[pl.BlockSpec((bb, chunk, 3 * W), lambda bi, j: (bi, j, 0))] + [per_b(s) for s in states]
                 + [full(c) for c in consts],
        out_specs=[pl.BlockSpec((bb, chunk, W), lambda bi, j: (bi, j, 0))] + [per_b(s) for s in states],
        out_shape=[jax.ShapeDtypeStruct((b, t, W), BF16)] + [jax.ShapeDtypeStruct(s.shape, F32) for s in states],
        scratch_shapes=[pltpu.VMEM((bb, chunk + 8, W), F32), pltpu.VMEM((bb,) + c0.shape[1:], F32),
                        pltpu.VMEM((bb,) + n0.shape[1:], F32), pltpu.VMEM((bb,) + m0.shape[1:], F32)],
        compiler_params=_cparams("arbitrary", "arbitrary"),
        name="mlstm",
    )(m3, *states, *consts)


def _interleave(ref, scr):
    dil, rows, width = ref.shape
    if dil == 1:
        return ref[0].astype(F32)
    for r in range(dil):
        for c in range(width // 128):
            scr[c, pl.ds(r, rows, stride=dil), :] = ref[r, :, c * 128:(c + 1) * 128].astype(F32)
    return jnp.concatenate([scr[c] for c in range(width // 128)], axis=1)


def _combine(os, lses):
    tm = os[0].shape[0]
    lane = lax.broadcasted_iota(jnp.int32, (tm, GROUP_WIDTH), 1)

    def spread(l2):
        out = jnp.zeros((tm, GROUP_WIDTH), F32)
        for h in range(HEADS_PER_GROUP):
            out = jnp.where(lane // HEAD_DIM == h, l2[:, 32 * h:32 * h + 1], out)
        return out

    ls = [spread(l) for l in lses]
    top = jnp.maximum(jnp.maximum(ls[0], ls[1]), ls[2])
    es = [jnp.exp(l - top) for l in ls]
    tot = es[0] + es[1] + es[2]
    return [(o * (e / tot)).astype(BF16) for o, e in zip(os, es)]


def _outproj_kernel(combine, alpha, *refs):
    if combine:
        (o0, o1, o2, l0, l1, l2, hm_ref, x_ref, wo_ref, g_ref, b_ref, wq_ref, x1_ref, qc_ref, scr) = refs
        att = _combine([_interleave(o, scr) for o in (o0, o1, o2)], [_interleave(l, scr) for l in (l0, l1, l2)])
    else:
        (a_ref, hm_ref, x_ref, wo_ref, g_ref, b_ref, wq_ref, x1_ref, qc_ref) = refs
        att = [a_ref[:, g * GROUP_WIDTH:(g + 1) * GROUP_WIDTH].astype(BF16) for g in range(len(ATT_GROUPS))]
    mix = _dot(hm_ref[...].astype(BF16), wo_ref[ATT_WIDTH:, :])
    for g, a in enumerate(att):
        mix = mix + _dot(a, wo_ref[g * GROUP_WIDTH:(g + 1) * GROUP_WIDTH, :])
    x1 = _layer_norm(alpha * x_ref[...] + mix, g_ref[...], b_ref[...])
    x1_ref[...] = x1
    qc_ref[...] = (_dot(x1.astype(BF16), wq_ref[...]) * ((x1.shape[1] // MEM_HEADS) ** -0.5)).astype(qc_ref.dtype)


def _outproj(att_parts, hm, x, w_out, ln_g, ln_b, w_cq, alpha, *, tm):
    n, d = x.shape
    combine = len(att_parts) > 1
    row = lambda a: pl.BlockSpec((tm, a.shape[1]), lambda i: (i, 0))
    full = lambda a: pl.BlockSpec(a.shape, lambda i: (0, 0))

    def split(a):
        _, dil, ts, w = a.shape
        tiles_per_seq = ts * dil // tm
        return pl.BlockSpec((None, dil, tm // dil, w), lambda i: (i // tiles_per_seq, 0, i % tiles_per_seq, 0))

    consts = [w_out, ln_g.reshape(1, d), ln_b.reshape(1, d), w_cq]
    return pl.pallas_call(
        functools.partial(_outproj_kernel, combine, alpha),
        grid=(n // tm,),
        in_specs=[split(a) if combine else row(a) for a in att_parts] + [row(hm), row(x)] + [full(c) for c in consts],
        out_specs=[pl.BlockSpec((tm, d), lambda i: (i, 0))] * 2,
        out_shape=[jax.ShapeDtypeStruct((n, d), F32), jax.ShapeDtypeStruct((n, d), BF16)],
        scratch_shapes=[pltpu.VMEM((GROUP_WIDTH // 128, tm, 128), F32)] if combine else [],
        compiler_params=_cparams("arbitrary"),
        name="outproj_ln1",
    )(*att_parts, hm, x, *consts)


def _memkv_kernel(mem_ref, wk_ref, wv_ref, kv_ref):
    d = mem_ref.shape[2]
    mem = mem_ref[0].astype(BF16)
    kv_ref[0, :, 0:d] = _dot(mem, wk_ref[...])
    kv_ref[0, :, d:] = _dot(mem, wv_ref[...])


def _memkv(mem, w_ck, w_cv):
    b, m, d = mem.shape
    return pl.pallas_call(
        _memkv_kernel,
        grid=(b,),
        in_specs=[pl.BlockSpec((1, m, d), lambda i: (i, 0, 0)),
                  pl.BlockSpec(w_ck.shape, lambda i: (0, 0)), pl.BlockSpec(w_cv.shape, lambda i: (0, 0))],
        out_specs=pl.BlockSpec((1, m, 2 * d), lambda i: (i, 0, 0)),
        out_shape=jax.ShapeDtypeStruct((b, m, 2 * d), F32),
        compiler_params=_cparams("arbitrary"),
        name="memory_kv",
    )(mem, w_ck, w_cv)


def _xattn_kernel(q_ref, kv_ref, o_ref):
    d = q_ref.shape[2]
    hd = d // MEM_HEADS
    q = q_ref[0]
    for h in range(MEM_HEADS):
        k = kv_ref[0, :, h * hd:(h + 1) * hd].astype(BF16)
        v = kv_ref[0, :, d + h * hd:d + (h + 1) * hd].astype(BF16)
        s = _dot_nt(q[:, h * hd:(h + 1) * hd], k)
        p = jnp.exp(s - jnp.max(s, axis=1, keepdims=True))
        p = p / jnp.sum(p, axis=1, keepdims=True)
        o_ref[0, :, h * hd:(h + 1) * hd] = _dot(p.astype(BF16), v).astype(o_ref.dtype)


def _xattn(qc, kv, *, tq):
    b, t, d = qc.shape
    m = kv.shape[1]
    return pl.pallas_call(
        _xattn_kernel,
        grid=(b, t // tq),
        in_specs=[pl.BlockSpec((1, tq, d), lambda bi, j: (bi, j, 0)),
                  pl.BlockSpec((1, m, 2 * d), lambda bi, j: (bi, 0, 0))],
        out_specs=pl.BlockSpec((1, tq, d), lambda bi, j: (bi, j, 0)),
        out_shape=jax.ShapeDtypeStruct((b, t, d), BF16),
        compiler_params=_cparams("arbitrary", "arbitrary"),
        name="cross_attention",
    )(qc, kv)


def _xout_kernel(alpha, o_ref, x_ref, w_ref, g_ref, b_ref, y_ref):
    y_ref[...] = _layer_norm(alpha * x_ref[...] + _dot(o_ref[...], w_ref[...]), g_ref[...], b_ref[...])


def _xout(oc, x1, w_co, ln_g, ln_b, alpha, *, tm):
    n, d = x1.shape
    row = pl.BlockSpec((tm, d), lambda i: (i, 0))
    full = lambda a: pl.BlockSpec(a.shape, lambda i: (0, 0))
    consts = [w_co, ln_g.reshape(1, d), ln_b.reshape(1, d)]
    return pl.pallas_call(
        functools.partial(_xout_kernel, alpha),
        grid=(n // tm,),
        in_specs=[row, row] + [full(c) for c in consts],
        out_specs=row,
        out_shape=jax.ShapeDtypeStruct((n, d), F32),
        compiler_params=_cparams("arbitrary"),
        name="cross_out_ln2",
    )(oc, x1, *consts)


def _first_index_of_max(vals, idx, big):
    mx = jnp.max(vals, axis=0, keepdims=True)
    return mx, jnp.min(jnp.where(vals == mx, idx, big), axis=0, keepdims=True)


def _route(xb, wrt_ref, br_ref):
    tm = xb.shape[0]
    n_e = wrt_ref.shape[0]
    per_group = n_e // N_EXPERT_GROUPS
    logits = _dot_nt(wrt_ref[...], xb)
    scores = _sigmoid(logits)
    biased = scores + br_ref[...]
    e_idx = lax.broadcasted_iota(jnp.int32, (n_e, tm), 0).astype(F32)
    g_scores = []
    for g in range(N_EXPERT_GROUPS):
        sub = biased[g * per_group:(g + 1) * per_group, :]
        sidx = lax.broadcasted_iota(jnp.int32, (per_group, tm), 0).astype(F32)
        m1, a1 = _first_index_of_max(sub, sidx, per_group)
        m2 = jnp.max(jnp.where(sidx == a1, -jnp.inf, sub), axis=0, keepdims=True)
        g_scores.append(m1 + m2)
    gs = jnp.concatenate(g_scores, axis=0)
    g_idx = lax.broadcasted_iota(jnp.int32, (N_EXPERT_GROUPS, tm), 0).astype(F32)
    g_sel = jnp.zeros((N_EXPERT_GROUPS, tm), F32)
    work = gs
    for _ in range(TOPK_GROUPS):
        _, a = _first_index_of_max(work, g_idx, N_EXPERT_GROUPS)
        hit = g_idx == a
        g_sel = jnp.where(hit, 1.0, g_sel)
        work = jnp.where(hit, -jnp.inf, work)
    e_mask = jnp.concatenate(
        [jnp.broadcast_to(g_sel[g:g + 1, :], (per_group, tm)) for g in range(N_EXPERT_GROUPS)], axis=0)
    work = jnp.where(e_mask > 0.5, biased, -jnp.inf)
    sel = jnp.zeros((n_e, tm), F32)
    picks = []
    for _ in range(TOP_K):
        cand = jnp.where(sel > 0.5, -jnp.inf, work)
        mx = jnp.max(cand, axis=0, keepdims=True)
        a = jnp.min(jnp.where((cand == mx) & (sel < 0.5), e_idx, float(n_e)), axis=0, keepdims=True)
        pick = jnp.where(e_idx == a, 1.0, 0.0)
        picks.append(pick)
        sel = sel + pick
    w_sel = sel * scores
    gates_t = w_sel / jnp.sum(w_sel, axis=0, keepdims=True) * ROUTED_SCALE
    return gates_t, sel, picks


ROW_CHUNK = 16
TILE_CHUNKS = 16
TILE_ROWS = ROW_CHUNK * TILE_CHUNKS
TOKEN_BLOCK = 256
GATHER_AHEAD = 3
TILE_SLOTS = GATHER_AHEAD + 1
MAX_SEGMENT_PIECES = TOKEN_BLOCK // ROW_CHUNK
assert MAX_SEGMENT_PIECES <= TILE_SLOTS * TILE_CHUNKS


def _slab_rows(tb, n_e):
    return -(-(TOP_K * tb + n_e * (ROW_CHUNK - 1)) // 128) * 128


def _block_of(refs, first_blocks):
    i = pl.program_id(0)
    x = refs[0][...]
    for ref, first in zip(refs[1:], first_blocks[1:]):
        x = jnp.where(i >= first, ref[...], x)
    return x


def _multi_specs(arrays, tb):
    firsts, specs, start = [], [], 0
    for a in arrays:
        nb = a.shape[0] // tb
        firsts.append(start)
        specs.append(pl.BlockSpec((tb, a.shape[1]),
                                  lambda i, start=start, nb=nb: (jnp.clip(i - start, 0, nb - 1), 0)))
        start += nb
    return firsts, specs, start


def _dispatch_kernel(first_blocks, *refs):
    n_x = len(first_blocks)
    x_refs, (wrt_ref, br_ref, xs_ref, pwt_ref, meta_ref) = refs[:n_x], refs[n_x:]
    tb = x_refs[0].shape[0]
    n_e = wrt_ref.shape[0]
    slab = xs_ref.shape[1]
    xb = _block_of(x_refs, first_blocks).astype(BF16)
    gates_t, sel, picks = _route(xb, wrt_ref, br_ref)
    cnt = jnp.sum(sel, axis=1, keepdims=True)
    padded = jnp.floor((cnt + (ROW_CHUNK - 1)) * (1.0 / ROW_CHUNK)) * ROW_CHUNK
    padded_b = jnp.broadcast_to(padded, (n_e, 128))
    er = lax.broadcasted_iota(jnp.int32, (n_e, n_e), 0)
    ec = lax.broadcasted_iota(jnp.int32, (n_e, n_e), 1)
    off_b = _dot((ec < er).astype(BF16), padded_b.astype(BF16))
    tr = lax.broadcasted_iota(jnp.int32, (tb, tb), 0)
    tc = lax.broadcasted_iota(jnp.int32, (tb, tb), 1)
    rank = _dot(sel.astype(BF16), (tr < tc).astype(BF16))
    dest = off_b[:, 0:1] + rank
    dest_k = [jnp.sum(p * dest, axis=0, keepdims=True) for p in picks]
    gate_k = [jnp.sum(p * gates_t, axis=0, keepdims=True) for p in picks]
    meta_ref[0, 0] = padded_b
    meta_ref[0, 1] = off_b

    stacked = jnp.concatenate(dest_k + [jnp.zeros((8 - TOP_K, tb), F32)] + gate_k
                              + [jnp.zeros((128 - 8 - TOP_K, tb), F32)], axis=0)
    cols = jnp.transpose(stacked)
    step = 512
    for c0 in range(0, slab, step):
        lanes = lax.broadcasted_iota(jnp.int32, (tb, step), 1).astype(F32) + float(c0)
        w = jnp.zeros((tb, step), F32)
        for k in range(TOP_K):
            w = jnp.where(lanes == cols[:, k:k + 1], cols[:, 8 + k:9 + k], w)
        pwt_ref[0, :, c0:c0 + step] = w.astype(pwt_ref.dtype)
        onehot_t = jnp.where(w != 0.0, 1.0, 0.0).astype(BF16)
        xs_ref[0, c0:c0 + step, :] = _dot_tn(onehot_t, xb).astype(xs_ref.dtype)


def _dispatch(xs_in, w_router_t, b_router):
    d = xs_in[0].shape[1]
    n_e = w_router_t.shape[0]
    tb = TOKEN_BLOCK
    firsts, x_specs, nblk = _multi_specs(xs_in, tb)
    slab = _slab_rows(tb, n_e)
    assert slab % 512 == 0
    return pl.pallas_call(
        functools.partial(_dispatch_kernel, tuple(firsts)),
        grid=(nblk,),
        in_specs=x_specs + [pl.BlockSpec(w_router_t.shape, lambda i: (0, 0)),
                            pl.BlockSpec((n_e, 1), lambda i: (0, 0))],
        out_specs=[pl.BlockSpec((1, slab, d), lambda i: (i, 0, 0)),
                   pl.BlockSpec((1, tb, slab), lambda i: (i, 0, 0)),
                   pl.BlockSpec((1, 2, n_e, 128), lambda i: (i, 0, 0, 0))],
        out_shape=[jax.ShapeDtypeStruct((nblk, slab, d), BF16),
                   jax.ShapeDtypeStruct((nblk, tb, slab), BF16),
                   jax.ShapeDtypeStruct((nblk, 2, n_e, 128), F32)],
        compiler_params=_cparams("arbitrary"),
        name="moe_dispatch",
    )(*xs_in, w_router_t, b_router.reshape(n_e, 1))


def _schedule_kernel(n_e, nblk, t_max, zero_row, slot_ref, first_ref, total_ref, tile0_ref,
                     te_ref, nv_ref, row_ref, nt_ref):
    def per_expert(e, carry):
        def per_blocks(bu, c):
            for u in range(unroll):
                idx = e * nblk + bu * unroll + u
                p = slot_ref[idx]
                r0 = first_ref[idx]
                for j in range(MAX_SEGMENT_PIECES):
                    row_ref[p + j] = r0 + j * ROW_CHUNK
            return c
        unroll = max(u for u in range(1, 9) if nblk % u == 0)
        lax.fori_loop(0, nblk // unroll, per_blocks, 0)

        cnt = total_ref[e]
        tile = tile0_ref[e]
        n_t = (cnt + (TILE_CHUNKS - 1)) // TILE_CHUNKS
        start = tile * TILE_CHUNKS

        def pad(p, c):
            row_ref[p] = zero_row
            return c
        lax.fori_loop(start + cnt, start + n_t * TILE_CHUNKS, pad, 0)

        def per_tile(i, c):
            te_ref[tile + i] = e
            nv_ref[tile + i] = jnp.minimum(TILE_CHUNKS, cnt - i * TILE_CHUNKS)
            return c
        lax.fori_loop(0, n_t, per_tile, 0)
        return carry
    lax.fori_loop(0, n_e, per_expert, 0)

    last = n_e - 1
    tile = tile0_ref[last] + (total_ref[last] + (TILE_CHUNKS - 1)) // TILE_CHUNKS
    nt_ref[0] = tile
    last_e = te_ref[jnp.maximum(tile - 1, 0)]

    def idle_tile(t, c):
        te_ref[t] = last_e
        nv_ref[t] = 0
        return c
    lax.fori_loop(tile, t_max, idle_tile, 0)

    def idle_piece(p, c):
        row_ref[p] = zero_row
        return c
    lax.fori_loop(tile * TILE_CHUNKS, t_max * TILE_CHUNKS, idle_piece, 0)


def _tile_schedule(meta, slab):
    nblk, _, n_e, _ = meta.shape
    pieces = (meta[:, 0, :, 0].astype(jnp.int32) // ROW_CHUNK).T
    first_row = (meta[:, 1, :, 0].astype(jnp.int32) + (jnp.arange(nblk, dtype=jnp.int32) * slab)[:, None]).T
    before = jnp.cumsum(pieces, axis=1) - pieces
    total = jnp.sum(pieces, axis=1)
    tiles = (total + (TILE_CHUNKS - 1)) // TILE_CHUNKS
    tile0 = jnp.cumsum(tiles) - tiles
    slot = tile0[:, None] * TILE_CHUNKS + before
    t_max = -(-(TOP_K * nblk * TOKEN_BLOCK + nblk * n_e * (ROW_CHUNK - 1)) // TILE_ROWS) + n_e + TILE_SLOTS
    assert slab - ROW_CHUNK >= TOP_K * TOKEN_BLOCK + n_e * (ROW_CHUNK - 1)
    smem = pl.BlockSpec(memory_space=pltpu.SMEM)
    i32 = lambda n: jax.ShapeDtypeStruct((n,), jnp.int32)
    return pl.pallas_call(
        functools.partial(_schedule_kernel, n_e, nblk, t_max, slab - ROW_CHUNK),
        in_specs=[smem] * 4,
        out_specs=[smem, smem, smem, smem],
        out_shape=[i32(t_max), i32(t_max), i32(t_max * TILE_CHUNKS), i32(1)],
        name="moe_schedule",
    )(slot.reshape(-1), first_row.reshape(-1), total, tile0)


def _grouped_kernel(te_ref, nv_ref, row_ref, nt_ref, xs_hbm, wg_ref, wu_ref, wd_ref, ys_hbm,
                    lhs, obuf, wg_b, wu_b, wd_b, sem_in, sem_out):
    t = pl.program_id(0)
    n_tiles = nt_ref[0]

    def piece(ref, tile, s):
        row = pl.multiple_of(row_ref[tile * TILE_CHUNKS + s], ROW_CHUNK)
        return ref.at[pl.ds(row, ROW_CHUNK)]

    def local(buf, slot, s):
        start = s * ROW_CHUNK if isinstance(s, int) else pl.multiple_of(s * ROW_CHUNK, ROW_CHUNK)
        return buf.at[slot, pl.ds(start, ROW_CHUNK)]

    def gather(tile, s):
        slot = tile % TILE_SLOTS
        return pltpu.make_async_copy(piece(xs_hbm, tile, s), local(lhs, slot, s), sem_in.at[slot])

    def scatter(tile, s):
        slot = tile % TILE_SLOTS
        return pltpu.make_async_copy(local(obuf, slot, s), piece(ys_hbm, tile, s), sem_out.at[slot])

    def gathered_tile(slot):
        return pltpu.make_async_copy(xs_hbm.at[pl.ds(0, TILE_ROWS)], lhs.at[slot], sem_in.at[slot])

    def scattered_tile(slot):
        return pltpu.make_async_copy(obuf.at[slot], ys_hbm.at[pl.ds(0, TILE_ROWS)], sem_out.at[slot])

    def for_valid_pieces(tile, fn):
        def body(s, carry):
            fn(tile, s)
            return carry
        lax.fori_loop(0, nv_ref[tile], body, 0)

    @pl.when(t == 0)
    def _():
        for tile in range(GATHER_AHEAD):
            for s in range(TILE_CHUNKS):
                gather(tile, s).start()

    @pl.when((t >= TILE_SLOTS) & (t - TILE_SLOTS < n_tiles))
    def _():
        full = nv_ref[t - TILE_SLOTS] == TILE_CHUNKS

        @pl.when(full)
        def _():
            scattered_tile(t % TILE_SLOTS).wait()

        @pl.when(jnp.logical_not(full))
        def _():
            for_valid_pieces(t - TILE_SLOTS, lambda tile, s: scatter(tile, s).wait())

    @pl.when((t < n_tiles) & ((t == 0) | (te_ref[t] != te_ref[jnp.maximum(t - 1, 0)])))
    def _():
        wg_b[...] = wg_ref[0].astype(BF16)
        wu_b[...] = wu_ref[0].astype(BF16)
        wd_b[...] = wd_ref[0].astype(BF16)

    @pl.when(t < n_tiles)
    def _():
        slot = t % TILE_SLOTS
        gathered_tile(slot).wait()
        for s in range(TILE_CHUNKS):
            gather(t + GATHER_AHEAD, s).start()
        x = lhs[slot]
        h = _silu(_dot(x, wg_b[...])) * _dot(x, wu_b[...])
        obuf[slot] = _dot(h.astype(BF16), wd_b[...]).astype(obuf.dtype)

    @pl.when((t >= n_tiles) & (t < n_tiles + GATHER_AHEAD))
    def _():
        gathered_tile(t % TILE_SLOTS).wait()

    @pl.when(t < n_tiles)
    def _():
        full = nv_ref[t] == TILE_CHUNKS

        @pl.when(full)
        def _():
            for s in range(TILE_CHUNKS):
                scatter(t, s).start()

        @pl.when(jnp.logical_not(full))
        def _():
            for_valid_pieces(t, lambda tile, s: scatter(tile, s).start())


def _grouped_experts(xs, tile_e, n_valid, rows_tbl, n_tiles, w_gate, w_up, w_down):
    rows, d = xs.shape
    n_e, _, ff = w_gate.shape
    t_max = tile_e.shape[0]
    grid_spec = pltpu.PrefetchScalarGridSpec(
        num_scalar_prefetch=4,
        grid=(t_max,),
        in_specs=[pl.BlockSpec(memory_space=pl.ANY),
                  pl.BlockSpec((1, d, ff), lambda t, te, nv, sr, nt: (te[t], 0, 0)),
                  pl.BlockSpec((1, d, ff), lambda t, te, nv, sr, nt: (te[t], 0, 0)),
                  pl.BlockSpec((1, ff, d), lambda t, te, nv, sr, nt: (te[t], 0, 0))],
        out_specs=pl.BlockSpec(memory_space=pl.ANY),
        scratch_shapes=[pltpu.VMEM((TILE_SLOTS, TILE_ROWS, d), BF16), pltpu.VMEM((TILE_SLOTS, TILE_ROWS, d), BF16),
                        pltpu.VMEM((d, ff), BF16), pltpu.VMEM((d, ff), BF16), pltpu.VMEM((ff, d), BF16),
                        pltpu.SemaphoreType.DMA((TILE_SLOTS,)), pltpu.SemaphoreType.DMA((TILE_SLOTS,))],
    )
    return pl.pallas_call(
        _grouped_kernel,
        grid_spec=grid_spec,
        out_shape=jax.ShapeDtypeStruct((rows, d), BF16),
        input_output_aliases={4: 0},
        compiler_params=_cparams("arbitrary"),
        name="moe_grouped_experts",
    )(tile_e, n_valid, rows_tbl, n_tiles, xs, w_gate, w_up, w_down)


def _moe_out_kernel(alpha, first_blocks, pwt_ref, ys_ref, *refs):
    n_x = len(first_blocks)
    x_refs, (sg_ref, su_ref, sd_ref, g_ref, b_ref), y_refs = refs[:n_x], refs[n_x:n_x + 5], refs[n_x + 5:]
    i = pl.program_id(0)
    routed = _dot(pwt_ref[0], ys_ref[0])
    x = _block_of(x_refs, first_blocks)
    xb = x.astype(BF16)
    hs = _silu(_dot(xb, sg_ref[...])) * _dot(xb, su_ref[...])
    shared = _dot(hs.astype(BF16), sd_ref[...])
    y = _layer_norm(alpha * x + (routed + shared), g_ref[...], b_ref[...])
    bounds = list(first_blocks[1:]) + [pl.num_programs(0)]
    for y_ref, lo, hi in zip(y_refs, first_blocks, bounds):
        @pl.when((i >= lo) & (i < hi))
        def _(y_ref=y_ref):
            y_ref[...] = y


def _moe_out(pwt, ys, xs_in, ws_gate, ws_up, ws_down, ln_g, ln_b, alpha):
    d = xs_in[0].shape[1]
    nblk, tb, slab = pwt.shape
    firsts, x_specs, _ = _multi_specs(xs_in, tb)
    consts = [ws_gate, ws_up, ws_down, ln_g.reshape(1, d), ln_b.reshape(1, d)]
    return pl.pallas_call(
        functools.partial(_moe_out_kernel, alpha, tuple(firsts)),
        grid=(nblk,),
        in_specs=[pl.BlockSpec((1, tb, slab), lambda i: (i, 0, 0)),
                  pl.BlockSpec((1, slab, d), lambda i: (i, 0, 0))] + x_specs
                 + [pl.BlockSpec(c.shape, lambda i: (0, 0)) for c in consts],
        out_specs=x_specs,
        out_shape=[jax.ShapeDtypeStruct(x.shape, F32) for x in xs_in],
        compiler_params=_cparams("arbitrary"),
        name="moe_combine",
    )(pwt, ys.reshape(nblk, slab, d), *xs_in, *consts)


def _moe(xs_in, w_router_t, b_router, w_gate, w_up, w_down, ws_gate, ws_up, ws_down, ln_g, ln_b, alpha):
    xs, pwt, meta = _dispatch(xs_in, w_router_t, b_router)
    nblk, slab, d = xs.shape
    tile_e, n_valid, rows_tbl, n_tiles = _tile_schedule(meta, slab)
    ys = _grouped_experts(xs.reshape(nblk * slab, d), tile_e, n_valid, rows_tbl, n_tiles, w_gate, w_up, w_down)
    return _moe_out(pwt, ys, xs_in, ws_gate, ws_up, ws_down, ln_g, ln_b, alpha)


def _rope_tables(pos):
    half = HEAD_DIM // 2
    inv_freq = ROPE_THETA ** (-jnp.arange(half, dtype=F32) / half)
    ang = pos.astype(F32)[:, None] * inv_freq[None, :]
    cos, sin = jnp.cos(ang), jnp.sin(ang)
    return jnp.tile(jnp.concatenate([cos, cos], axis=1), (1, 2)), jnp.tile(jnp.concatenate([-sin, sin], axis=1), (1, 2))


def _block_diag(w):
    h, d, _ = w.shape
    out = jnp.zeros((h * d, h * d), w.dtype)
    for i in range(h):
        out = out.at[i * d:(i + 1) * d, i * d:(i + 1) * d].set(w[i])
    return out


def _pick(n, pref):
    return pref if n % pref == 0 else n


def kernel(x_prompt, x_sample, mem_prompt, cache_win128, cache_win512, cache_win2048, cache_mem_kv, state_conv, state_C, state_n, state_m, w_in, conv_w, conv_b, wq_m, wk_m, w_if, b_if, mh_norm_w, skip_m, w_out, ln1_g, ln1_b, w_cq, w_ck, w_cv, w_co, ln2_g, ln2_b, w_router, b_router, w_gate, w_up, w_down, ws_gate, ws_up, ws_down, ln3_g, ln3_b):
    depth = w_in.shape[0]
    assert depth == 1
    alpha = float((2 * depth) ** 0.25)
    bp, seq, d = x_prompt.shape
    bs, dec, _ = x_sample.shape
    assert seq % ATT_GROUPS[-1][0] == 0
    for c, (window, _) in zip((cache_win128, cache_win512, cache_win2048), ATT_GROUPS):
        assert c.shape[2] == window

    l = 0
    bf = lambda a: a.astype(BF16)
    w_in_b, w_out_b = bf(w_in[l]), bf(w_out[l])
    w_cq_b, w_ck_b, w_cv_b, w_co_b = bf(w_cq[l]), bf(w_ck[l]), bf(w_cv[l]), bf(w_co[l])
    wq_bd, wk_bd = bf(_block_diag(wq_m[l])), bf(_block_diag(wk_m[l]))
    w_router_t = bf(w_router[l].T)
    ws_gate_b, ws_up_b, ws_down_b = bf(ws_gate[l]), bf(ws_up[l]), bf(ws_down[l])

    def tail_of_layer(x1, qc, kv, batch, t, tm, tq):
        n = batch * t
        oc = _xattn(qc.reshape(batch, t, d), kv, tq=tq).reshape(n, d)
        return _xout(oc, x1, w_co_b, ln2_g[l], ln2_b[l], alpha, tm=tm)

    def mlstm(m3, batch, t, states, chunk, bb):
        return _mlstm(m3.reshape(batch, t, -1), *states, conv_w[l], conv_b[l], wq_bd, wk_bd, w_if[l], b_if[l],
                      mh_norm_w[l], skip_m[l], chunk=chunk, bb=bb)

    np_ = bp * seq
    xp = x_prompt.reshape(np_, d)
    cos_p, sin_p = _rope_tables(jnp.arange(seq))
    *qkv, m3, t128, t512, t2048 = _inproj_prompt(xp, w_in_b, cos_p, sin_p, seq=seq, tm=512)
    parts, lses = [], []
    for g in range(len(ATT_GROUPS)):
        o, lse = _win_attn(*qkv[3 * g:3 * g + 3])
        parts.append(o)
        lses.append(lse)
    zeros_p = [jnp.zeros((bp, CONV_WIDTH - 1, MLSTM_WIDTH), F32), jnp.zeros((bp, MLSTM_WIDTH, HEAD_DIM), F32),
               jnp.zeros((bp, MLSTM_HEADS, HEAD_DIM), F32), jnp.zeros((bp, 1, MLSTM_HEADS), F32)]
    hm_p, p_conv, p_c, p_n, p_m = mlstm(m3, bp, seq, zeros_p, 128, bp)
    x1, qc = _outproj(parts + lses, hm_p.reshape(np_, -1), xp, w_out_b, ln1_g[l], ln1_b[l], w_cq_b, alpha, tm=256)
    kv_p = _memkv(mem_prompt, w_ck_b, w_cv_b)
    x2_p = tail_of_layer(x1, qc, kv_p, bp, seq, 256, 512)

    ns = bs * dec
    xs = x_sample.reshape(ns, d)
    cos_s, sin_s = _rope_tables(jnp.tile(PAST_LEN + jnp.arange(dec), bs))
    qs, ks, vs, m3s = _inproj(xs, w_in_b, cos_s, sin_s, tm=ns)
    caches = [c[l].reshape(bs, c.shape[2], 2 * GROUP_WIDTH) for c in (cache_win128, cache_win512, cache_win2048)]
    att_s, s128, s512, s2048 = _dec_attn(qs.reshape(bs, dec, -1), ks.reshape(bs, dec, -1), vs.reshape(bs, dec, -1),
                                         caches)
    states_s = [state_conv[l], state_C[l].reshape(bs, MLSTM_WIDTH, HEAD_DIM), state_n[l],
                state_m[l].reshape(bs, 1, MLSTM_HEADS)]
    hm_s, s_conv, s_c, s_n, s_m = mlstm(m3s, bs, dec, states_s, dec, 4 if bs % 4 == 0 else 1)
    x1s, qcs = _outproj([att_s.reshape(ns, -1)], hm_s.reshape(ns, -1), xs, w_out_b, ln1_g[l], ln1_b[l], w_cq_b,
                        alpha, tm=ns)
    kv_s = cache_mem_kv[l].reshape(bs, cache_mem_kv.shape[2], 2 * d)
    x2_s = tail_of_layer(x1s, qcs, kv_s, bs, dec, ns, dec)

    y_p, y_s = _moe([x2_p, x2_s], w_router_t, b_router[l], w_gate[l], w_up[l], w_down[l],
                    ws_gate_b, ws_up_b, ws_down_b, ln3_g[l], ln3_b[l], alpha)

    win_shape = lambda a, b_: a.reshape(1, b_, a.shape[1], 2, HEADS_PER_GROUP, HEAD_DIM)
    return (y_p.reshape(bp, seq, d), y_s.reshape(bs, dec, d),
            win_shape(t128, bp), win_shape(t512, bp), win_shape(t2048, bp),
            kv_p.reshape(1, bp, mem_prompt.shape[1], 2, MEM_HEADS, d // MEM_HEADS),
            p_conv[None], p_c.reshape(1, bp, MLSTM_HEADS, HEAD_DIM, HEAD_DIM), p_n[None],
            p_m.reshape(1, bp, MLSTM_HEADS),
            win_shape(s128, bs), win_shape(s512, bs), win_shape(s2048, bs),
            s_conv[None], s_c.reshape(1, bs, MLSTM_HEADS, HEAD_DIM, HEAD_DIM), s_n[None],
            s_m.reshape(1, bs, MLSTM_HEADS))
```

```python
import functools
import math

import jax
import jax.numpy as jnp
from jax import lax
from jax.experimental import pallas as pl
from jax.experimental.pallas import tpu as pltpu

F32 = jnp.float32
BF16 = jnp.bfloat16

HEAD_DIM = 64
ATT_GROUPS = ((128, 1), (512, 4), (2048, 16))
HEADS_PER_GROUP = 4
GROUP_WIDTH = HEADS_PER_GROUP * HEAD_DIM
ATT_WIDTH = GROUP_WIDTH * len(ATT_GROUPS)
N_KEYS = 129
ROPE_THETA = 10000.0
PAST_LEN = 8192
MLSTM_HEADS = 4
MLSTM_WIDTH = MLSTM_HEADS * HEAD_DIM
CONV_WIDTH = 4
MEM_HEADS = 4
N_EXPERT_GROUPS = 8
TOPK_GROUPS = 4
TOP_K = 6
ROUTED_SCALE = 2.5
LN_EPS = 1e-5
NEG = -1e30
VMEM_LIMIT = 56 * 1024 * 1024


def _cparams(*sem):
    return pltpu.CompilerParams(dimension_semantics=sem, vmem_limit_bytes=VMEM_LIMIT)


def _dot(a, b):
    return jnp.dot(a, b, preferred_element_type=F32)


def _dot_nt(a, b):
    return lax.dot_general(a, b, (((1,), (1,)), ((), ())), preferred_element_type=F32)


def _dot_tn(a, b):
    return lax.dot_general(a, b, (((0,), (0,)), ((), ())), preferred_element_type=F32)


def _split_bf16(a):
    hi = a.astype(BF16)
    return hi, (a - hi.astype(F32)).astype(BF16)


def _dot_split(a, b, dims, a_exact=False, b_exact=False):
    dn = (dims, ((), ()))
    dot = lambda x, y: lax.dot_general(x, y, dn, preferred_element_type=F32)
    a_hi, a_lo = _split_bf16(a)
    b_hi, b_lo = _split_bf16(b)
    out = dot(a_hi, b_hi)
    if not b_exact:
        out = out + dot(a_hi, b_lo)
    if not a_exact:
        out = out + dot(a_lo, b_hi)
    return out


def _layer_norm(x, g, b):
    mu = jnp.mean(x, axis=-1, keepdims=True)
    xc = x - mu
    var = jnp.mean(xc * xc, axis=-1, keepdims=True)
    return xc * lax.rsqrt(var + LN_EPS) * g + b


def _sigmoid(x):
    return 1.0 / (1.0 + jnp.exp(-x))


def _silu(x):
    return x * _sigmoid(x)


def _log_sigmoid(x):
    return jnp.minimum(x, 0.0) - jnp.log(1.0 + jnp.exp(-jnp.abs(x)))


def _project_qkv(x_ref, w_ref, cos_ref, sin_ref):
    tm = x_ref.shape[0]
    x = x_ref[...].astype(BF16)
    cos = jnp.concatenate([cos_ref[...]] * (ATT_WIDTH // 128), axis=1)
    sin = jnp.concatenate([sin_ref[...]] * (ATT_WIDTH // 128), axis=1)
    lane = lax.broadcasted_iota(jnp.int32, (tm, ATT_WIDTH), 1)
    first_half = (lane % HEAD_DIM) < (HEAD_DIM // 2)

    def rope(t):
        fwd = pltpu.roll(t, ATT_WIDTH - HEAD_DIM // 2, 1)
        bwd = pltpu.roll(t, HEAD_DIM // 2, 1)
        return t * cos + jnp.where(first_half, fwd, bwd) * sin

    q = rope(_dot(x, w_ref[:, 0:ATT_WIDTH])) * (HEAD_DIM ** -0.5)
    k = rope(_dot(x, w_ref[:, ATT_WIDTH:2 * ATT_WIDTH]))
    v = _dot(x, w_ref[:, 2 * ATT_WIDTH:3 * ATT_WIDTH])
    return q, k, v, _dot(x, w_ref[:, 3 * ATT_WIDTH:])


def _inproj_kernel(x_ref, w_ref, cos_ref, sin_ref, q_ref, k_ref, v_ref, m_ref):
    q, k, v, m = _project_qkv(x_ref, w_ref, cos_ref, sin_ref)
    q_ref[...] = q.astype(q_ref.dtype)
    k_ref[...] = k.astype(k_ref.dtype)
    v_ref[...] = v.astype(v_ref.dtype)
    m_ref[...] = m


def _inproj_prompt_kernel(x_ref, w_ref, cos_ref, sin_ref, *refs):
    n_g = len(ATT_GROUPS)
    qkv_refs, m_ref, tails, scr = refs[:3 * n_g], refs[3 * n_g], refs[3 * n_g + 1:4 * n_g + 1], refs[4 * n_g + 1]
    tm = x_ref.shape[0]
    q, k, v, m = _project_qkv(x_ref, w_ref, cos_ref, sin_ref)
    m_ref[...] = m
    for g, (_, dil) in enumerate(ATT_GROUPS):
        cols = slice(g * GROUP_WIDTH, (g + 1) * GROUP_WIDTH)
        for a, val in enumerate((q, k, v)):
            o_ref = qkv_refs[3 * g + a]
            if dil == 1:
                o_ref[0, 0] = val[:, cols].astype(o_ref.dtype)
                continue
            for c in range(GROUP_WIDTH // 128):
                scr[c] = val[:, g * GROUP_WIDTH + c * 128:g * GROUP_WIDTH + (c + 1) * 128]
            for r in range(dil):
                for c in range(GROUP_WIDTH // 128):
                    o_ref[0, r, :, c * 128:(c + 1) * 128] = scr[c, pl.ds(r, tm // dil, stride=dil), :].astype(o_ref.dtype)
        t_ref = tails[g]
        rows = t_ref.shape[1]
        t_ref[0, :, 0:GROUP_WIDTH] = k[tm - rows:, cols]
        t_ref[0, :, GROUP_WIDTH:] = v[tm - rows:, cols]


def _inproj_prompt(x, w_in, cos, sin, *, seq, tm):
    n, d = x.shape
    tiles_per_seq = seq // tm
    batch = n // seq
    out_shape, out_specs = [], []
    for _, dil in ATT_GROUPS:
        for _ in range(3):
            out_shape.append(jax.ShapeDtypeStruct((batch, dil, seq // dil, GROUP_WIDTH), BF16))
            out_specs.append(pl.BlockSpec((1, dil, tm // dil, GROUP_WIDTH),
                                          lambda i: (i // tiles_per_seq, 0, i % tiles_per_seq, 0)))
    m_width = w_in.shape[1] - 3 * ATT_WIDTH
    out_shape.append(jax.ShapeDtypeStruct((n, m_width), F32))
    out_specs.append(pl.BlockSpec((tm, m_width), lambda i: (i, 0)))
    for window, _ in ATT_GROUPS:
        rows = min(window, tm)
        first = tiles_per_seq - window // rows if window > rows else tiles_per_seq - 1
        out_shape.append(jax.ShapeDtypeStruct((batch, min(window, seq), 2 * GROUP_WIDTH), F32))
        out_specs.append(pl.BlockSpec(
            (1, rows, 2 * GROUP_WIDTH),
            lambda i, first=first: (i // tiles_per_seq, jnp.maximum(i % tiles_per_seq - first, 0), 0)))
    return pl.pallas_call(
        _inproj_prompt_kernel,
        grid=(n // tm,),
        in_specs=[pl.BlockSpec((tm, d), lambda i: (i, 0)),
                  pl.BlockSpec(w_in.shape, lambda i: (0, 0)),
                  pl.BlockSpec((tm, 128), lambda i: (i % tiles_per_seq, 0)),
                  pl.BlockSpec((tm, 128), lambda i: (i % tiles_per_seq, 0))],
        out_specs=out_specs,
        out_shape=out_shape,
        scratch_shapes=[pltpu.VMEM((GROUP_WIDTH // 128, tm, 128), F32)],
        compiler_params=_cparams("arbitrary"),
        name="inproj_rope_prompt",
    )(x, w_in, cos, sin)


def _inproj(x, w_in, cos, sin, *, tm):
    n, d = x.shape
    out_shape = [jax.ShapeDtypeStruct((n, ATT_WIDTH), F32)] * 3 + [
        jax.ShapeDtypeStruct((n, w_in.shape[1] - 3 * ATT_WIDTH), F32)]
    row_spec = lambda w: pl.BlockSpec((tm, w), lambda i: (i, 0))
    out_specs = [row_spec(ATT_WIDTH)] * 3 + [row_spec(w_in.shape[1] - 3 * ATT_WIDTH)]
    tiles_per_seq = cos.shape[0] // tm
    return pl.pallas_call(
        _inproj_kernel,
        grid=(n // tm,),
        in_specs=[row_spec(d),
                  pl.BlockSpec(w_in.shape, lambda i: (0, 0)),
                  pl.BlockSpec((tm, 128), lambda i: (i % tiles_per_seq, 0)),
                  pl.BlockSpec((tm, 128), lambda i: (i % tiles_per_seq, 0))],
        out_specs=out_specs,
        out_shape=out_shape,
        compiler_params=_cparams("arbitrary"),
        name="inproj_rope",
    )(x, w_in, cos, sin)


def _win_attn_kernel(q_ref, kp_ref, kc_ref, vp_ref, vc_ref, o_ref, lse_ref):
    j = pl.program_id(2)
    tq = q_ref.shape[1]
    sub = N_KEYS - 1
    assert kp_ref.shape[1] == sub and tq % sub == 0
    k_all = jnp.concatenate([kp_ref[0], kc_ref[0]], axis=0)
    v_all = jnp.concatenate([vp_ref[0], vc_ref[0]], axis=0)
    n_h = HEADS_PER_GROUP
    row = lax.broadcasted_iota(jnp.int32, (n_h * sub, 2 * sub), 0) % sub
    col = lax.broadcasted_iota(jnp.int32, (n_h * sub, 2 * sub), 1)
    band = (col >= row) & (col <= row + sub)
    head_rows = lax.broadcasted_iota(jnp.int32, (n_h * sub, GROUP_WIDTH), 0) // sub
    head_lanes = lax.broadcasted_iota(jnp.int32, (n_h * sub, GROUP_WIDTH), 1) // HEAD_DIM
    own = head_rows == head_lanes
    lane_l = lax.broadcasted_iota(jnp.int32, (sub, 128), 1)
    for i in range(tq // sub):
        q = q_ref[0, i * sub:(i + 1) * sub, :]
        kk = k_all[i * sub:(i + 2) * sub]
        vv = v_all[i * sub:(i + 2) * sub]
        valid = band & ((j > 0) | (col >= sub)) if i == 0 else band
        q4 = jnp.concatenate([q] * n_h, axis=0)
        q4 = jnp.where(own, q4, jnp.zeros_like(q4))
        s = jnp.where(valid, _dot_nt(q4, kk), NEG)
        m = jnp.max(s, axis=1, keepdims=True)
        p = jnp.exp(s - m)
        l = jnp.sum(p, axis=1, keepdims=True)
        o4 = jnp.where(own, _dot(p.astype(BF16), vv) / l, 0.0)
        lse4 = m + jnp.log(l)
        o_acc = o4[0:sub]
        lse_acc = jnp.zeros((sub, 128), F32)
        for h in range(n_h):
            if h:
                o_acc = o_acc + o4[h * sub:(h + 1) * sub]
            lse_acc = jnp.where(lane_l // 32 == h, lse4[h * sub:(h + 1) * sub], lse_acc)
        o_ref[0, i * sub:(i + 1) * sub, :] = o_acc.astype(o_ref.dtype)
        lse_ref[0, i * sub:(i + 1) * sub, :] = lse_acc


def _win_attn(q, k, v, *, tq=512):
    b, dil, ts, _ = q.shape
    tq = min(tq, ts)
    sub = N_KEYS - 1
    cur = pl.BlockSpec((None, 1, tq, GROUP_WIDTH), lambda bi, r, j: (bi, r, j, 0))
    prev = pl.BlockSpec((None, 1, sub, GROUP_WIDTH),
                        lambda bi, r, j: (bi, r, jnp.maximum(j * (tq // sub) - 1, 0), 0))
    return pl.pallas_call(
        _win_attn_kernel,
        grid=(b, dil, ts // tq),
        in_specs=[cur, prev, cur, prev, cur],
        out_specs=[pl.BlockSpec((None, 1, tq, GROUP_WIDTH), lambda bi, r, j: (bi, r, j, 0)),
                   pl.BlockSpec((None, 1, tq, 128), lambda bi, r, j: (bi, r, j, 0))],
        out_shape=[jax.ShapeDtypeStruct((b, dil, ts, GROUP_WIDTH), BF16),
                   jax.ShapeDtypeStruct((b, dil, ts, 128), F32)],
        compiler_params=_cparams("arbitrary", "arbitrary", "arbitrary"),
        name="window_attention",
    )(q, k, k, v, v)


def _dec_attn_kernel(q_ref, k_ref, v_ref, c0_ref, c1_ref, c2_ref, att_ref, o0_ref, o1_ref, o2_ref,
                     e0_ref, e1_ref, e2_ref):
    t_new = q_ref.shape[1]
    n_h = HEADS_PER_GROUP
    q = q_ref[0]
    k_new = k_ref[0]
    v_new = v_ref[0]
    n_q = n_h * t_new
    own = (lax.broadcasted_iota(jnp.int32, (n_q, GROUP_WIDTH), 0) // t_new
           == lax.broadcasted_iota(jnp.int32, (n_q, GROUP_WIDTH), 1) // HEAD_DIM)
    outs, lses = [], []
    for g, ((window, dil), c_ref, o_ref, e_ref) in enumerate(
            zip(ATT_GROUPS, (c0_ref, c1_ref, c2_ref), (o0_ref, o1_ref, o2_ref), (e0_ref, e1_ref, e2_ref))):
        w = c_ref.shape[1]
        cols = slice(g * GROUP_WIDTH, (g + 1) * GROUP_WIDTH)
        kv_new = jnp.concatenate([k_new[:, cols], v_new[:, cols]], axis=1)
        n_chunk = e_ref.shape[0]
        pad_rows = e_ref.shape[1] - (w + t_new)
        if pad_rows:
            @pl.when(pl.program_id(0) == 0)
            def _(e_ref=e_ref, w=w, pad_rows=pad_rows):
                e_ref[:, w + t_new:, :] = jnp.zeros((n_chunk, pad_rows, 128), F32)
        for c in range(n_chunk):
            e_ref[c, 0:w, :] = c_ref[0, :, c * 128:(c + 1) * 128]
            e_ref[c, w:w + t_new, :] = kv_new[:, c * 128:(c + 1) * 128]
            o_ref[0, :, c * 128:(c + 1) * 128] = e_ref[c, t_new:w + t_new, :]
        classes = min(dil, t_new)
        parts = [jnp.concatenate([e_ref[c, pl.ds(r, DEC_SPAN, stride=dil), :] for c in range(n_chunk)], axis=1)
                 for r in range(classes)]
        kv_all = jnp.concatenate(parts, axis=0) if classes > 1 else parts[0]
        k_all = kv_all[:, 0:GROUP_WIDTH].astype(BF16)
        v_all = kv_all[:, GROUP_WIDTH:].astype(BF16)
        q4 = jnp.where(own, jnp.concatenate([q[:, cols]] * n_h, axis=0), 0.0)
        s = _dot_nt(q4.astype(BF16), k_all)
        t_idx = lax.broadcasted_iota(jnp.int32, s.shape, 0) % t_new
        col = lax.broadcasted_iota(jnp.int32, s.shape, 1)
        first = t_idx // dil
        step_i = col % DEC_SPAN
        valid = (col // DEC_SPAN == t_idx % dil) & (step_i >= first) & (step_i <= first + (N_KEYS - 1))
        s = jnp.where(valid, s, NEG)
        m = jnp.max(s, axis=1, keepdims=True)
        p = jnp.exp(s - m)
        l = jnp.sum(p, axis=1, keepdims=True)
        outs.append(_dot(p.astype(BF16), v_all) / l)
        lses.append(m + jnp.log(l))
    top = jnp.maximum(jnp.maximum(lses[0], lses[1]), lses[2])
    es = [jnp.exp(l - top) for l in lses]
    tot = es[0] + es[1] + es[2]
    for g in range(len(ATT_GROUPS)):
        weighted = jnp.where(own, outs[g] * (es[g] / tot), 0.0)
        acc = weighted[0:t_new]
        for h in range(1, n_h):
            acc = acc + weighted[h * t_new:(h + 1) * t_new]
        att_ref[0, :, g * GROUP_WIDTH:(g + 1) * GROUP_WIDTH] = acc


DEC_SPAN = 136


def _dec_attn(q, k, v, caches):
    b, t_new, _ = q.shape
    assert N_KEYS + t_new - 1 <= DEC_SPAN
    tok = pl.BlockSpec((1, t_new, ATT_WIDTH), lambda i: (i, 0, 0))
    cspec = [pl.BlockSpec((1,) + c.shape[1:], lambda i: (i, 0, 0)) for c in caches]

    def ext_rows(c, dil):
        return max(c.shape[1] + t_new, (min(dil, t_new) - 1) + (DEC_SPAN - 1) * dil + 1)

    return pl.pallas_call(
        _dec_attn_kernel,
        grid=(b,),
        in_specs=[tok, tok, tok] + cspec,
        out_specs=[tok] + cspec,
        out_shape=[jax.ShapeDtypeStruct(q.shape, F32)] + [jax.ShapeDtypeStruct(c.shape, F32) for c in caches],
        scratch_shapes=[pltpu.VMEM((c.shape[2] // 128, ext_rows(c, dil), 128), F32)
                        for c, (_, dil) in zip(caches, ATT_GROUPS)],
        compiler_params=_cparams("arbitrary"),
        name="decode_attention",
    )(q, k, v, *caches)


def _mlstm_kernel(m3_ref, conv0_ref, c0_ref, n0_ref, m0_ref, *rest):
    consts, (hm_ref, convo_ref, co_ref, no_ref, mo_ref, cbuf, c_s, n_s, m_s) = rest[:10], rest[10:]
    for b in range(m3_ref.shape[0]):
        _mlstm_chunk(m3_ref.at[b], conv0_ref.at[b], c0_ref.at[b], n0_ref.at[b], m0_ref.at[b], *consts,
                     hm_ref.at[b], convo_ref.at[b], co_ref.at[b], no_ref.at[b], mo_ref.at[b],
                     cbuf.at[b], c_s.at[b], n_s.at[b], m_s.at[b])


def _mlstm_chunk(m3_ref, conv0_ref, c0_ref, n0_ref, m0_ref, convw_ref, convb_ref, wq_ref, wk_ref,
                 wif_ref, wift_ref, bif_ref, bift_ref, normw_ref, skip_ref,
                 hm_ref, convo_ref, co_ref, no_ref, mo_ref, cbuf, c_s, n_s, m_s):
    j = pl.program_id(1)
    L = m3_ref.shape[0]
    W = MLSTM_WIDTH
    D = HEAD_DIM

    @pl.when(j == 0)
    def _():
        cbuf[0:8, :] = jnp.zeros((8, W), F32)
        cbuf[8 - (CONV_WIDTH - 1):8, :] = conv0_ref[...]
        c_s[...] = c0_ref[...]
        n_s[...] = n0_ref[...]
        m_s[...] = m0_ref[...]

    blk = m3_ref[...]
    c_in = blk[:, 0:W]
    v_m = blk[:, W:2 * W]
    z = blk[:, 2 * W:3 * W]
    cbuf[8:8 + L, :] = c_in
    acc = jnp.zeros((L, W), F32) + convb_ref[...]
    for tap in range(CONV_WIDTH):
        off = 8 - (CONV_WIDTH - 1) + tap
        acc = acc + cbuf[off:off + L, :] * convw_ref[tap:tap + 1, :]
    xc = _silu(acc)
    convo_ref[...] = cbuf[8 + L - (CONV_WIDTH - 1):8 + L, :]
    cbuf[0:8, :] = cbuf[L:L + 8, :]

    xcb = xc.astype(BF16)
    q_m = _dot(xcb, wq_ref[...])
    k_m = _dot(xcb, wk_ref[...])
    gate_in = jnp.concatenate([q_m, k_m, v_m], axis=1)
    nn, nt = ((1,), (0,)), ((1,), (1,))
    g_col = _dot_split(gate_in, wif_ref[...], nn) + bif_ref[...]
    g_row = _dot_split(wift_ref[...], gate_in, nt) + bift_ref[...]
    i_col, lf_col = g_col[:, 0:MLSTM_HEADS], _log_sigmoid(g_col[:, MLSTM_HEADS:])
    i_row, lf_row = g_row[0:MLSTM_HEADS, :], _log_sigmoid(g_row[MLSTM_HEADS:, :])
    rr = lax.broadcasted_iota(jnp.int32, (L, L), 0)
    cc = lax.broadcasted_iota(jnp.int32, (L, L), 1)
    causal = cc <= rr
    tri = causal.astype(F32)
    b_col = _dot_split(tri, lf_col, nn, a_exact=True)
    b_row = _dot_split(lf_row, tri, nt, b_exact=True)
    ks = k_m * (D ** -0.5)
    kb = ks.astype(BF16)
    vb = v_m.astype(BF16)
    n_h = MLSTM_HEADS
    heads = range(n_h)
    own = (lax.broadcasted_iota(jnp.int32, (n_h * L, W), 0) // L
           == lax.broadcasted_iota(jnp.int32, (n_h * L, W), 1) // D)
    q4 = jnp.where(own, jnp.concatenate([q_m] * n_h, axis=0), 0.0)
    q4b = q4.astype(BF16)
    m_prev = m_s[...]
    per_row = lambda a: jnp.concatenate([a[:, h:h + 1] for h in heads], axis=0)
    per_key = lambda a: jnp.concatenate([jnp.broadcast_to(a[h:h + 1, :], (L, L)) for h in heads], axis=0)
    bc = per_row(b_col)
    m_prev_r = jnp.concatenate([jnp.broadcast_to(m_prev[:, h:h + 1], (L, 1)) for h in heads], axis=0)
    causal4 = (lax.broadcasted_iota(jnp.int32, (n_h * L, L), 1)
               <= lax.broadcasted_iota(jnp.int32, (n_h * L, L), 0) % L)
    d_intra = jnp.where(causal4, bc - per_key(b_row) + per_key(i_row), -jnp.inf)
    a_inter = bc + m_prev_r
    m_t = jnp.maximum(a_inter, jnp.max(d_intra, axis=1, keepdims=True))
    s = _dot_nt(q4b, kb) * jnp.exp(d_intra - m_t)
    w_inter = jnp.exp(a_inter - m_t)
    c_all = c_s[...]
    c_bd = jnp.where(lax.broadcasted_iota(jnp.int32, (W, W), 0) // D == lax.broadcasted_iota(jnp.int32, (W, W), 1) // D,
                     jnp.concatenate([c_all] * n_h, axis=1), 0.0)
    n_all = n_s[...]
    n_flat = jnp.concatenate([n_all[h:h + 1, :] for h in heads], axis=1)
    num = _dot(s.astype(BF16), vb) + w_inter * _dot_nt(q4b, c_bd.astype(BF16))
    den = jnp.sum(s, axis=1, keepdims=True) + w_inter * jnp.sum(q4 * n_flat, axis=1, keepdims=True)
    hh = jnp.where(own, num / jnp.maximum(jnp.abs(den), jnp.exp(-m_t)), 0.0)
    mu = jnp.sum(hh, axis=1, keepdims=True) * (1.0 / D)
    hc = jnp.where(own, hh - mu, 0.0)
    var = jnp.sum(hc * hc, axis=1, keepdims=True) * (1.0 / D)
    hn = hc * lax.rsqrt(var + LN_EPS)
    h_all = hn[0:L]
    for h in range(1, n_h):
        h_all = h_all + hn[h * L:(h + 1) * L]
    g_tot = b_col[L - 1:L, :]
    a_end = g_tot + m_prev
    d_end = g_tot - b_col + i_col
    m_new = jnp.maximum(a_end, jnp.max(d_end, axis=0, keepdims=True))
    w_s = jnp.exp(d_end - m_new)
    decay = jnp.exp(a_end - m_new)
    w_s_l = jnp.concatenate([jnp.broadcast_to(w_s[:, h:h + 1], (L, D)) for h in heads], axis=1)
    upd = _dot_tn((v_m * w_s_l).astype(BF16), kb)
    decay_r = jnp.concatenate([jnp.broadcast_to(decay[:, h:h + 1], (D, 1)) for h in heads], axis=0)
    c_s[...] = decay_r * c_all + jnp.concatenate([upd[h * D:(h + 1) * D, h * D:(h + 1) * D] for h in heads], axis=0)
    k_sum = jnp.sum(w_s_l * ks, axis=0, keepdims=True)
    n_s[...] = (jnp.concatenate([jnp.broadcast_to(decay[:, h:h + 1], (1, D)) for h in heads], axis=0) * n_all
                + jnp.concatenate([k_sum[:, h * D:(h + 1) * D] for h in heads], axis=0))
    m_s[...] = m_new
    hm = (h_all * normw_ref[...] + skip_ref[...] * xc) * _silu(z)
    hm_ref[...] = hm.astype(hm_ref.dtype)
    co_ref[...] = c_s[...]
    no_ref[...] = n_s[...]
    mo_ref[...] = m_s[...]


def _mlstm(m3, conv0, c0, n0, m0, conv_w, conv_b, wq_bd, wk_bd, w_if, b_if, norm_w, skip, *, chunk, bb):
    b, t, _ = m3.shape
    W = MLSTM_WIDTH
    full = lambda a: pl.BlockSpec(a.shape, lambda bi, j: (0,) * a.ndim)
    per_b = lambda a: pl.BlockSpec((bb,) + a.shape[1:], lambda bi, j: (bi,) + (0,) * (a.ndim - 1))
    consts = [conv_w, conv_b.reshape(1, W), wq_bd, wk_bd, w_if, w_if.T, b_if.reshape(1, -1),
              b_if.reshape(-1, 1), norm_w.reshape(1, W), skip.reshape(1, W)]
    states = [conv0, c0, n0, m0]
    return pl.pallas_call(
        _mlstm_kernel,
        grid=(b // bb, t // chunk),
        in_specs=[pl.BlockSpec((bb, chunk, 3 * W), lambda bi, j: (bi, j, 0))] + [per_b(s) for s in states]
                 + [full(c) for c in consts],
        out_specs=[pl.BlockSpec((bb, chunk, W), lambda bi, j: (bi, j, 0))] + [per_b(s) for s in states],
        out_shape=[jax.ShapeDtypeStruct((b, t, W), BF16)] + [jax.ShapeDtypeStruct(s.shape, F32) for s in states],
        scratch_shapes=[pltpu.VMEM((bb, chunk + 8, W), F32), pltpu.VMEM((bb,) + c0.shape[1:], F32),
                        pltpu.VMEM((bb,) + n0.shape[1:], F32), pltpu.VMEM((bb,) + m0.shape[1:], F32)],
        compiler_params=_cparams("arbitrary", "arbitrary"),
        name="mlstm",
    )(m3, *states, *consts)


def _interleave(ref, scr):
    dil, rows, width = ref.shape
    if dil == 1:
        return ref[0].astype(F32)
    for r in range(dil):
        for c in range(width // 128):
            scr[c, pl.ds(r, rows, stride=dil), :] = ref[r, :, c * 128:(c + 1) * 128].astype(F32)
    return jnp.concatenate([scr[c] for c in range(width // 128)], axis=1)


def _combine(os, lses):
    tm = os[0].shape[0]
    lane = lax.broadcasted_iota(jnp.int32, (tm, GROUP_WIDTH), 1)

    def spread(l2):
        out = jnp.zeros((tm, GROUP_WIDTH), F32)
        for h in range(HEADS_PER_GROUP):
            out = jnp.where(lane // HEAD_DIM == h, l2[:, 32 * h:32 * h + 1], out)
        return out

    ls = [spread(l) for l in lses]
    top = jnp.maximum(jnp.maximum(ls[0], ls[1]), ls[2])
    es = [jnp.exp(l - top) for l in ls]
    tot = es[0] + es[1] + es[2]
    return [(o * (e / tot)).astype(BF16) for o, e in zip(os, es)]


def _outproj_kernel(combine, alpha, *refs):
    if combine:
        (o0, o1, o2, l0, l1, l2, hm_ref, x_ref, wo_ref, g_ref, b_ref, wq_ref, x1_ref, qc_ref, scr) = refs
        att = _combine([_interleave(o, scr) for o in (o0, o1, o2)], [_interleave(l, scr) for l in (l0, l1, l2)])
    else:
        (a_ref, hm_ref, x_ref, wo_ref, g_ref, b_ref, wq_ref, x1_ref, qc_ref) = refs
        att = [a_ref[:, g * GROUP_WIDTH:(g + 1) * GROUP_WIDTH].astype(BF16) for g in range(len(ATT_GROUPS))]
    mix = _dot(hm_ref[...].astype(BF16), wo_ref[ATT_WIDTH:, :])
    for g, a in enumerate(att):
        mix = mix + _dot(a, wo_ref[g * GROUP_WIDTH:(g + 1) * GROUP_WIDTH, :])
    x1 = _layer_norm(alpha * x_ref[...] + mix, g_ref[...], b_ref[...])
    x1_ref[...] = x1
    qc_ref[...] = (_dot(x1.astype(BF16), wq_ref[...]) * ((x1.shape[1] // MEM_HEADS) ** -0.5)).astype(qc_ref.dtype)


def _outproj(att_parts, hm, x, w_out, ln_g, ln_b, w_cq, alpha, *, tm):
    n, d = x.shape
    combine = len(att_parts) > 1
    row = lambda a: pl.BlockSpec((tm, a.shape[1]), lambda i: (i, 0))
    full = lambda a: pl.BlockSpec(a.shape, lambda i: (0, 0))

    def split(a):
        _, dil, ts, w = a.shape
        tiles_per_seq = ts * dil // tm
        return pl.BlockSpec((None, dil, tm // dil, w), lambda i: (i // tiles_per_seq, 0, i % tiles_per_seq, 0))

    consts = [w_out, ln_g.reshape(1, d), ln_b.reshape(1, d), w_cq]
    return pl.pallas_call(
        functools.partial(_outproj_kernel, combine, alpha),
        grid=(n // tm,),
        in_specs=[split(a) if combine else row(a) for a in att_parts] + [row(hm), row(x)] + [full(c) for c in consts],
        out_specs=[pl.BlockSpec((tm, d), lambda i: (i, 0))] * 2,
        out_shape=[jax.ShapeDtypeStruct((n, d), F32), jax.ShapeDtypeStruct((n, d), BF16)],
        scratch_shapes=[pltpu.VMEM((GROUP_WIDTH // 128, tm, 128), F32)] if combine else [],
        compiler_params=_cparams("arbitrary"),
        name="outproj_ln1",
    )(*att_parts, hm, x, *consts)


def _memkv_kernel(mem_ref, wk_ref, wv_ref, kv_ref):
    d = mem_ref.shape[2]
    mem = mem_ref[0].astype(BF16)
    kv_ref[0, :, 0:d] = _dot(mem, wk_ref[...])
    kv_ref[0, :, d:] = _dot(mem, wv_ref[...])


def _memkv(mem, w_ck, w_cv):
    b, m, d = mem.shape
    return pl.pallas_call(
        _memkv_kernel,
        grid=(b,),
        in_specs=[pl.BlockSpec((1, m, d), lambda i: (i, 0, 0)),
                  pl.BlockSpec(w_ck.shape, lambda i: (0, 0)), pl.BlockSpec(w_cv.shape, lambda i: (0, 0))],
        out_specs=pl.BlockSpec((1, m, 2 * d), lambda i: (i, 0, 0)),
        out_shape=jax.ShapeDtypeStruct((b, m, 2 * d), F32),
        compiler_params=_cparams("arbitrary"),
        name="memory_kv",
    )(mem, w_ck, w_cv)


def _xattn_kernel(q_ref, kv_ref, o_ref):
    d = q_ref.shape[2]
    hd = d // MEM_HEADS
    q = q_ref[0]
    for h in range(MEM_HEADS):
        k = kv_ref[0, :, h * hd:(h + 1) * hd].astype(BF16)
        v = kv_ref[0, :, d + h * hd:d + (h + 1) * hd].astype(BF16)
        s = _dot_nt(q[:, h * hd:(h + 1) * hd], k)
        p = jnp.exp(s - jnp.max(s, axis=1, keepdims=True))
        p = p / jnp.sum(p, axis=1, keepdims=True)
        o_ref[0, :, h * hd:(h + 1) * hd] = _dot(p.astype(BF16), v).astype(o_ref.dtype)


def _xattn(qc, kv, *, tq):
    b, t, d = qc.shape
    m = kv.shape[1]
    return pl.pallas_call(
        _xattn_kernel,
        grid=(b, t // tq),
        in_specs=[pl.BlockSpec((1, tq, d), lambda bi, j: (bi, j, 0)),
                  pl.BlockSpec((1, m, 2 * d), lambda bi, j: (bi, 0, 0))],
        out_specs=pl.BlockSpec((1, tq, d), lambda bi, j: (bi, j, 0)),
        out_shape=jax.ShapeDtypeStruct((b, t, d), BF16),
        compiler_params=_cparams("arbitrary", "arbitrary"),
        name="cross_attention",
    )(qc, kv)


def _xout_kernel(alpha, o_ref, x_ref, w_ref, g_ref, b_ref, y_ref):
    y_ref[...] = _layer_norm(alpha * x_ref[...] + _dot(o_ref[...], w_ref[...]), g_ref[...], b_ref[...])


def _xout(oc, x1, w_co, ln_g, ln_b, alpha, *, tm):
    n, d = x1.shape
    row = pl.BlockSpec((tm, d), lambda i: (i, 0))
    full = lambda a: pl.BlockSpec(a.shape, lambda i: (0, 0))
    consts = [w_co, ln_g.reshape(1, d), ln_b.reshape(1, d)]
    return pl.pallas_call(
        functools.partial(_xout_kernel, alpha),
        grid=(n // tm,),
        in_specs=[row, row] + [full(c) for c in consts],
        out_specs=row,
        out_shape=jax.ShapeDtypeStruct((n, d), F32),
        compiler_params=_cparams("arbitrary"),
        name="cross_out_ln2",
    )(oc, x1, *consts)


def _first_index_of_max(vals, idx, big):
    mx = jnp.max(vals, axis=0, keepdims=True)
    return mx, jnp.min(jnp.where(vals == mx, idx, big), axis=0, keepdims=True)


def _route(xb, wrt_ref, br_ref):
    tm = xb.shape[0]
    n_e = wrt_ref.shape[0]
    per_group = n_e // N_EXPERT_GROUPS
    logits = _dot_nt(wrt_ref[...], xb)
    scores = _sigmoid(logits)
    biased = scores + br_ref[...]
    e_idx = lax.broadcasted_iota(jnp.int32, (n_e, tm), 0).astype(F32)
    g_scores = []
    for g in range(N_EXPERT_GROUPS):
        sub = biased[g * per_group:(g + 1) * per_group, :]
        sidx = lax.broadcasted_iota(jnp.int32, (per_group, tm), 0).astype(F32)
        m1, a1 = _first_index_of_max(sub, sidx, per_group)
        m2 = jnp.max(jnp.where(sidx == a1, -jnp.inf, sub), axis=0, keepdims=True)
        g_scores.append(m1 + m2)
    gs = jnp.concatenate(g_scores, axis=0)
    g_idx = lax.broadcasted_iota(jnp.int32, (N_EXPERT_GROUPS, tm), 0).astype(F32)
    g_sel = jnp.zeros((N_EXPERT_GROUPS, tm), F32)
    work = gs
    for _ in range(TOPK_GROUPS):
        _, a = _first_index_of_max(work, g_idx, N_EXPERT_GROUPS)
        hit = g_idx == a
        g_sel = jnp.where(hit, 1.0, g_sel)
        work = jnp.where(hit, -jnp.inf, work)
    e_mask = jnp.concatenate(
        [jnp.broadcast_to(g_sel[g:g + 1, :], (per_group, tm)) for g in range(N_EXPERT_GROUPS)], axis=0)
    work = jnp.where(e_mask > 0.5, biased, -jnp.inf)
    sel = jnp.zeros((n_e, tm), F32)
    picks = []
    for _ in range(TOP_K):
        cand = jnp.where(sel > 0.5, -jnp.inf, work)
        mx = jnp.max(cand, axis=0, keepdims=True)
        a = jnp.min(jnp.where((cand == mx) & (sel < 0.5), e_idx, float(n_e)), axis=0, keepdims=True)
        pick = jnp.where(e_idx == a, 1.0, 0.0)
        picks.append(pick)
        sel = sel + pick
    w_sel = sel * scores
    gates_t = w_sel / jnp.sum(w_sel, axis=0, keepdims=True) * ROUTED_SCALE
    return gates_t, sel, picks


ROW_CHUNK = 16
TILE_CHUNKS = 32
TILE_ROWS = ROW_CHUNK * TILE_CHUNKS
TOKEN_BLOCK = 256
GATHER_AHEAD = 3
TILE_SLOTS = GATHER_AHEAD + 1
MAX_SEGMENT_PIECES = TOKEN_BLOCK // ROW_CHUNK
assert MAX_SEGMENT_PIECES <= TILE_SLOTS * TILE_CHUNKS


def _slab_rows(tb, n_e):
    return -(-(TOP_K * tb + n_e * (ROW_CHUNK - 1)) // 128) * 128


def _block_of(refs, first_blocks):
    i = pl.program_id(0)
    x = refs[0][...]
    for ref, first in zip(refs[1:], first_blocks[1:]):
        x = jnp.where(i >= first, ref[...], x)
    return x


def _multi_specs(arrays, tb):
    firsts, specs, start = [], [], 0
    for a in arrays:
        nb = a.shape[0] // tb
        firsts.append(start)
        specs.append(pl.BlockSpec((tb, a.shape[1]),
                                  lambda i, start=start, nb=nb: (jnp.clip(i - start, 0, nb - 1), 0)))
        start += nb
    return firsts, specs, start


def _dispatch_kernel(first_blocks, *refs):
    n_x = len(first_blocks)
    x_refs, (wrt_ref, br_ref, xs_ref, pwt_ref, meta_ref) = refs[:n_x], refs[n_x:]
    tb = x_refs[0].shape[0]
    n_e = wrt_ref.shape[0]
    slab = xs_ref.shape[1]
    xb = _block_of(x_refs, first_blocks).astype(BF16)
    gates_t, sel, picks = _route(xb, wrt_ref, br_ref)
    cnt = jnp.sum(sel, axis=1, keepdims=True)
    padded = jnp.floor((cnt + (ROW_CHUNK - 1)) * (1.0 / ROW_CHUNK)) * ROW_CHUNK
    padded_b = jnp.broadcast_to(padded, (n_e, 128))
    er = lax.broadcasted_iota(jnp.int32, (n_e, n_e), 0)
    ec = lax.broadcasted_iota(jnp.int32, (n_e, n_e), 1)
    off_b = _dot((ec < er).astype(BF16), padded_b.astype(BF16))
    tr = lax.broadcasted_iota(jnp.int32, (tb, tb), 0)
    tc = lax.broadcasted_iota(jnp.int32, (tb, tb), 1)
    rank = _dot(sel.astype(BF16), (tr < tc).astype(BF16))
    dest = off_b[:, 0:1] + rank
    dest_k = [jnp.sum(p * dest, axis=0, keepdims=True) for p in picks]
    gate_k = [jnp.sum(p * gates_t, axis=0, keepdims=True) for p in picks]
    meta_ref[0, 0] = padded_b
    meta_ref[0, 1] = off_b

    stacked = jnp.concatenate(dest_k + [jnp.zeros((8 - TOP_K, tb), F32)] + gate_k
                              + [jnp.zeros((128 - 8 - TOP_K, tb), F32)], axis=0)
    cols = jnp.transpose(stacked)
    step = 512
    for c0 in range(0, slab, step):
        lanes = lax.broadcasted_iota(jnp.int32, (tb, step), 1).astype(F32) + float(c0)
        w = jnp.zeros((tb, step), F32)
        for k in range(TOP_K):
            w = jnp.where(lanes == cols[:, k:k + 1], cols[:, 8 + k:9 + k], w)
        pwt_ref[0, :, c0:c0 + step] = w.astype(pwt_ref.dtype)
        onehot_t = jnp.where(w != 0.0, 1.0, 0.0).astype(BF16)
        xs_ref[0, c0:c0 + step, :] = _dot_tn(onehot_t, xb).astype(xs_ref.dtype)


def _dispatch(xs_in, w_router_t, b_router):
    d = xs_in[0].shape[1]
    n_e = w_router_t.shape[0]
    tb = TOKEN_BLOCK
    firsts, x_specs, nblk = _multi_specs(xs_in, tb)
    slab = _slab_rows(tb, n_e)
    assert slab % 512 == 0
    return pl.pallas_call(
        functools.partial(_dispatch_kernel, tuple(firsts)),
        grid=(nblk,),
        in_specs=x_specs + [pl.BlockSpec(w_router_t.shape, lambda i: (0, 0)),
                            pl.BlockSpec((n_e, 1), lambda i: (0, 0))],
        out_specs=[pl.BlockSpec((1, slab, d), lambda i: (i, 0, 0)),
                   pl.BlockSpec((1, tb, slab), lambda i: (i, 0, 0)),
                   pl.BlockSpec((1, 2, n_e, 128), lambda i: (i, 0, 0, 0))],
        out_shape=[jax.ShapeDtypeStruct((nblk, slab, d), BF16),
                   jax.ShapeDtypeStruct((nblk, tb, slab), BF16),
                   jax.ShapeDtypeStruct((nblk, 2, n_e, 128), F32)],
        compiler_params=_cparams("arbitrary"),
        name="moe_dispatch",
    )(*xs_in, w_router_t, b_router.reshape(n_e, 1))


def _schedule_kernel(n_e, nblk, t_max, zero_row, slot_ref, first_ref, total_ref, tile0_ref,
                     te_ref, nv_ref, row_ref, nt_ref):
    def per_expert(e, carry):
        def per_blocks(bu, c):
            for u in range(unroll):
                idx = e * nblk + bu * unroll + u
                p = slot_ref[idx]
                r0 = first_ref[idx]
                for j in range(MAX_SEGMENT_PIECES):
                    row_ref[p + j] = r0 + j * ROW_CHUNK
            return c
        unroll = max(u for u in range(1, 9) if nblk % u == 0)
        lax.fori_loop(0, nblk // unroll, per_blocks, 0)

        cnt = total_ref[e]
        tile = tile0_ref[e]
        n_t = (cnt + (TILE_CHUNKS - 1)) // TILE_CHUNKS
        start = tile * TILE_CHUNKS

        def pad(p, c):
            row_ref[p] = zero_row
            return c
        lax.fori_loop(start + cnt, start + n_t * TILE_CHUNKS, pad, 0)

        def per_tile(i, c):
            te_ref[tile + i] = e
            nv_ref[tile + i] = jnp.minimum(TILE_CHUNKS, cnt - i * TILE_CHUNKS)
            return c
        lax.fori_loop(0, n_t, per_tile, 0)
        return carry
    lax.fori_loop(0, n_e, per_expert, 0)

    last = n_e - 1
    tile = tile0_ref[last] + (total_ref[last] + (TILE_CHUNKS - 1)) // TILE_CHUNKS
    nt_ref[0] = tile
    last_e = te_ref[jnp.maximum(tile - 1, 0)]

    def idle_tile(t, c):
        te_ref[t] = last_e
        nv_ref[t] = 0
        return c
    lax.fori_loop(tile, t_max, idle_tile, 0)

    def idle_piece(p, c):
        row_ref[p] = zero_row
        return c
    lax.fori_loop(tile * TILE_CHUNKS, t_max * TILE_CHUNKS, idle_piece, 0)


def _tile_schedule(meta, slab):
    nblk, _, n_e, _ = meta.shape
    pieces = (meta[:, 0, :, 0].astype(jnp.int32) // ROW_CHUNK).T
    first_row = (meta[:, 1, :, 0].astype(jnp.int32) + (jnp.arange(nblk, dtype=jnp.int32) * slab)[:, None]).T
    before = jnp.cumsum(pieces, axis=1) - pieces
    total = jnp.sum(pieces, axis=1)
    tiles = (total + (TILE_CHUNKS - 1)) // TILE_CHUNKS
    tile0 = jnp.cumsum(tiles) - tiles
    slot = tile0[:, None] * TILE_CHUNKS + before
    t_max = -(-(TOP_K * nblk * TOKEN_BLOCK + nblk * n_e * (ROW_CHUNK - 1)) // TILE_ROWS) + n_e + TILE_SLOTS
    assert slab - ROW_CHUNK >= TOP_K * TOKEN_BLOCK + n_e * (ROW_CHUNK - 1)
    smem = pl.BlockSpec(memory_space=pltpu.SMEM)
    i32 = lambda n: jax.ShapeDtypeStruct((n,), jnp.int32)
    return pl.pallas_call(
        functools.partial(_schedule_kernel, n_e, nblk, t_max, slab - ROW_CHUNK),
        in_specs=[smem] * 4,
        out_specs=[smem, smem, smem, smem],
        out_shape=[i32(t_max), i32(t_max), i32(t_max * TILE_CHUNKS), i32(1)],
        name="moe_schedule",
    )(slot.reshape(-1), first_row.reshape(-1), total, tile0)


def _grouped_kernel(te_ref, nv_ref, row_ref, nt_ref, xs_hbm, wg_ref, wu_ref, wd_ref, ys_hbm,
                    lhs, obuf, wg_b, wu_b, wd_b, sem_in, sem_out):
    t = pl.program_id(0)
    n_tiles = nt_ref[0]

    def piece(ref, tile, s):
        row = pl.multiple_of(row_ref[tile * TILE_CHUNKS + s], ROW_CHUNK)
        return ref.at[pl.ds(row, ROW_CHUNK)]

    def local(buf, slot, s):
        start = s * ROW_CHUNK if isinstance(s, int) else pl.multiple_of(s * ROW_CHUNK, ROW_CHUNK)
        return buf.at[slot, pl.ds(start, ROW_CHUNK)]

    def gather(tile, s):
        slot = tile % TILE_SLOTS
        return pltpu.make_async_copy(piece(xs_hbm, tile, s), local(lhs, slot, s), sem_in.at[slot])

    def scatter(tile, s):
        slot = tile % TILE_SLOTS
        return pltpu.make_async_copy(local(obuf, slot, s), piece(ys_hbm, tile, s), sem_out.at[slot])

    def gathered_tile(slot):
        return pltpu.make_async_copy(xs_hbm.at[pl.ds(0, TILE_ROWS)], lhs.at[slot], sem_in.at[slot])

    def scattered_tile(slot):
        return pltpu.make_async_copy(obuf.at[slot], ys_hbm.at[pl.ds(0, TILE_ROWS)], sem_out.at[slot])

    def for_valid_pieces(tile, fn):
        def body(s, carry):
            fn(tile, s)
            return carry
        lax.fori_loop(0, nv_ref[tile], body, 0)

    @pl.when(t == 0)
    def _():
        for tile in range(GATHER_AHEAD):
            for s in range(TILE_CHUNKS):
                gather(tile, s).start()

    @pl.when((t >= TILE_SLOTS) & (t - TILE_SLOTS < n_tiles))
    def _():
        full = nv_ref[t - TILE_SLOTS] == TILE_CHUNKS

        @pl.when(full)
        def _():
            scattered_tile(t % TILE_SLOTS).wait()

        @pl.when(jnp.logical_not(full))
        def _():
            for_valid_pieces(t - TILE_SLOTS, lambda tile, s: scatter(tile, s).wait())

    @pl.when((t < n_tiles) & ((t == 0) | (te_ref[t] != te_ref[jnp.maximum(t - 1, 0)])))
    def _():
        wg_b[...] = wg_ref[0].astype(BF16)
        wu_b[...] = wu_ref[0].astype(BF16)
        wd_b[...] = wd_ref[0].astype(BF16)

    @pl.when(t < n_tiles)
    def _():
        slot = t % TILE_SLOTS
        gathered_tile(slot).wait()
        for s in range(TILE_CHUNKS):
            gather(t + GATHER_AHEAD, s).start()
        x = lhs[slot]
        h = _silu(_dot(x, wg_b[...])) * _dot(x, wu_b[...])
        obuf[slot] = _dot(h.astype(BF16), wd_b[...]).astype(obuf.dtype)

    @pl.when((t >= n_tiles) & (t < n_tiles + GATHER_AHEAD))
    def _():
        gathered_tile(t % TILE_SLOTS).wait()

    @pl.when(t < n_tiles)
    def _():
        full = nv_ref[t] == TILE_CHUNKS

        @pl.when(full)
        def _():
            for s in range(TILE_CHUNKS):
                scatter(t, s).start()

        @pl.when(jnp.logical_not(full))
        def _():
            for_valid_pieces(t, lambda tile, s: scatter(tile, s).start())


def _grouped_experts(xs, tile_e, n_valid, rows_tbl, n_tiles, w_gate, w_up, w_down):
    rows, d = xs.shape
    n_e, _, ff = w_gate.shape
    t_max = tile_e.shape[0]
    grid_spec = pltpu.PrefetchScalarGridSpec(
        num_scalar_prefetch=4,
        grid=(t_max,),
        in_specs=[pl.BlockSpec(memory_space=pl.ANY),
                  pl.BlockSpec((1, d, ff), lambda t, te, nv, sr, nt: (te[t], 0, 0)),
                  pl.BlockSpec((1, d, ff), lambda t, te, nv, sr, nt: (te[t], 0, 0)),
                  pl.BlockSpec((1, ff, d), lambda t, te, nv, sr, nt: (te[t], 0, 0))],
        out_specs=pl.BlockSpec(memory_space=pl.ANY),
        scratch_shapes=[pltpu.VMEM((TILE_SLOTS, TILE_ROWS, d), BF16), pltpu.VMEM((TILE_SLOTS, TILE_ROWS, d), BF16),
                        pltpu.VMEM((d, ff), BF16), pltpu.VMEM((d, ff), BF16), pltpu.VMEM((ff, d), BF16),
                        pltpu.SemaphoreType.DMA((TILE_SLOTS,)), pltpu.SemaphoreType.DMA((TILE_SLOTS,))],
    )
    return pl.pallas_call(
        _grouped_kernel,
        grid_spec=grid_spec,
        out_shape=jax.ShapeDtypeStruct((rows, d), BF16),
        input_output_aliases={4: 0},
        compiler_params=_cparams("arbitrary"),
        name="moe_grouped_experts",
    )(tile_e, n_valid, rows_tbl, n_tiles, xs, w_gate, w_up, w_down)


def _moe_out_kernel(alpha, first_blocks, pwt_ref, ys_ref, *refs):
    n_x = len(first_blocks)
    x_refs, (sg_ref, su_ref, sd_ref, g_ref, b_ref), y_refs = refs[:n_x], refs[n_x:n_x + 5], refs[n_x + 5:]
    i = pl.program_id(0)
    routed = _dot(pwt_ref[0], ys_ref[0])
    x = _block_of(x_refs, first_blocks)
    xb = x.astype(BF16)
    hs = _silu(_dot(xb, sg_ref[...])) * _dot(xb, su_ref[...])
    shared = _dot(hs.astype(BF16), sd_ref[...])
    y = _layer_norm(alpha * x + (routed + shared), g_ref[...], b_ref[...])
    bounds = list(first_blocks[1:]) + [pl.num_programs(0)]
    for y_ref, lo, hi in zip(y_refs, first_blocks, bounds):
        @pl.when((i >= lo) & (i < hi))
        def _(y_ref=y_ref):
            y_ref[...] = y


def _moe_out(pwt, ys, xs_in, ws_gate, ws_up, ws_down, ln_g, ln_b, alpha):
    d = xs_in[0].shape[1]
    nblk, tb, slab = pwt.shape
    firsts, x_specs, _ = _multi_specs(xs_in, tb)
    consts = [ws_gate, ws_up, ws_down, ln_g.reshape(1, d), ln_b.reshape(1, d)]
    return pl.pallas_call(
        functools.partial(_moe_out_kernel, alpha, tuple(firsts)),
        grid=(nblk,),
        in_specs=[pl.BlockSpec((1, tb, slab), lambda i: (i, 0, 0)),
                  pl.BlockSpec((1, slab, d), lambda i: (i, 0, 0))] + x_specs
                 + [pl.BlockSpec(c.shape, lambda i: (0, 0)) for c in consts],
        out_specs=x_specs,
        out_shape=[jax.ShapeDtypeStruct(x.shape, F32) for x in xs_in],
        compiler_params=_cparams("arbitrary"),
        name="moe_combine",
    )(pwt, ys.reshape(nblk, slab, d), *xs_in, *consts)


def _moe(xs_in, w_router_t, b_router, w_gate, w_up, w_down, ws_gate, ws_up, ws_down, ln_g, ln_b, alpha):
    xs, pwt, meta = _dispatch(xs_in, w_router_t, b_router)
    nblk, slab, d = xs.shape
    tile_e, n_valid, rows_tbl, n_tiles = _tile_schedule(meta, slab)
    ys = _grouped_experts(xs.reshape(nblk * slab, d), tile_e, n_valid, rows_tbl, n_tiles, w_gate, w_up, w_down)
    return _moe_out(pwt, ys, xs_in, ws_gate, ws_up, ws_down, ln_g, ln_b, alpha)


def _rope_tables(pos):
    half = HEAD_DIM // 2
    inv_freq = ROPE_THETA ** (-jnp.arange(half, dtype=F32) / half)
    ang = pos.astype(F32)[:, None] * inv_freq[None, :]
    cos, sin = jnp.cos(ang), jnp.sin(ang)
    return jnp.tile(jnp.concatenate([cos, cos], axis=1), (1, 2)), jnp.tile(jnp.concatenate([-sin, sin], axis=1), (1, 2))


def _block_diag(w):
    h, d, _ = w.shape
    out = jnp.zeros((h * d, h * d), w.dtype)
    for i in range(h):
        out = out.at[i * d:(i + 1) * d, i * d:(i + 1) * d].set(w[i])
    return out


def _pick(n, pref):
    return pref if n % pref == 0 else n


def kernel(x_prompt, x_sample, mem_prompt, cache_win128, cache_win512, cache_win2048, cache_mem_kv, state_conv, state_C, state_n, state_m, w_in, conv_w, conv_b, wq_m, wk_m, w_if, b_if, mh_norm_w, skip_m, w_out, ln1_g, ln1_b, w_cq, w_ck, w_cv, w_co, ln2_g, ln2_b, w_router, b_router, w_gate, w_up, w_down, ws_gate, ws_up, ws_down, ln3_g, ln3_b):
    depth = w_in.shape[0]
    assert depth == 1
    alpha = float((2 * depth) ** 0.25)
    bp, seq, d = x_prompt.shape
    bs, dec, _ = x_sample.shape
    assert seq % ATT_GROUPS[-1][0] == 0
    for c, (window, _) in zip((cache_win128, cache_win512, cache_win2048), ATT_GROUPS):
        assert c.shape[2] == window

    l = 0
    bf = lambda a: a.astype(BF16)
    w_in_b, w_out_b = bf(w_in[l]), bf(w_out[l])
    w_cq_b, w_ck_b, w_cv_b, w_co_b = bf(w_cq[l]), bf(w_ck[l]), bf(w_cv[l]), bf(w_co[l])
    wq_bd, wk_bd = bf(_block_diag(wq_m[l])), bf(_block_diag(wk_m[l]))
    w_router_t = bf(w_router[l].T)
    ws_gate_b, ws_up_b, ws_down_b = bf(ws_gate[l]), bf(ws_up[l]), bf(ws_down[l])

    def tail_of_layer(x1, qc, kv, batch, t, tm, tq):
        n = batch * t
        oc = _xattn(qc.reshape(batch, t, d), kv, tq=tq).reshape(n, d)
        return _xout(oc, x1, w_co_b, ln2_g[l], ln2_b[l], alpha, tm=tm)

    def mlstm(m3, batch, t, states, chunk, bb):
        return _mlstm(m3.reshape(batch, t, -1), *states, conv_w[l], conv_b[l], wq_bd, wk_bd, w_if[l], b_if[l],
                      mh_norm_w[l], skip_m[l], chunk=chunk, bb=bb)

    np_ = bp * seq
    xp = x_prompt.reshape(np_, d)
    cos_p, sin_p = _rope_tables(jnp.arange(seq))
    *qkv, m3, t128, t512, t2048 = _inproj_prompt(xp, w_in_b, cos_p, sin_p, seq=seq, tm=512)
    parts, lses = [], []
    for g in range(len(ATT_GROUPS)):
        o, lse = _win_attn(*qkv[3 * g:3 * g + 3])
        parts.append(o)
        lses.append(lse)
    zeros_p = [jnp.zeros((bp, CONV_WIDTH - 1, MLSTM_WIDTH), F32), jnp.zeros((bp, MLSTM_WIDTH, HEAD_DIM), F32),
               jnp.zeros((bp, MLSTM_HEADS, HEAD_DIM), F32), jnp.zeros((bp, 1, MLSTM_HEADS), F32)]
    hm_p, p_conv, p_c, p_n, p_m = mlstm(m3, bp, seq, zeros_p, 128, bp)
    x1, qc = _outproj(parts + lses, hm_p.reshape(np_, -1), xp, w_out_b, ln1_g[l], ln1_b[l], w_cq_b, alpha, tm=256)
    kv_p = _memkv(mem_prompt, w_ck_b, w_cv_b)
    x2_p = tail_of_layer(x1, qc, kv_p, bp, seq, 256, 512)

    ns = bs * dec
    xs = x_sample.reshape(ns, d)
    cos_s, sin_s = _rope_tables(jnp.tile(PAST_LEN + jnp.arange(dec), bs))
    qs, ks, vs, m3s = _inproj(xs, w_in_b, cos_s, sin_s, tm=ns)
    caches = [c[l].reshape(bs, c.shape[2], 2 * GROUP_WIDTH) for c in (cache_win128, cache_win512, cache_win2048)]
    att_s, s128, s512, s2048 = _dec_attn(qs.reshape(bs, dec, -1), ks.reshape(bs, dec, -1), vs.reshape(bs, dec, -1),
                                         caches)
    states_s = [state_conv[l], state_C[l].reshape(bs, MLSTM_WIDTH, HEAD_DIM), state_n[l],
                state_m[l].reshape(bs, 1, MLSTM_HEADS)]
    hm_s, s_conv, s_c, s_n, s_m = mlstm(m3s, bs, dec, states_s, dec, 4 if bs % 4 == 0 else 1)
    x1s, qcs = _outproj([att_s.reshape(ns, -1)], hm_s.reshape(ns, -1), xs, w_out_b, ln1_g[l], ln1_b[l], w_cq_b,
                        alpha, tm=ns)
    kv_s = cache_mem_kv[l].reshape(bs, cache_mem_kv.shape[2], 2 * d)
    x2_s = tail_of_layer(x1s, qcs, kv_s, bs, dec, ns, dec)

    y_p, y_s = _moe([x2_p, x2_s], w_router_t, b_router[l], w_gate[l], w_up[l], w_down[l],
                    ws_gate_b, ws_up_b, ws_down_b, ln3_g[l], ln3_b[l], alpha)

    win_shape = lambda a, b_: a.reshape(1, b_, a.shape[1], 2, HEADS_PER_GROUP, HEAD_DIM)
    return (y_p.reshape(bp, seq, d), y_s.reshape(bs, dec, d),
            win_shape(t128, bp), win_shape(t512, bp), win_shape(t2048, bp),
            kv_p.reshape(1, bp, mem_prompt.shape[1], 2, MEM_HEADS, d // MEM_HEADS),
            p_conv[None], p_c.reshape(1, bp, MLSTM_HEADS, HEAD_DIM, HEAD_DIM), p_n[None],
            p_m.reshape(1, bp, MLSTM_HEADS),
            win_shape(s128, bs), win_shape(s512, bs), win_shape(s2048, bs),
            s_conv[None], s_c.reshape(1, bs, MLSTM_HEADS, HEAD_DIM, HEAD_DIM), s_n[None],
            s_m.reshape(1, bs, MLSTM_HEADS))
```

```python
import functools
import math

import jax
import jax.numpy as jnp
from jax import lax
from jax.experimental import pallas as pl
from jax.experimental.pallas import tpu as pltpu

F32 = jnp.float32
BF16 = jnp.bfloat16

HEAD_DIM = 64
ATT_GROUPS = ((128, 1), (512, 4), (2048, 16))
HEADS_PER_GROUP = 4
GROUP_WIDTH = HEADS_PER_GROUP * HEAD_DIM
ATT_WIDTH = GROUP_WIDTH * len(ATT_GROUPS)
N_KEYS = 129
ROPE_THETA = 10000.0
PAST_LEN = 8192
MLSTM_HEADS = 4
MLSTM_WIDTH = MLSTM_HEADS * HEAD_DIM
CONV_WIDTH = 4
MEM_HEADS = 4
N_EXPERT_GROUPS = 8
TOPK_GROUPS = 4
TOP_K = 6
ROUTED_SCALE = 2.5
LN_EPS = 1e-5
NEG = -1e30
VMEM_LIMIT = 56 * 1024 * 1024


def _cparams(*sem):
    return pltpu.CompilerParams(dimension_semantics=sem, vmem_limit_bytes=VMEM_LIMIT)


def _dot(a, b):
    return jnp.dot(a, b, preferred_element_type=F32)


def _dot_nt(a, b):
    return lax.dot_general(a, b, (((1,), (1,)), ((), ())), preferred_element_type=F32)


def _dot_tn(a, b):
    return lax.dot_general(a, b, (((0,), (0,)), ((), ())), preferred_element_type=F32)


def _split_bf16(a):
    hi = a.astype(BF16)
    return hi, (a - hi.astype(F32)).astype(BF16)


def _dot_split(a, b, dims, a_exact=False, b_exact=False):
    dn = (dims, ((), ()))
    dot = lambda x, y: lax.dot_general(x, y, dn, preferred_element_type=F32)
    a_hi, a_lo = _split_bf16(a)
    b_hi, b_lo = _split_bf16(b)
    out = dot(a_hi, b_hi)
    if not b_exact:
        out = out + dot(a_hi, b_lo)
    if not a_exact:
        out = out + dot(a_lo, b_hi)
    return out


def _layer_norm(x, g, b):
    mu = jnp.mean(x, axis=-1, keepdims=True)
    xc = x - mu
    var = jnp.mean(xc * xc, axis=-1, keepdims=True)
    return xc * lax.rsqrt(var + LN_EPS) * g + b


def _sigmoid(x):
    return 1.0 / (1.0 + jnp.exp(-x))


def _silu(x):
    return x * _sigmoid(x)


def _log_sigmoid(x):
    return jnp.minimum(x, 0.0) - jnp.log(1.0 + jnp.exp(-jnp.abs(x)))


def _project_qkv(x_ref, w_ref, cos_ref, sin_ref):
    tm = x_ref.shape[0]
    x = x_ref[...].astype(BF16)
    cos = jnp.concatenate([cos_ref[...]] * (ATT_WIDTH // 128), axis=1)
    sin = jnp.concatenate([sin_ref[...]] * (ATT_WIDTH // 128), axis=1)
    lane = lax.broadcasted_iota(jnp.int32, (tm, ATT_WIDTH), 1)
    first_half = (lane % HEAD_DIM) < (HEAD_DIM // 2)

    def rope(t):
        fwd = pltpu.roll(t, ATT_WIDTH - HEAD_DIM // 2, 1)
        bwd = pltpu.roll(t, HEAD_DIM // 2, 1)
        return t * cos + jnp.where(first_half, fwd, bwd) * sin

    q = rope(_dot(x, w_ref[:, 0:ATT_WIDTH])) * (HEAD_DIM ** -0.5)
    k = rope(_dot(x, w_ref[:, ATT_WIDTH:2 * ATT_WIDTH]))
    v = _dot(x, w_ref[:, 2 * ATT_WIDTH:3 * ATT_WIDTH])
    return q, k, v, _dot(x, w_ref[:, 3 * ATT_WIDTH:])


def _inproj_kernel(x_ref, w_ref, cos_ref, sin_ref, q_ref, k_ref, v_ref, m_ref):
    q, k, v, m = _project_qkv(x_ref, w_ref, cos_ref, sin_ref)
    q_ref[...] = q.astype(q_ref.dtype)
    k_ref[...] = k.astype(k_ref.dtype)
    v_ref[...] = v.astype(v_ref.dtype)
    m_ref[...] = m


def _inproj_prompt_kernel(x_ref, w_ref, cos_ref, sin_ref, *refs):
    n_g = len(ATT_GROUPS)
    qkv_refs, m_ref, tails, scr = refs[:3 * n_g], refs[3 * n_g], refs[3 * n_g + 1:4 * n_g + 1], refs[4 * n_g + 1]
    tm = x_ref.shape[0]
    q, k, v, m = _project_qkv(x_ref, w_ref, cos_ref, sin_ref)
    m_ref[...] = m
    for g, (_, dil) in enumerate(ATT_GROUPS):
        cols = slice(g * GROUP_WIDTH, (g + 1) * GROUP_WIDTH)
        for a, val in enumerate((q, k, v)):
            o_ref = qkv_refs[3 * g + a]
            if dil == 1:
                o_ref[0, 0] = val[:, cols].astype(o_ref.dtype)
                continue
            for c in range(GROUP_WIDTH // 128):
                scr[c] = val[:, g * GROUP_WIDTH + c * 128:g * GROUP_WIDTH + (c + 1) * 128]
            for r in range(dil):
                for c in range(GROUP_WIDTH // 128):
                    o_ref[0, r, :, c * 128:(c + 1) * 128] = scr[c, pl.ds(r, tm // dil, stride=dil), :].astype(o_ref.dtype)
        t_ref = tails[g]
        rows = t_ref.shape[1]
        t_ref[0, :, 0:GROUP_WIDTH] = k[tm - rows:, cols]
        t_ref[0, :, GROUP_WIDTH:] = v[tm - rows:, cols]


def _inproj_prompt(x, w_in, cos, sin, *, seq, tm):
    n, d = x.shape
    tiles_per_seq = seq // tm
    batch = n // seq
    out_shape, out_specs = [], []
    for _, dil in ATT_GROUPS:
        for _ in range(3):
            out_shape.append(jax.ShapeDtypeStruct((batch, dil, seq // dil, GROUP_WIDTH), BF16))
            out_specs.append(pl.BlockSpec((1, dil, tm // dil, GROUP_WIDTH),
                                          lambda i: (i // tiles_per_seq, 0, i % tiles_per_seq, 0)))
    m_width = w_in.shape[1] - 3 * ATT_WIDTH
    out_shape.append(jax.ShapeDtypeStruct((n, m_width), F32))
    out_specs.append(pl.BlockSpec((tm, m_width), lambda i: (i, 0)))
    for window, _ in ATT_GROUPS:
        rows = min(window, tm)
        first = tiles_per_seq - window // rows if window > rows else tiles_per_seq - 1
        out_shape.append(jax.ShapeDtypeStruct((batch, min(window, seq), 2 * GROUP_WIDTH), F32))
        out_specs.append(pl.BlockSpec(
            (1, rows, 2 * GROUP_WIDTH),
            lambda i, first=first: (i // tiles_per_seq, jnp.maximum(i % tiles_per_seq - first, 0), 0)))
    return pl.pallas_call(
        _inproj_prompt_kernel,
        grid=(n // tm,),
        in_specs=[pl.BlockSpec((tm, d), lambda i: (i, 0)),
                  pl.BlockSpec(w_in.shape, lambda i: (0, 0)),
                  pl.BlockSpec((tm, 128), lambda i: (i % tiles_per_seq, 0)),
                  pl.BlockSpec((tm, 128), lambda i: (i % tiles_per_seq, 0))],
        out_specs=out_specs,
        out_shape=out_shape,
        scratch_shapes=[pltpu.VMEM((GROUP_WIDTH // 128, tm, 128), F32)],
        compiler_params=_cparams("arbitrary"),
        name="inproj_rope_prompt",
    )(x, w_in, cos, sin)


def _inproj(x, w_in, cos, sin, *, tm):
    n, d = x.shape
    out_shape = [jax.ShapeDtypeStruct((n, ATT_WIDTH), F32)] * 3 + [
        jax.ShapeDtypeStruct((n, w_in.shape[1] - 3 * ATT_WIDTH), F32)]
    row_spec = lambda w: pl.BlockSpec((tm, w), lambda i: (i, 0))
    out_specs = [row_spec(ATT_WIDTH)] * 3 + [row_spec(w_in.shape[1] - 3 * ATT_WIDTH)]
    tiles_per_seq = cos.shape[0] // tm
    return pl.pallas_call(
        _inproj_kernel,
        grid=(n // tm,),
        in_specs=[row_spec(d),
                  pl.BlockSpec(w_in.shape, lambda i: (0, 0)),
                  pl.BlockSpec((tm, 128), lambda i: (i % tiles_per_seq, 0)),
                  pl.BlockSpec((tm, 128), lambda i: (i % tiles_per_seq, 0))],
        out_specs=out_specs,
        out_shape=out_shape,
        compiler_params=_cparams("arbitrary"),
        name="inproj_rope",
    )(x, w_in, cos, sin)


def _win_attn_kernel(q_ref, kp_ref, kc_ref, vp_ref, vc_ref, o_ref, lse_ref):
    j = pl.program_id(2)
    tq = q_ref.shape[1]
    sub = N_KEYS - 1
    assert kp_ref.shape[1] == sub and tq % sub == 0
    k_all = jnp.concatenate([kp_ref[0], kc_ref[0]], axis=0)
    v_all = jnp.concatenate([vp_ref[0], vc_ref[0]], axis=0)
    n_h = HEADS_PER_GROUP
    row = lax.broadcasted_iota(jnp.int32, (n_h * sub, 2 * sub), 0) % sub
    col = lax.broadcasted_iota(jnp.int32, (n_h * sub, 2 * sub), 1)
    band = (col >= row) & (col <= row + sub)
    head_rows = lax.broadcasted_iota(jnp.int32, (n_h * sub, GROUP_WIDTH), 0) // sub
    head_lanes = lax.broadcasted_iota(jnp.int32, (n_h * sub, GROUP_WIDTH), 1) // HEAD_DIM
    own = head_rows == head_lanes
    lane_l = lax.broadcasted_iota(jnp.int32, (sub, 128), 1)
    for i in range(tq // sub):
        q = q_ref[0, i * sub:(i + 1) * sub, :]
        kk = k_all[i * sub:(i + 2) * sub]
        vv = v_all[i * sub:(i + 2) * sub]
        valid = band & ((j > 0) | (col >= sub)) if i == 0 else band
        q4 = jnp.concatenate([q] * n_h, axis=0)
        q4 = jnp.where(own, q4, jnp.zeros_like(q4))
        s = jnp.where(valid, _dot_nt(q4, kk), NEG)
        m = jnp.max(s, axis=1, keepdims=True)
        p = jnp.exp(s - m)
        l = jnp.sum(p, axis=1, keepdims=True)
        o4 = jnp.where(own, _dot(p.astype(BF16), vv) / l, 0.0)
        lse4 = m + jnp.log(l)
        o_acc = o4[0:sub]
        lse_acc = jnp.zeros((sub, 128), F32)
        for h in range(n_h):
            if h:
                o_acc = o_acc + o4[h * sub:(h + 1) * sub]
            lse_acc = jnp.where(lane_l // 32 == h, lse4[h * sub:(h + 1) * sub], lse_acc)
        o_ref[0, i * sub:(i + 1) * sub, :] = o_acc.astype(o_ref.dtype)
        lse_ref[0, i * sub:(i + 1) * sub, :] = lse_acc


def _win_attn(q, k, v, *, tq=512):
    b, dil, ts, _ = q.shape
    tq = min(tq, ts)
    sub = N_KEYS - 1
    cur = pl.BlockSpec((None, 1, tq, GROUP_WIDTH), lambda bi, r, j: (bi, r, j, 0))
    prev = pl.BlockSpec((None, 1, sub, GROUP_WIDTH),
                        lambda bi, r, j: (bi, r, jnp.maximum(j * (tq // sub) - 1, 0), 0))
    return pl.pallas_call(
        _win_attn_kernel,
        grid=(b, dil, ts // tq),
        in_specs=[cur, prev, cur, prev, cur],
        out_specs=[pl.BlockSpec((None, 1, tq, GROUP_WIDTH), lambda bi, r, j: (bi, r, j, 0)),
                   pl.BlockSpec((None, 1, tq, 128), lambda bi, r, j: (bi, r, j, 0))],
        out_shape=[jax.ShapeDtypeStruct((b, dil, ts, GROUP_WIDTH), BF16),
                   jax.ShapeDtypeStruct((b, dil, ts, 128), F32)],
        compiler_params=_cparams("arbitrary", "arbitrary", "arbitrary"),
        name="window_attention",
    )(q, k, k, v, v)


def _dec_attn_kernel(q_ref, k_ref, v_ref, c0_ref, c1_ref, c2_ref, att_ref, o0_ref, o1_ref, o2_ref,
                     e0_ref, e1_ref, e2_ref):
    t_new = q_ref.shape[1]
    n_h = HEADS_PER_GROUP
    q = q_ref[0]
    k_new = k_ref[0]
    v_new = v_ref[0]
    n_q = n_h * t_new
    own = (lax.broadcasted_iota(jnp.int32, (n_q, GROUP_WIDTH), 0) // t_new
           == lax.broadcasted_iota(jnp.int32, (n_q, GROUP_WIDTH), 1) // HEAD_DIM)
    outs, lses = [], []
    for g, ((window, dil), c_ref, o_ref, e_ref) in enumerate(
            zip(ATT_GROUPS, (c0_ref, c1_ref, c2_ref), (o0_ref, o1_ref, o2_ref), (e0_ref, e1_ref, e2_ref))):
        w = c_ref.shape[1]
        cols = slice(g * GROUP_WIDTH, (g + 1) * GROUP_WIDTH)
        kv_new = jnp.concatenate([k_new[:, cols], v_new[:, cols]], axis=1)
        n_chunk = e_ref.shape[0]
        pad_rows = e_ref.shape[1] - (w + t_new)
        if pad_rows:
            @pl.when(pl.program_id(0) == 0)
            def _(e_ref=e_ref, w=w, pad_rows=pad_rows):
                e_ref[:, w + t_new:, :] = jnp.zeros((n_chunk, pad_rows, 128), F32)
        for c in range(n_chunk):
            e_ref[c, 0:w, :] = c_ref[0, :, c * 128:(c + 1) * 128]
            e_ref[c, w:w + t_new, :] = kv_new[:, c * 128:(c + 1) * 128]
            o_ref[0, :, c * 128:(c + 1) * 128] = e_ref[c, t_new:w + t_new, :]
        classes = min(dil, t_new)
        parts = [jnp.concatenate([e_ref[c, pl.ds(r, DEC_SPAN, stride=dil), :] for c in range(n_chunk)], axis=1)
                 for r in range(classes)]
        kv_all = jnp.concatenate(parts, axis=0) if classes > 1 else parts[0]
        k_all = kv_all[:, 0:GROUP_WIDTH].astype(BF16)
        v_all = kv_all[:, GROUP_WIDTH:].astype(BF16)
        q4 = jnp.where(own, jnp.concatenate([q[:, cols]] * n_h, axis=0), 0.0)
        s = _dot_nt(q4.astype(BF16), k_all)
        t_idx = lax.broadcasted_iota(jnp.int32, s.shape, 0) % t_new
        col = lax.broadcasted_iota(jnp.int32, s.shape, 1)
        first = t_idx // dil
        step_i = col % DEC_SPAN
        valid = (col // DEC_SPAN == t_idx % dil) & (step_i >= first) & (step_i <= first + (N_KEYS - 1))
        s = jnp.where(valid, s, NEG)
        m = jnp.max(s, axis=1, keepdims=True)
        p = jnp.exp(s - m)
        l = jnp.sum(p, axis=1, keepdims=True)
        outs.append(_dot(p.astype(BF16), v_all) / l)
        lses.append(m + jnp.log(l))
    top = jnp.maximum(jnp.maximum(lses[0], lses[1]), lses[2])
    es = [jnp.exp(l - top) for l in lses]
    tot = es[0] + es[1] + es[2]
    for g in range(len(ATT_GROUPS)):
        weighted = jnp.where(own, outs[g] * (es[g] / tot), 0.0)
        acc = weighted[0:t_new]
        for h in range(1, n_h):
            acc = acc + weighted[h * t_new:(h + 1) * t_new]
        att_ref[0, :, g * GROUP_WIDTH:(g + 1) * GROUP_WIDTH] = acc


DEC_SPAN = 136


def _dec_attn(q, k, v, caches):
    b, t_new, _ = q.shape
    assert N_KEYS + t_new - 1 <= DEC_SPAN
    tok = pl.BlockSpec((1, t_new, ATT_WIDTH), lambda i: (i, 0, 0))
    cspec = [pl.BlockSpec((1,) + c.shape[1:], lambda i: (i, 0, 0)) for c in caches]

    def ext_rows(c, dil):
        return max(c.shape[1] + t_new, (min(dil, t_new) - 1) + (DEC_SPAN - 1) * dil + 1)

    return pl.pallas_call(
        _dec_attn_kernel,
        grid=(b,),
        in_specs=[tok, tok, tok] + cspec,
        out_specs=[tok] + cspec,
        out_shape=[jax.ShapeDtypeStruct(q.shape, F32)] + [jax.ShapeDtypeStruct(c.shape, F32) for c in caches],
        scratch_shapes=[pltpu.VMEM((c.shape[2] // 128, ext_rows(c, dil), 128), F32)
                        for c, (_, dil) in zip(caches, ATT_GROUPS)],
        compiler_params=_cparams("arbitrary"),
        name="decode_attention",
    )(q, k, v, *caches)


def _mlstm_kernel(m3_ref, conv0_ref, c0_ref, n0_ref, m0_ref, *rest):
    consts, (hm_ref, convo_ref, co_ref, no_ref, mo_ref, cbuf, c_s, n_s, m_s) = rest[:10], rest[10:]
    for b in range(m3_ref.shape[0]):
        _mlstm_chunk(m3_ref.at[b], conv0_ref.at[b], c0_ref.at[b], n0_ref.at[b], m0_ref.at[b], *consts,
                     hm_ref.at[b], convo_ref.at[b], co_ref.at[b], no_ref.at[b], mo_ref.at[b],
                     cbuf.at[b], c_s.at[b], n_s.at[b], m_s.at[b])


def _mlstm_chunk(m3_ref, conv0_ref, c0_ref, n0_ref, m0_ref, convw_ref, convb_ref, wq_ref, wk_ref,
                 wif_ref, wift_ref, bif_ref, bift_ref, normw_ref, skip_ref,
                 hm_ref, convo_ref, co_ref, no_ref, mo_ref, cbuf, c_s, n_s, m_s):
    j = pl.program_id(1)
    L = m3_ref.shape[0]
    W = MLSTM_WIDTH
    D = HEAD_DIM

    @pl.when(j == 0)
    def _():
        cbuf[0:8, :] = jnp.zeros((8, W), F32)
        cbuf[8 - (CONV_WIDTH - 1):8, :] = conv0_ref[...]
        c_s[...] = c0_ref[...]
        n_s[...] = n0_ref[...]
        m_s[...] = m0_ref[...]

    blk = m3_ref[...]
    c_in = blk[:, 0:W]
    v_m = blk[:, W:2 * W]
    z = blk[:, 2 * W:3 * W]
    cbuf[8:8 + L, :] = c_in
    acc = jnp.zeros((L, W), F32) + convb_ref[...]
    for tap in range(CONV_WIDTH):
        off = 8 - (CONV_WIDTH - 1) + tap
        acc = acc + cbuf[off:off + L, :] * convw_ref[tap:tap + 1, :]
    xc = _silu(acc)
    convo_ref[...] = cbuf[8 + L - (CONV_WIDTH - 1):8 + L, :]
    cbuf[0:8, :] = cbuf[L:L + 8, :]

    xcb = xc.astype(BF16)
    q_m = _dot(xcb, wq_ref[...])
    k_m = _dot(xcb, wk_ref[...])
    gate_in = jnp.concatenate([q_m, k_m, v_m], axis=1)
    nn, nt = ((1,), (0,)), ((1,), (1,))
    g_col = _dot_split(gate_in, wif_ref[...], nn) + bif_ref[...]
    g_row = _dot_split(wift_ref[...], gate_in, nt) + bift_ref[...]
    i_col, lf_col = g_col[:, 0:MLSTM_HEADS], _log_sigmoid(g_col[:, MLSTM_HEADS:])
    i_row, lf_row = g_row[0:MLSTM_HEADS, :], _log_sigmoid(g_row[MLSTM_HEADS:, :])
    rr = lax.broadcasted_iota(jnp.int32, (L, L), 0)
    cc = lax.broadcasted_iota(jnp.int32, (L, L), 1)
    causal = cc <= rr
    tri = causal.astype(F32)
    b_col = _dot_split(tri, lf_col, nn, a_exact=True)
    b_row = _dot_split(lf_row, tri, nt, b_exact=True)
    ks = k_m * (D ** -0.5)
    kb = ks.astype(BF16)
    vb = v_m.astype(BF16)
    n_h = MLSTM_HEADS
    heads = range(n_h)
    own = (lax.broadcasted_iota(jnp.int32, (n_h * L, W), 0) // L
           == lax.broadcasted_iota(jnp.int32, (n_h * L, W), 1) // D)
    q4 = jnp.where(own, jnp.concatenate([q_m] * n_h, axis=0), 0.0)
    q4b = q4.astype(BF16)
    m_prev = m_s[...]
    per_row = lambda a: jnp.concatenate([a[:, h:h + 1] for h in heads], axis=0)
    per_key = lambda a: jnp.concatenate([jnp.broadcast_to(a[h:h + 1, :], (L, L)) for h in heads], axis=0)
    bc = per_row(b_col)
    m_prev_r = jnp.concatenate([jnp.broadcast_to(m_prev[:, h:h + 1], (L, 1)) for h in heads], axis=0)
    causal4 = (lax.broadcasted_iota(jnp.int32, (n_h * L, L), 1)
               <= lax.broadcasted_iota(jnp.int32, (n_h * L, L), 0) % L)
    d_intra = jnp.where(causal4, bc - per_key(b_row) + per_key(i_row), -jnp.inf)
    a_inter = bc + m_prev_r
    m_t = jnp.maximum(a_inter, jnp.max(d_intra, axis=1, keepdims=True))
    s = _dot_nt(q4b, kb) * jnp.exp(d_intra - m_t)
    w_inter = jnp.exp(a_inter - m_t)
    c_all = c_s[...]
    c_bd = jnp.where(lax.broadcasted_iota(jnp.int32, (W, W), 0) // D == lax.broadcasted_iota(jnp.int32, (W, W), 1) // D,
                     jnp.concatenate([c_all] * n_h, axis=1), 0.0)
    n_all = n_s[...]
    n_flat = jnp.concatenate([n_all[h:h + 1, :] for h in heads], axis=1)
    num = _dot(s.astype(BF16), vb) + w_inter * _dot_nt(q4b, c_bd.astype(BF16))
    den = jnp.sum(s, axis=1, keepdims=True) + w_inter * jnp.sum(q4 * n_flat, axis=1, keepdims=True)
    hh = jnp.where(own, num / jnp.maximum(jnp.abs(den), jnp.exp(-m_t)), 0.0)
    mu = jnp.sum(hh, axis=1, keepdims=True) * (1.0 / D)
    hc = jnp.where(own, hh - mu, 0.0)
    var = jnp.sum(hc * hc, axis=1, keepdims=True) * (1.0 / D)
    hn = hc * lax.rsqrt(var + LN_EPS)
    h_all = hn[0:L]
    for h in range(1, n_h):
        h_all = h_all + hn[h * L:(h + 1) * L]
    g_tot = b_col[L - 1:L, :]
    a_end = g_tot + m_prev
    d_end = g_tot - b_col + i_col
    m_new = jnp.maximum(a_end, jnp.max(d_end, axis=0, keepdims=True))
    w_s = jnp.exp(d_end - m_new)
    decay = jnp.exp(a_end - m_new)
    w_s_l = jnp.concatenate([jnp.broadcast_to(w_s[:, h:h + 1], (L, D)) for h in heads], axis=1)
    upd = _dot_tn((v_m * w_s_l).astype(BF16), kb)
    decay_r = jnp.concatenate([jnp.broadcast_to(decay[:, h:h + 1], (D, 1)) for h in heads], axis=0)
    c_s[...] = decay_r * c_all + jnp.concatenate([upd[h * D:(h + 1) * D, h * D:(h + 1) * D] for h in heads], axis=0)
    k_sum = jnp.sum(w_s_l * ks, axis=0, keepdims=True)
    n_s[...] = (jnp.concatenate([jnp.broadcast_to(decay[:, h:h + 1], (1, D)) for h in heads], axis=0) * n_all
                + jnp.concatenate([k_sum[:, h * D:(h + 1) * D] for h in heads], axis=0))
    m_s[...] = m_new
    hm = (h_all * normw_ref[...] + skip_ref[...] * xc) * _silu(z)
    hm_ref[...] = hm.astype(hm_ref.dtype)
    co_ref[...] = c_s[...]
    no_ref[...] = n_s[...]
    mo_ref[...] = m_s[...]


def _mlstm(m3, conv0, c0, n0, m0, conv_w, conv_b, wq_bd, wk_bd, w_if, b_if, norm_w, skip, *, chunk, bb):
    b, t, _ = m3.shape
    W = MLSTM_WIDTH
    full = lambda a: pl.BlockSpec(a.shape, lambda bi, j: (0,) * a.ndim)
    per_b = lambda a: pl.BlockSpec((bb,) + a.shape[1:], lambda bi, j: (bi,) + (0,) * (a.ndim - 1))
    consts = [conv_w, conv_b.reshape(1, W), wq_bd, wk_bd, w_if, w_if.T, b_if.reshape(1, -1),
              b_if.reshape(-1, 1), norm_w.reshape(1, W), skip.reshape(1, W)]
    states = [conv0, c0, n0, m0]
    return pl.pallas_call(
        _mlstm_kernel,
        grid=(b // bb, t // chunk),
        in_specs=[pl.BlockSpec((bb, chunk, 3 * W), lambda bi, j: (bi, j, 0))] + [per_b(s) for s in states]
                 + [full(c) for c in consts],
        out_specs=[pl.BlockSpec((bb, chunk, W), lambda bi, j: (bi, j, 0))] + [per_b(s) for s in states],
        out_shape=[jax.ShapeDtypeStruct((b, t, W), BF16)] + [jax.ShapeDtypeStruct(s.shape, F32) for s in states],
        scratch_shapes=[pltpu.VMEM((bb, chunk + 8, W), F32), pltpu.VMEM((bb,) + c0.shape[1:], F32),
                        pltpu.VMEM((bb,) + n0.shape[1:], F32), pltpu.VMEM((bb,) + m0.shape[1:], F32)],
        compiler_params=_cparams("arbitrary", "arbitrary"),
        name="mlstm",
    )(m3, *states, *consts)


def _interleave(ref, scr):
    dil, rows, width = ref.shape
    if dil == 1:
        return ref[0].astype(F32)
    for r in range(dil):
        for c in range(width // 128):
            scr[c, pl.ds(r, rows, stride=dil), :] = ref[r, :, c * 128:(c + 1) * 128].astype(F32)
    return jnp.concatenate([scr[c] for c in range(width // 128)], axis=1)


def _combine(os, lses):
    tm = os[0].shape[0]
    lane = lax.broadcasted_iota(jnp.int32, (tm, GROUP_WIDTH), 1)

    def spread(l2):
        out = jnp.zeros((tm, GROUP_WIDTH), F32)
        for h in range(HEADS_PER_GROUP):
            out = jnp.where(lane // HEAD_DIM == h, l2[:, 32 * h:32 * h + 1], out)
        return out

    ls = [spread(l) for l in lses]
    top = jnp.maximum(jnp.maximum(ls[0], ls[1]), ls[2])
    es = [jnp.exp(l - top) for l in ls]
    tot = es[0] + es[1] + es[2]
    return [(o * (e / tot)).astype(BF16) for o, e in zip(os, es)]


def _outproj_kernel(combine, alpha, *refs):
    if combine:
        (o0, o1, o2, l0, l1, l2, hm_ref, x_ref, wo_ref, g_ref, b_ref, wq_ref, x1_ref, qc_ref, scr) = refs
        att = _combine([_interleave(o, scr) for o in (o0, o1, o2)], [_interleave(l, scr) for l in (l0, l1, l2)])
    else:
        (a_ref, hm_ref, x_ref, wo_ref, g_ref, b_ref, wq_ref, x1_ref, qc_ref) = refs
        att = [a_ref[:, g * GROUP_WIDTH:(g + 1) * GROUP_WIDTH].astype(BF16) for g in range(len(ATT_GROUPS))]
    mix = _dot(hm_ref[...].astype(BF16), wo_ref[ATT_WIDTH:, :])
    for g, a in enumerate(att):
        mix = mix + _dot(a, wo_ref[g * GROUP_WIDTH:(g + 1) * GROUP_WIDTH, :])
    x1 = _layer_norm(alpha * x_ref[...] + mix, g_ref[...], b_ref[...])
    x1_ref[...] = x1
    qc_ref[...] = (_dot(x1.astype(BF16), wq_ref[...]) * ((x1.shape[1] // MEM_HEADS) ** -0.5)).astype(qc_ref.dtype)


def _outproj(att_parts, hm, x, w_out, ln_g, ln_b, w_cq, alpha, *, tm):
    n, d = x.shape
    combine = len(att_parts) > 1
    row = lambda a: pl.BlockSpec((tm, a.shape[1]), lambda i: (i, 0))
    full = lambda a: pl.BlockSpec(a.shape, lambda i: (0, 0))

    def split(a):
        _, dil, ts, w = a.shape
        tiles_per_seq = ts * dil // tm
        return pl.BlockSpec((None, dil, tm // dil, w), lambda i: (i // tiles_per_seq, 0, i % tiles_per_seq, 0))

    consts = [w_out, ln_g.reshape(1, d), ln_b.reshape(1, d), w_cq]
    return pl.pallas_call(
        functools.partial(_outproj_kernel, combine, alpha),
        grid=(n // tm,),
        in_specs=[split(a) if combine else row(a) for a in att_parts] + [row(hm), row(x)] + [full(c) for c in consts],
        out_specs=[pl.BlockSpec((tm, d), lambda i: (i, 0))] * 2,
        out_shape=[jax.ShapeDtypeStruct((n, d), F32), jax.ShapeDtypeStruct((n, d), BF16)],
        scratch_shapes=[pltpu.VMEM((GROUP_WIDTH // 128, tm, 128), F32)] if combine else [],
        compiler_params=_cparams("arbitrary"),
        name="outproj_ln1",
    )(*att_parts, hm, x, *consts)


def _memkv_kernel(mem_ref, wk_ref, wv_ref, kv_ref):
    d = mem_ref.shape[2]
    mem = mem_ref[0].astype(BF16)
    kv_ref[0, :, 0:d] = _dot(mem, wk_ref[...])
    kv_ref[0, :, d:] = _dot(mem, wv_ref[...])


def _memkv(mem, w_ck, w_cv):
    b, m, d = mem.shape
    return pl.pallas_call(
        _memkv_kernel,
        grid=(b,),
        in_specs=[pl.BlockSpec((1, m, d), lambda i: (i, 0, 0)),
                  pl.BlockSpec(w_ck.shape, lambda i: (0, 0)), pl.BlockSpec(w_cv.shape, lambda i: (0, 0))],
        out_specs=pl.BlockSpec((1, m, 2 * d), lambda i: (i, 0, 0)),
        out_shape=jax.ShapeDtypeStruct((b, m, 2 * d), F32),
        compiler_params=_cparams("arbitrary"),
        name="memory_kv",
    )(mem, w_ck, w_cv)


def _xattn_kernel(q_ref, kv_ref, o_ref):
    d = q_ref.shape[2]
    hd = d // MEM_HEADS
    q = q_ref[0]
    for h in range(MEM_HEADS):
        k = kv_ref[0, :, h * hd:(h + 1) * hd].astype(BF16)
        v = kv_ref[0, :, d + h * hd:d + (h + 1) * hd].astype(BF16)
        s = _dot_nt(q[:, h * hd:(h + 1) * hd], k)
        p = jnp.exp(s - jnp.max(s, axis=1, keepdims=True))
        p = p / jnp.sum(p, axis=1, keepdims=True)
        o_ref[0, :, h * hd:(h + 1) * hd] = _dot(p.astype(BF16), v).astype(o_ref.dtype)


def _xattn(qc, kv, *, tq):
    b, t, d = qc.shape
    m = kv.shape[1]
    return pl.pallas_call(
        _xattn_kernel,
        grid=(b, t // tq),
        in_specs=[pl.BlockSpec((1, tq, d), lambda bi, j: (bi, j, 0)),
                  pl.BlockSpec((1, m, 2 * d), lambda bi, j: (bi, 0, 0))],
        out_specs=pl.BlockSpec((1, tq, d), lambda bi, j: (bi, j, 0)),
        out_shape=jax.ShapeDtypeStruct((b, t, d), BF16),
        compiler_params=_cparams("arbitrary", "arbitrary"),
        name="cross_attention",
    )(qc, kv)


def _xout_kernel(alpha, o_ref, x_ref, w_ref, g_ref, b_ref, y_ref):
    y_ref[...] = _layer_norm(alpha * x_ref[...] + _dot(o_ref[...], w_ref[...]), g_ref[...], b_ref[...])


def _xout(oc, x1, w_co, ln_g, ln_b, alpha, *, tm):
    n, d = x1.shape
    row = pl.BlockSpec((tm, d), lambda i: (i, 0))
    full = lambda a: pl.BlockSpec(a.shape, lambda i: (0, 0))
    consts = [w_co, ln_g.reshape(1, d), ln_b.reshape(1, d)]
    return pl.pallas_call(
        functools.partial(_xout_kernel, alpha),
        grid=(n // tm,),
        in_specs=[row, row] + [full(c) for c in consts],
        out_specs=row,
        out_shape=jax.ShapeDtypeStruct((n, d), F32),
        compiler_params=_cparams("arbitrary"),
        name="cross_out_ln2",
    )(oc, x1, *consts)


def _first_index_of_max(vals, idx, big):
    mx = jnp.max(vals, axis=0, keepdims=True)
    return mx, jnp.min(jnp.where(vals == mx, idx, big), axis=0, keepdims=True)


def _route(xb, wrt_ref, br_ref):
    tm = xb.shape[0]
    n_e = wrt_ref.shape[0]
    per_group = n_e // N_EXPERT_GROUPS
    logits = _dot_nt(wrt_ref[...], xb)
    scores = _sigmoid(logits)
    biased = scores + br_ref[...]
    e_idx = lax.broadcasted_iota(jnp.int32, (n_e, tm), 0).astype(F32)
    g_scores = []
    for g in range(N_EXPERT_GROUPS):
        sub = biased[g * per_group:(g + 1) * per_group, :]
        sidx = lax.broadcasted_iota(jnp.int32, (per_group, tm), 0).astype(F32)
        m1, a1 = _first_index_of_max(sub, sidx, per_group)
        m2 = jnp.max(jnp.where(sidx == a1, -jnp.inf, sub), axis=0, keepdims=True)
        g_scores.append(m1 + m2)
    gs = jnp.concatenate(g_scores, axis=0)
    g_idx = lax.broadcasted_iota(jnp.int32, (N_EXPERT_GROUPS, tm), 0).astype(F32)
    g_sel = jnp.zeros((N_EXPERT_GROUPS, tm), F32)
    work = gs
    for _ in range(TOPK_GROUPS):
        _, a = _first_index_of_max(work, g_idx, N_EXPERT_GROUPS)
        hit = g_idx == a
        g_sel = jnp.where(hit, 1.0, g_sel)
        work = jnp.where(hit, -jnp.inf, work)
    e_mask = jnp.concatenate(
        [jnp.broadcast_to(g_sel[g:g + 1, :], (per_group, tm)) for g in range(N_EXPERT_GROUPS)], axis=0)
    work = jnp.where(e_mask > 0.5, biased, -jnp.inf)
    sel = jnp.zeros((n_e, tm), F32)
    picks = []
    for _ in range(TOP_K):
        cand = jnp.where(sel > 0.5, -jnp.inf, work)
        mx = jnp.max(cand, axis=0, keepdims=True)
        a = jnp.min(jnp.where((cand == mx) & (sel < 0.5), e_idx, float(n_e)), axis=0, keepdims=True)
        pick = jnp.where(e_idx == a, 1.0, 0.0)
        picks.append(pick)
        sel = sel + pick
    w_sel = sel * scores
    gates_t = w_sel / jnp.sum(w_sel, axis=0, keepdims=True) * ROUTED_SCALE
    return gates_t, sel, picks


ROW_CHUNK = 16
TILE_CHUNKS = 32
TILE_ROWS = ROW_CHUNK * TILE_CHUNKS
TOKEN_BLOCK = 256
GATHER_AHEAD = 3
TILE_SLOTS = GATHER_AHEAD + 1
MAX_SEGMENT_PIECES = TOKEN_BLOCK // ROW_CHUNK
assert MAX_SEGMENT_PIECES <= TILE_SLOTS * TILE_CHUNKS


def _slab_rows(tb, n_e):
    return -(-(TOP_K * tb + n_e * (ROW_CHUNK - 1)) // 128) * 128


def _block_of(refs, first_blocks):
    i = pl.program_id(0)
    x = refs[0][...]
    for ref, first in zip(refs[1:], first_blocks[1:]):
        x = jnp.where(i >= first, ref[...], x)
    return x


def _multi_specs(arrays, tb):
    firsts, specs, start = [], [], 0
    for a in arrays:
        nb = a.shape[0] // tb
        firsts.append(start)
        specs.append(pl.BlockSpec((tb, a.shape[1]),
                                  lambda i, start=start, nb=nb: (jnp.clip(i - start, 0, nb - 1), 0)))
        start += nb
    return firsts, specs, start


def _dispatch_kernel(first_blocks, *refs):
    n_x = len(first_blocks)
    x_refs, (wrt_ref, br_ref, xs_ref, pwt_ref, meta_ref) = refs[:n_x], refs[n_x:]
    tb = x_refs[0].shape[0]
    n_e = wrt_ref.shape[0]
    slab = xs_ref.shape[1]
    xb = _block_of(x_refs, first_blocks).astype(BF16)
    gates_t, sel, picks = _route(xb, wrt_ref, br_ref)
    cnt = jnp.sum(sel, axis=1, keepdims=True)
    padded = jnp.floor((cnt + (ROW_CHUNK - 1)) * (1.0 / ROW_CHUNK)) * ROW_CHUNK
    padded_b = jnp.broadcast_to(padded, (n_e, 128))
    er = lax.broadcasted_iota(jnp.int32, (n_e, n_e), 0)
    ec = lax.broadcasted_iota(jnp.int32, (n_e, n_e), 1)
    off_b = _dot((ec < er).astype(BF16), padded_b.astype(BF16))
    tr = lax.broadcasted_iota(jnp.int32, (tb, tb), 0)
    tc = lax.broadcasted_iota(jnp.int32, (tb, tb), 1)
    rank = _dot(sel.astype(BF16), (tr < tc).astype(BF16))
    dest = off_b[:, 0:1] + rank
    dest_k = [jnp.sum(p * dest, axis=0, keepdims=True) for p in picks]
    gate_k = [jnp.sum(p * gates_t, axis=0, keepdims=True) for p in picks]
    meta_ref[0, 0] = padded_b
    meta_ref[0, 1] = off_b

    stacked = jnp.concatenate(dest_k + [jnp.zeros((8 - TOP_K, tb), F32)] + gate_k
                              + [jnp.zeros((128 - 8 - TOP_K, tb), F32)], axis=0)
    cols = jnp.transpose(stacked)
    step = 512
    for c0 in range(0, slab, step):
        lanes = lax.broadcasted_iota(jnp.int32, (tb, step), 1).astype(F32) + float(c0)
        w = jnp.zeros((tb, step), F32)
        for k in range(TOP_K):
            w = jnp.where(lanes == cols[:, k:k + 1], cols[:, 8 + k:9 + k], w)
        pwt_ref[0, :, c0:c0 + step] = w.astype(pwt_ref.dtype)
        onehot_t = jnp.where(w != 0.0, 1.0, 0.0).astype(BF16)
        xs_ref[0, c0:c0 + step, :] = _dot_tn(onehot_t, xb).astype(xs_ref.dtype)


def _dispatch(xs_in, w_router_t, b_router):
    d = xs_in[0].shape[1]
    n_e = w_router_t.shape[0]
    tb = TOKEN_BLOCK
    firsts, x_specs, nblk = _multi_specs(xs_in, tb)
    slab = _slab_rows(tb, n_e)
    assert slab % 512 == 0
    return pl.pallas_call(
        functools.partial(_dispatch_kernel, tuple(firsts)),
        grid=(nblk,),
        in_specs=x_specs + [pl.BlockSpec(w_router_t.shape, lambda i: (0, 0)),
                            pl.BlockSpec((n_e, 1), lambda i: (0, 0))],
        out_specs=[pl.BlockSpec((1, slab, d), lambda i: (i, 0, 0)),
                   pl.BlockSpec((1, tb, slab), lambda i: (i, 0, 0)),
                   pl.BlockSpec((1, 2, n_e, 128), lambda i: (i, 0, 0, 0))],
        out_shape=[jax.ShapeDtypeStruct((nblk, slab, d), BF16),
                   jax.ShapeDtypeStruct((nblk, tb, slab), BF16),
                   jax.ShapeDtypeStruct((nblk, 2, n_e, 128), F32)],
        compiler_params=_cparams("arbitrary"),
        name="moe_dispatch",
    )(*xs_in, w_router_t, b_router.reshape(n_e, 1))


def _schedule_kernel(n_e, nblk, t_max, zero_row, slot_ref, first_ref, total_ref, tile0_ref,
                     te_ref, nv_ref, row_ref, nt_ref):
    def per_expert(e, carry):
        def per_blocks(bu, c):
            for u in range(unroll):
                idx = e * nblk + bu * unroll + u
                p = slot_ref[idx]
                r0 = first_ref[idx]
                for j in range(MAX_SEGMENT_PIECES):
                    row_ref[p + j] = r0 + j * ROW_CHUNK
            return c
        unroll = max(u for u in range(1, 9) if nblk % u == 0)
        lax.fori_loop(0, nblk // unroll, per_blocks, 0)

        cnt = total_ref[e]
        tile = tile0_ref[e]
        n_t = (cnt + (TILE_CHUNKS - 1)) // TILE_CHUNKS
        start = tile * TILE_CHUNKS

        def pad(p, c):
            row_ref[p] = zero_row
            return c
        lax.fori_loop(start + cnt, start + n_t * TILE_CHUNKS, pad, 0)

        def per_tile(i, c):
            te_ref[tile + i] = e
            nv_ref[tile + i] = jnp.minimum(TILE_CHUNKS, cnt - i * TILE_CHUNKS)
            return c
        lax.fori_loop(0, n_t, per_tile, 0)
        return carry
    lax.fori_loop(0, n_e, per_expert, 0)

    last = n_e - 1
    tile = tile0_ref[last] + (total_ref[last] + (TILE_CHUNKS - 1)) // TILE_CHUNKS
    nt_ref[0] = tile
    last_e = te_ref[jnp.maximum(tile - 1, 0)]

    def idle_tile(t, c):
        te_ref[t] = last_e
        nv_ref[t] = 0
        return c
    lax.fori_loop(tile, t_max, idle_tile, 0)

    def idle_piece(p, c):
        row_ref[p] = zero_row
        return c
    lax.fori_loop(tile * TILE_CHUNKS, t_max * TILE_CHUNKS, idle_piece, 0)


def _tile_schedule(meta, slab):
    nblk, _, n_e, _ = meta.shape
    pieces = (meta[:, 0, :, 0].astype(jnp.int32) // ROW_CHUNK).T
    first_row = (meta[:, 1, :, 0].astype(jnp.int32) + (jnp.arange(nblk, dtype=jnp.int32) * slab)[:, None]).T
    before = jnp.cumsum(pieces, axis=1) - pieces
    total = jnp.sum(pieces, axis=1)
    tiles = (total + (TILE_CHUNKS - 1)) // TILE_CHUNKS
    tile0 = jnp.cumsum(tiles) - tiles
    slot = tile0[:, None] * TILE_CHUNKS + before
    t_max = -(-(TOP_K * nblk * TOKEN_BLOCK + nblk * n_e * (ROW_CHUNK - 1)) // TILE_ROWS) + n_e + TILE_SLOTS
    assert slab - ROW_CHUNK >= TOP_K * TOKEN_BLOCK + n_e * (ROW_CHUNK - 1)
    smem = pl.BlockSpec(memory_space=pltpu.SMEM)
    i32 = lambda n: jax.ShapeDtypeStruct((n,), jnp.int32)
    return pl.pallas_call(
        functools.partial(_schedule_kernel, n_e, nblk, t_max, slab - ROW_CHUNK),
        in_specs=[smem] * 4,
        out_specs=[smem, smem, smem, smem],
        out_shape=[i32(t_max), i32(t_max), i32(t_max * TILE_CHUNKS), i32(1)],
        name="moe_schedule",
    )(slot.reshape(-1), first_row.reshape(-1), total, tile0)


def _grouped_kernel(te_ref, nv_ref, row_ref, nt_ref, xs_hbm, wg_ref, wu_ref, wd_ref, ys_hbm,
                    lhs, obuf, wg_b, wu_b, wd_b, sem_in, sem_out):
    t = pl.program_id(0)
    n_tiles = nt_ref[0]

    def piece(ref, tile, s):
        row = pl.multiple_of(row_ref[tile * TILE_CHUNKS + s], ROW_CHUNK)
        return ref.at[pl.ds(row, ROW_CHUNK)]

    def local(buf, slot, s):
        start = s * ROW_CHUNK if isinstance(s, int) else pl.multiple_of(s * ROW_CHUNK, ROW_CHUNK)
        return buf.at[slot, pl.ds(start, ROW_CHUNK)]

    def gather(tile, s):
        slot = tile % TILE_SLOTS
        return pltpu.make_async_copy(piece(xs_hbm, tile, s), local(lhs, slot, s), sem_in.at[slot])

    def scatter(tile, s):
        slot = tile % TILE_SLOTS
        return pltpu.make_async_copy(local(obuf, slot, s), piece(ys_hbm, tile, s), sem_out.at[slot])

    def gathered_tile(slot):
        return pltpu.make_async_copy(xs_hbm.at[pl.ds(0, TILE_ROWS)], lhs.at[slot], sem_in.at[slot])

    def scattered_tile(slot):
        return pltpu.make_async_copy(obuf.at[slot], ys_hbm.at[pl.ds(0, TILE_ROWS)], sem_out.at[slot])

    def for_valid_pieces(tile, fn):
        def body(s, carry):
            fn(tile, s)
            return carry
        lax.fori_loop(0, nv_ref[tile], body, 0)

    @pl.when(t == 0)
    def _():
        for tile in range(GATHER_AHEAD):
            for s in range(TILE_CHUNKS):
                gather(tile, s).start()

    @pl.when((t >= TILE_SLOTS) & (t - TILE_SLOTS < n_tiles))
    def _():
        full = nv_ref[t - TILE_SLOTS] == TILE_CHUNKS

        @pl.when(full)
        def _():
            scattered_tile(t % TILE_SLOTS).wait()

        @pl.when(jnp.logical_not(full))
        def _():
            for_valid_pieces(t - TILE_SLOTS, lambda tile, s: scatter(tile, s).wait())

    @pl.when((t < n_tiles) & ((t == 0) | (te_ref[t] != te_ref[jnp.maximum(t - 1, 0)])))
    def _():
        wg_b[...] = wg_ref[0].astype(BF16)
        wu_b[...] = wu_ref[0].astype(BF16)
        wd_b[...] = wd_ref[0].astype(BF16)

    @pl.when(t < n_tiles)
    def _():
        slot = t % TILE_SLOTS
        gathered_tile(slot).wait()
        for s in range(TILE_CHUNKS):
            gather(t + GATHER_AHEAD, s).start()
        x = lhs[slot]
        h = _silu(_dot(x, wg_b[...])) * _dot(x, wu_b[...])
        obuf[slot] = _dot(h.astype(BF16), wd_b[...]).astype(obuf.dtype)

    @pl.when((t >= n_tiles) & (t < n_tiles + GATHER_AHEAD))
    def _():
        gathered_tile(t % TILE_SLOTS).wait()

    @pl.when(t < n_tiles)
    def _():
        full = nv_ref[t] == TILE_CHUNKS

        @pl.when(full)
        def _():
            for s in range(TILE_CHUNKS):
                scatter(t, s).start()

        @pl.when(jnp.logical_not(full))
        def _():
            for_valid_pieces(t, lambda tile, s: scatter(tile, s).start())


def _grouped_experts(xs, tile_e, n_valid, rows_tbl, n_tiles, w_gate, w_up, w_down):
    rows, d = xs.shape
    n_e, _, ff = w_gate.shape
    t_max = tile_e.shape[0]
    grid_spec = pltpu.PrefetchScalarGridSpec(
        num_scalar_prefetch=4,
        grid=(t_max,),
        in_specs=[pl.BlockSpec(memory_space=pl.ANY),
                  pl.BlockSpec((1, d, ff), lambda t, te, nv, sr, nt: (te[t], 0, 0)),
                  pl.BlockSpec((1, d, ff), lambda t, te, nv, sr, nt: (te[t], 0, 0)),
                  pl.BlockSpec((1, ff, d), lambda t, te, nv, sr, nt: (te[t], 0, 0))],
        out_specs=pl.BlockSpec(memory_space=pl.ANY),
        scratch_shapes=[pltpu.VMEM((TILE_SLOTS, TILE_ROWS, d), BF16), pltpu.VMEM((TILE_SLOTS, TILE_ROWS, d), BF16),
                        pltpu.VMEM((d, ff), BF16), pltpu.VMEM((d, ff), BF16), pltpu.VMEM((ff, d), BF16),
                        pltpu.SemaphoreType.DMA((TILE_SLOTS,)), pltpu.SemaphoreType.DMA((TILE_SLOTS,))],
    )
    return pl.pallas_call(
        _grouped_kernel,
        grid_spec=grid_spec,
        out_shape=jax.ShapeDtypeStruct((rows, d), BF16),
        input_output_aliases={4: 0},
        compiler_params=_cparams("arbitrary"),
        name="moe_grouped_experts",
    )(tile_e, n_valid, rows_tbl, n_tiles, xs, w_gate, w_up, w_down)


def _moe_out_kernel(alpha, first_blocks, pwt_ref, ys_ref, *refs):
    n_x = len(first_blocks)
    x_refs, (sg_ref, su_ref, sd_ref, g_ref, b_ref), y_refs = refs[:n_x], refs[n_x:n_x + 5], refs[n_x + 5:]
    i = pl.program_id(0)
    routed = _dot(pwt_ref[0], ys_ref[0])
    x = _block_of(x_refs, first_blocks)
    xb = x.astype(BF16)
    hs = _silu(_dot(xb, sg_ref[...])) * _dot(xb, su_ref[...])
    shared = _dot(hs.astype(BF16), sd_ref[...])
    y = _layer_norm(alpha * x + (routed + shared), g_ref[...], b_ref[...])
    bounds = list(first_blocks[1:]) + [pl.num_programs(0)]
    for y_ref, lo, hi in zip(y_refs, first_blocks, bounds):
        @pl.when((i >= lo) & (i < hi))
        def _(y_ref=y_ref):
            y_ref[...] = y


def _moe_out(pwt, ys, xs_in, ws_gate, ws_up, ws_down, ln_g, ln_b, alpha):
    d = xs_in[0].shape[1]
    nblk, tb, slab = pwt.shape
    firsts, x_specs, _ = _multi_specs(xs_in, tb)
    consts = [ws_gate, ws_up, ws_down, ln_g.reshape(1, d), ln_b.reshape(1, d)]
    return pl.pallas_call(
        functools.partial(_moe_out_kernel, alpha, tuple(firsts)),
        grid=(nblk,),
        in_specs=[pl.BlockSpec((1, tb, slab), lambda i: (i, 0, 0)),
                  pl.BlockSpec((1, slab, d), lambda i: (i, 0, 0))] + x_specs
                 + [pl.BlockSpec(c.shape, lambda i: (0, 0)) for c in consts],
        out_specs=x_specs,
        out_shape=[jax.ShapeDtypeStruct(x.shape, F32) for x in xs_in],
        compiler_params=_cparams("arbitrary"),
        name="moe_combine",
    )(pwt, ys.reshape(nblk, slab, d), *xs_in, *consts)


def _moe(xs_in, w_router_t, b_router, w_gate, w_up, w_down, ws_gate, ws_up, ws_down, ln_g, ln_b, alpha):
    xs, pwt, meta = _dispatch(xs_in, w_router_t, b_router)
    nblk, slab, d = xs.shape
    tile_e, n_valid, rows_tbl, n_tiles = _tile_schedule(meta, slab)
    ys = _grouped_experts(xs.reshape(nblk * slab, d), tile_e, n_valid, rows_tbl, n_tiles, w_gate, w_up, w_down)
    return _moe_out(pwt, ys, xs_in, ws_gate, ws_up, ws_down, ln_g, ln_b, alpha)


def _rope_tables(pos):
    half = HEAD_DIM // 2
    inv_freq = ROPE_THETA ** (-jnp.arange(half, dtype=F32) / half)
    ang = pos.astype(F32)[:, None] * inv_freq[None, :]
    cos, sin = jnp.cos(ang), jnp.sin(ang)
    return jnp.tile(jnp.concatenate([cos, cos], axis=1), (1, 2)), jnp.tile(jnp.concatenate([-sin, sin], axis=1), (1, 2))


def _block_diag(w):
    h, d, _ = w.shape
    out = jnp.zeros((h * d, h * d), w.dtype)
    for i in range(h):
        out = out.at[i * d:(i + 1) * d, i * d:(i + 1) * d].set(w[i])
    return out


def _pick(n, pref):
    return pref if n % pref == 0 else n


def kernel(x_prompt, x_sample, mem_prompt, cache_win128, cache_win512, cache_win2048, cache_mem_kv, state_conv, state_C, state_n, state_m, w_in, conv_w, conv_b, wq_m, wk_m, w_if, b_if, mh_norm_w, skip_m, w_out, ln1_g, ln1_b, w_cq, w_ck, w_cv, w_co, ln2_g, ln2_b, w_router, b_router, w_gate, w_up, w_down, ws_gate, ws_up, ws_down, ln3_g, ln3_b):
    depth = w_in.shape[0]
    assert depth == 1
    alpha = float((2 * depth) ** 0.25)
    bp, seq, d = x_prompt.shape
    bs, dec, _ = x_sample.shape
    assert seq % ATT_GROUPS[-1][0] == 0
    for c, (window, _) in zip((cache_win128, cache_win512, cache_win2048), ATT_GROUPS):
        assert c.shape[2] == window

    l = 0
    bf = lambda a: a.astype(BF16)
    w_in_b, w_out_b = bf(w_in[l]), bf(w_out[l])
    w_cq_b, w_ck_b, w_cv_b, w_co_b = bf(w_cq[l]), bf(w_ck[l]), bf(w_cv[l]), bf(w_co[l])
    wq_bd, wk_bd = bf(_block_diag(wq_m[l])), bf(_block_diag(wk_m[l]))
    w_router_t = bf(w_router[l].T)
    ws_gate_b, ws_up_b, ws_down_b = bf(ws_gate[l]), bf(ws_up[l]), bf(ws_down[l])

    def tail_of_layer(x1, qc, kv, batch, t, tm, tq):
        n = batch * t
        oc = _xattn(qc.reshape(batch, t, d), kv, tq=tq).reshape(n, d)
        return _xout(oc, x1, w_co_b, ln2_g[l], ln2_b[l], alpha, tm=tm)

    def mlstm(m3, batch, t, states, chunk, bb):
        return _mlstm(m3.reshape(batch, t, -1), *states, conv_w[l], conv_b[l], wq_bd, wk_bd, w_if[l], b_if[l],
                      mh_norm_w[l], skip_m[l], chunk=chunk, bb=bb)

    np_ = bp * seq
    xp = x_prompt.reshape(np_, d)
    cos_p, sin_p = _rope_tables(jnp.arange(seq))
    *qkv, m3, t128, t512, t2048 = _inproj_prompt(xp, w_in_b, cos_p, sin_p, seq=seq, tm=512)
    parts, lses = [], []
    for g in range(len(ATT_GROUPS)):
        o, lse = _win_attn(*qkv[3 * g:3 * g + 3])
        parts.append(o)
        lses.append(lse)
    zeros_p = [jnp.zeros((bp, CONV_WIDTH - 1, MLSTM_WIDTH), F32), jnp.zeros((bp, MLSTM_WIDTH, HEAD_DIM), F32),
               jnp.zeros((bp, MLSTM_HEADS, HEAD_DIM), F32), jnp.zeros((bp, 1, MLSTM_HEADS), F32)]
    hm_p, p_conv, p_c, p_n, p_m = mlstm(m3, bp, seq, zeros_p, 128, bp)
    x1, qc = _outproj(parts + lses, hm_p.reshape(np_, -1), xp, w_out_b, ln1_g[l], ln1_b[l], w_cq_b, alpha, tm=512)
    kv_p = _memkv(mem_prompt, w_ck_b, w_cv_b)
    x2_p = tail_of_layer(x1, qc, kv_p, bp, seq, 512, 512)

    ns = bs * dec
    xs = x_sample.reshape(ns, d)
    cos_s, sin_s = _rope_tables(jnp.tile(PAST_LEN + jnp.arange(dec), bs))
    qs, ks, vs, m3s = _inproj(xs, w_in_b, cos_s, sin_s, tm=ns)
    caches = [c[l].reshape(bs, c.shape[2], 2 * GROUP_WIDTH) for c in (cache_win128, cache_win512, cache_win2048)]
    att_s, s128, s512, s2048 = _dec_attn(qs.reshape(bs, dec, -1), ks.reshape(bs, dec, -1), vs.reshape(bs, dec, -1),
                                         caches)
    states_s = [state_conv[l], state_C[l].reshape(bs, MLSTM_WIDTH, HEAD_DIM), state_n[l],
                state_m[l].reshape(bs, 1, MLSTM_HEADS)]
    hm_s, s_conv, s_c, s_n, s_m = mlstm(m3s, bs, dec, states_s, dec, 4 if bs % 4 == 0 else 1)
    x1s, qcs = _outproj([att_s.reshape(ns, -1)], hm_s.reshape(ns, -1), xs, w_out_b, ln1_g[l], ln1_b[l], w_cq_b,
                        alpha, tm=ns)
    kv_s = cache_mem_kv[l].reshape(bs, cache_mem_kv.shape[2], 2 * d)
    x2_s = tail_of_layer(x1s, qcs, kv_s, bs, dec, ns, dec)

    y_p, y_s = _moe([x2_p, x2_s], w_router_t, b_router[l], w_gate[l], w_up[l], w_down[l],
                    ws_gate_b, ws_up_b, ws_down_b, ln3_g[l], ln3_b[l], alpha)

    win_shape = lambda a, b_: a.reshape(1, b_, a.shape[1], 2, HEADS_PER_GROUP, HEAD_DIM)
    return (y_p.reshape(bp, seq, d), y_s.reshape(bs, dec, d),
            win_shape(t128, bp), win_shape(t512, bp), win_shape(t2048, bp),
            kv_p.reshape(1, bp, mem_prompt.shape[1], 2, MEM_HEADS, d // MEM_HEADS),
            p_conv[None], p_c.reshape(1, bp, MLSTM_HEADS, HEAD_DIM, HEAD_DIM), p_n[None],
            p_m.reshape(1, bp, MLSTM_HEADS),
            win_shape(s128, bs), win_shape(s512, bs), win_shape(s2048, bs),
            s_conv[None], s_c.reshape(1, bs, MLSTM_HEADS, HEAD_DIM, HEAD_DIM), s_n[None],
            s_m.reshape(1, bs, MLSTM_HEADS))
```

```python
import functools
import math

import jax
import jax.numpy as jnp
from jax import lax
from jax.experimental import pallas as pl
from jax.experimental.pallas import tpu as pltpu

F32 = jnp.float32
BF16 = jnp.bfloat16

HEAD_DIM = 64
ATT_GROUPS = ((128, 1), (512, 4), (2048, 16))
HEADS_PER_GROUP = 4
GROUP_WIDTH = HEADS_PER_GROUP * HEAD_DIM
ATT_WIDTH = GROUP_WIDTH * len(ATT_GROUPS)
N_KEYS = 129
ROPE_THETA = 10000.0
PAST_LEN = 8192
MLSTM_HEADS = 4
MLSTM_WIDTH = MLSTM_HEADS * HEAD_DIM
CONV_WIDTH = 4
MEM_HEADS = 4
N_EXPERT_GROUPS = 8
TOPK_GROUPS = 4
TOP_K = 6
ROUTED_SCALE = 2.5
LN_EPS = 1e-5
NEG = -1e30
VMEM_LIMIT = 56 * 1024 * 1024


def _cparams(*sem):
    return pltpu.CompilerParams(dimension_semantics=sem, vmem_limit_bytes=VMEM_LIMIT)


def _dot(a, b):
    return jnp.dot(a, b, preferred_element_type=F32)


def _dot_nt(a, b):
    return lax.dot_general(a, b, (((1,), (1,)), ((), ())), preferred_element_type=F32)


def _dot_tn(a, b):
    return lax.dot_general(a, b, (((0,), (0,)), ((), ())), preferred_element_type=F32)


def _split_bf16(a):
    hi = a.astype(BF16)
    return hi, (a - hi.astype(F32)).astype(BF16)


def _dot_split(a, b, dims, a_exact=False, b_exact=False):
    dn = (dims, ((), ()))
    dot = lambda x, y: lax.dot_general(x, y, dn, preferred_element_type=F32)
    a_hi, a_lo = _split_bf16(a)
    b_hi, b_lo = _split_bf16(b)
    out = dot(a_hi, b_hi)
    if not b_exact:
        out = out + dot(a_hi, b_lo)
    if not a_exact:
        out = out + dot(a_lo, b_hi)
    return out


def _layer_norm(x, g, b):
    mu = jnp.mean(x, axis=-1, keepdims=True)
    xc = x - mu
    var = jnp.mean(xc * xc, axis=-1, keepdims=True)
    return xc * lax.rsqrt(var + LN_EPS) * g + b


def _sigmoid(x):
    return 1.0 / (1.0 + jnp.exp(-x))


def _silu(x):
    return x * _sigmoid(x)


def _log_sigmoid(x):
    return jnp.minimum(x, 0.0) - jnp.log(1.0 + jnp.exp(-jnp.abs(x)))


def _project_qkv(x_ref, w_ref, cos_ref, sin_ref):
    tm = x_ref.shape[0]
    x = x_ref[...].astype(BF16)
    cos = jnp.concatenate([cos_ref[...]] * (ATT_WIDTH // 128), axis=1)
    sin = jnp.concatenate([sin_ref[...]] * (ATT_WIDTH // 128), axis=1)
    lane = lax.broadcasted_iota(jnp.int32, (tm, ATT_WIDTH), 1)
    first_half = (lane % HEAD_DIM) < (HEAD_DIM // 2)

    def rope(t):
        fwd = pltpu.roll(t, ATT_WIDTH - HEAD_DIM // 2, 1)
        bwd = pltpu.roll(t, HEAD_DIM // 2, 1)
        return t * cos + jnp.where(first_half, fwd, bwd) * sin

    q = rope(_dot(x, w_ref[:, 0:ATT_WIDTH])) * (HEAD_DIM ** -0.5)
    k = rope(_dot(x, w_ref[:, ATT_WIDTH:2 * ATT_WIDTH]))
    v = _dot(x, w_ref[:, 2 * ATT_WIDTH:3 * ATT_WIDTH])
    return q, k, v, _dot(x, w_ref[:, 3 * ATT_WIDTH:])


def _inproj_kernel(x_ref, w_ref, cos_ref, sin_ref, q_ref, k_ref, v_ref, m_ref):
    q, k, v, m = _project_qkv(x_ref, w_ref, cos_ref, sin_ref)
    q_ref[...] = q.astype(q_ref.dtype)
    k_ref[...] = k.astype(k_ref.dtype)
    v_ref[...] = v.astype(v_ref.dtype)
    m_ref[...] = m


def _inproj_prompt_kernel(x_ref, w_ref, cos_ref, sin_ref, *refs):
    n_g = len(ATT_GROUPS)
    qkv_refs, m_ref, tails, scr = refs[:3 * n_g], refs[3 * n_g], refs[3 * n_g + 1:4 * n_g + 1], refs[4 * n_g + 1]
    tm = x_ref.shape[0]
    q, k, v, m = _project_qkv(x_ref, w_ref, cos_ref, sin_ref)
    m_ref[...] = m
    for g, (_, dil) in enumerate(ATT_GROUPS):
        cols = slice(g * GROUP_WIDTH, (g + 1) * GROUP_WIDTH)
        for a, val in enumerate((q, k, v)):
            o_ref = qkv_refs[3 * g + a]
            if dil == 1:
                o_ref[0, 0] = val[:, cols].astype(o_ref.dtype)
                continue
            for c in range(GROUP_WIDTH // 128):
                scr[c] = val[:, g * GROUP_WIDTH + c * 128:g * GROUP_WIDTH + (c + 1) * 128]
            for r in range(dil):
                for c in range(GROUP_WIDTH // 128):
                    o_ref[0, r, :, c * 128:(c + 1) * 128] = scr[c, pl.ds(r, tm // dil, stride=dil), :].astype(o_ref.dtype)
        t_ref = tails[g]
        rows = t_ref.shape[1]
        t_ref[0, :, 0:GROUP_WIDTH] = k[tm - rows:, cols]
        t_ref[0, :, GROUP_WIDTH:] = v[tm - rows:, cols]


def _inproj_prompt(x, w_in, cos, sin, *, seq, tm):
    n, d = x.shape
    tiles_per_seq = seq // tm
    batch = n // seq
    out_shape, out_specs = [], []
    for _, dil in ATT_GROUPS:
        for _ in range(3):
            out_shape.append(jax.ShapeDtypeStruct((batch, dil, seq // dil, GROUP_WIDTH), BF16))
            out_specs.append(pl.BlockSpec((1, dil, tm // dil, GROUP_WIDTH),
                                          lambda i: (i // tiles_per_seq, 0, i % tiles_per_seq, 0)))
    m_width = w_in.shape[1] - 3 * ATT_WIDTH
    out_shape.append(jax.ShapeDtypeStruct((n, m_width), F32))
    out_specs.append(pl.BlockSpec((tm, m_width), lambda i: (i, 0)))
    for window, _ in ATT_GROUPS:
        rows = min(window, tm)
        first = tiles_per_seq - window // rows if window > rows else tiles_per_seq - 1
        out_shape.append(jax.ShapeDtypeStruct((batch, min(window, seq), 2 * GROUP_WIDTH), F32))
        out_specs.append(pl.BlockSpec(
            (1, rows, 2 * GROUP_WIDTH),
            lambda i, first=first: (i // tiles_per_seq, jnp.maximum(i % tiles_per_seq - first, 0), 0)))
    return pl.pallas_call(
        _inproj_prompt_kernel,
        grid=(n // tm,),
        in_specs=[pl.BlockSpec((tm, d), lambda i: (i, 0)),
                  pl.BlockSpec(w_in.shape, lambda i: (0, 0)),
                  pl.BlockSpec((tm, 128), lambda i: (i % tiles_per_seq, 0)),
                  pl.BlockSpec((tm, 128), lambda i: (i % tiles_per_seq, 0))],
        out_specs=out_specs,
        out_shape=out_shape,
        scratch_shapes=[pltpu.VMEM((GROUP_WIDTH // 128, tm, 128), F32)],
        compiler_params=_cparams("arbitrary"),
        name="inproj_rope_prompt",
    )(x, w_in, cos, sin)


def _inproj(x, w_in, cos, sin, *, tm):
    n, d = x.shape
    out_shape = [jax.ShapeDtypeStruct((n, ATT_WIDTH), F32)] * 3 + [
        jax.ShapeDtypeStruct((n, w_in.shape[1] - 3 * ATT_WIDTH), F32)]
    row_spec = lambda w: pl.BlockSpec((tm, w), lambda i: (i, 0))
    out_specs = [row_spec(ATT_WIDTH)] * 3 + [row_spec(w_in.shape[1] - 3 * ATT_WIDTH)]
    tiles_per_seq = cos.shape[0] // tm
    return pl.pallas_call(
        _inproj_kernel,
        grid=(n // tm,),
        in_specs=[row_spec(d),
                  pl.BlockSpec(w_in.shape, lambda i: (0, 0)),
                  pl.BlockSpec((tm, 128), lambda i: (i % tiles_per_seq, 0)),
                  pl.BlockSpec((tm, 128), lambda i: (i % tiles_per_seq, 0))],
        out_specs=out_specs,
        out_shape=out_shape,
        compiler_params=_cparams("arbitrary"),
        name="inproj_rope",
    )(x, w_in, cos, sin)


def _win_attn_kernel(q_ref, kp_ref, kc_ref, vp_ref, vc_ref, o_ref, lse_ref):
    j = pl.program_id(2)
    tq = q_ref.shape[1]
    sub = N_KEYS - 1
    assert kp_ref.shape[1] == sub and tq % sub == 0
    k_all = jnp.concatenate([kp_ref[0], kc_ref[0]], axis=0)
    v_all = jnp.concatenate([vp_ref[0], vc_ref[0]], axis=0)
    n_h = HEADS_PER_GROUP
    row = lax.broadcasted_iota(jnp.int32, (n_h * sub, 2 * sub), 0) % sub
    col = lax.broadcasted_iota(jnp.int32, (n_h * sub, 2 * sub), 1)
    band = (col >= row) & (col <= row + sub)
    head_rows = lax.broadcasted_iota(jnp.int32, (n_h * sub, GROUP_WIDTH), 0) // sub
    head_lanes = lax.broadcasted_iota(jnp.int32, (n_h * sub, GROUP_WIDTH), 1) // HEAD_DIM
    own = head_rows == head_lanes
    lane_l = lax.broadcasted_iota(jnp.int32, (sub, 128), 1)
    for i in range(tq // sub):
        q = q_ref[0, i * sub:(i + 1) * sub, :]
        kk = k_all[i * sub:(i + 2) * sub]
        vv = v_all[i * sub:(i + 2) * sub]
        valid = band & ((j > 0) | (col >= sub)) if i == 0 else band
        q4 = jnp.concatenate([q] * n_h, axis=0)
        q4 = jnp.where(own, q4, jnp.zeros_like(q4))
        s = jnp.where(valid, _dot_nt(q4, kk), NEG)
        m = jnp.max(s, axis=1, keepdims=True)
        p = jnp.exp(s - m)
        l = jnp.sum(p, axis=1, keepdims=True)
        o4 = jnp.where(own, _dot(p.astype(BF16), vv) / l, 0.0)
        lse4 = m + jnp.log(l)
        o_acc = o4[0:sub]
        lse_acc = jnp.zeros((sub, 128), F32)
        for h in range(n_h):
            if h:
                o_acc = o_acc + o4[h * sub:(h + 1) * sub]
            lse_acc = jnp.where(lane_l // 32 == h, lse4[h * sub:(h + 1) * sub], lse_acc)
        o_ref[0, i * sub:(i + 1) * sub, :] = o_acc.astype(o_ref.dtype)
        lse_ref[0, i * sub:(i + 1) * sub, :] = lse_acc


def _win_attn(q, k, v, *, tq=512):
    b, dil, ts, _ = q.shape
    tq = min(tq, ts)
    sub = N_KEYS - 1
    cur = pl.BlockSpec((None, 1, tq, GROUP_WIDTH), lambda bi, r, j: (bi, r, j, 0))
    prev = pl.BlockSpec((None, 1, sub, GROUP_WIDTH),
                        lambda bi, r, j: (bi, r, jnp.maximum(j * (tq // sub) - 1, 0), 0))
    return pl.pallas_call(
        _win_attn_kernel,
        grid=(b, dil, ts // tq),
        in_specs=[cur, prev, cur, prev, cur],
        out_specs=[pl.BlockSpec((None, 1, tq, GROUP_WIDTH), lambda bi, r, j: (bi, r, j, 0)),
                   pl.BlockSpec((None, 1, tq, 128), lambda bi, r, j: (bi, r, j, 0))],
        out_shape=[jax.ShapeDtypeStruct((b, dil, ts, GROUP_WIDTH), BF16),
                   jax.ShapeDtypeStruct((b, dil, ts, 128), F32)],
        compiler_params=_cparams("arbitrary", "arbitrary", "arbitrary"),
        name="window_attention",
    )(q, k, k, v, v)


def _dec_attn_kernel(q_ref, k_ref, v_ref, c0_ref, c1_ref, c2_ref, att_ref, o0_ref, o1_ref, o2_ref,
                     e0_ref, e1_ref, e2_ref):
    t_new = q_ref.shape[1]
    n_h = HEADS_PER_GROUP
    q = q_ref[0]
    k_new = k_ref[0]
    v_new = v_ref[0]
    n_q = n_h * t_new
    own = (lax.broadcasted_iota(jnp.int32, (n_q, GROUP_WIDTH), 0) // t_new
           == lax.broadcasted_iota(jnp.int32, (n_q, GROUP_WIDTH), 1) // HEAD_DIM)
    outs, lses = [], []
    for g, ((window, dil), c_ref, o_ref, e_ref) in enumerate(
            zip(ATT_GROUPS, (c0_ref, c1_ref, c2_ref), (o0_ref, o1_ref, o2_ref), (e0_ref, e1_ref, e2_ref))):
        w = c_ref.shape[1]
        cols = slice(g * GROUP_WIDTH, (g + 1) * GROUP_WIDTH)
        kv_new = jnp.concatenate([k_new[:, cols], v_new[:, cols]], axis=1)
        n_chunk = e_ref.shape[0]
        pad_rows = e_ref.shape[1] - (w + t_new)
        if pad_rows:
            @pl.when(pl.program_id(0) == 0)
            def _(e_ref=e_ref, w=w, pad_rows=pad_rows):
                e_ref[:, w + t_new:, :] = jnp.zeros((n_chunk, pad_rows, 128), F32)
        for c in range(n_chunk):
            e_ref[c, 0:w, :] = c_ref[0, :, c * 128:(c + 1) * 128]
            e_ref[c, w:w + t_new, :] = kv_new[:, c * 128:(c + 1) * 128]
            o_ref[0, :, c * 128:(c + 1) * 128] = e_ref[c, t_new:w + t_new, :]
        classes = min(dil, t_new)
        parts = [jnp.concatenate([e_ref[c, pl.ds(r, DEC_SPAN, stride=dil), :] for c in range(n_chunk)], axis=1)
                 for r in range(classes)]
        kv_all = jnp.concatenate(parts, axis=0) if classes > 1 else parts[0]
        k_all = kv_all[:, 0:GROUP_WIDTH].astype(BF16)
        v_all = kv_all[:, GROUP_WIDTH:].astype(BF16)
        q4 = jnp.where(own, jnp.concatenate([q[:, cols]] * n_h, axis=0), 0.0)
        s = _dot_nt(q4.astype(BF16), k_all)
        t_idx = lax.broadcasted_iota(jnp.int32, s.shape, 0) % t_new
        col = lax.broadcasted_iota(jnp.int32, s.shape, 1)
        first = t_idx // dil
        step_i = col % DEC_SPAN
        valid = (col // DEC_SPAN == t_idx % dil) & (step_i >= first) & (step_i <= first + (N_KEYS - 1))
        s = jnp.where(valid, s, NEG)
        m = jnp.max(s, axis=1, keepdims=True)
        p = jnp.exp(s - m)
        l = jnp.sum(p, axis=1, keepdims=True)
        outs.append(_dot(p.astype(BF16), v_all) / l)
        lses.append(m + jnp.log(l))
    top = jnp.maximum(jnp.maximum(lses[0], lses[1]), lses[2])
    es = [jnp.exp(l - top) for l in lses]
    tot = es[0] + es[1] + es[2]
    for g in range(len(ATT_GROUPS)):
        weighted = jnp.where(own, outs[g] * (es[g] / tot), 0.0)
        acc = weighted[0:t_new]
        for h in range(1, n_h):
            acc = acc + weighted[h * t_new:(h + 1) * t_new]
        att_ref[0, :, g * GROUP_WIDTH:(g + 1) * GROUP_WIDTH] = acc


DEC_SPAN = 136


def _dec_attn(q, k, v, caches):
    b, t_new, _ = q.shape
    assert N_KEYS + t_new - 1 <= DEC_SPAN
    tok = pl.BlockSpec((1, t_new, ATT_WIDTH), lambda i: (i, 0, 0))
    cspec = [pl.BlockSpec((1,) + c.shape[1:], lambda i: (i, 0, 0)) for c in caches]

    def ext_rows(c, dil):
        return max(c.shape[1] + t_new, (min(dil, t_new) - 1) + (DEC_SPAN - 1) * dil + 1)

    return pl.pallas_call(
        _dec_attn_kernel,
        grid=(b,),
        in_specs=[tok, tok, tok] + cspec,
        out_specs=[tok] + cspec,
        out_shape=[jax.ShapeDtypeStruct(q.shape, F32)] + [jax.ShapeDtypeStruct(c.shape, F32) for c in caches],
        scratch_shapes=[pltpu.VMEM((c.shape[2] // 128, ext_rows(c, dil), 128), F32)
                        for c, (_, dil) in zip(caches, ATT_GROUPS)],
        compiler_params=_cparams("arbitrary"),
        name="decode_attention",
    )(q, k, v, *caches)


def _mlstm_kernel(m3_ref, conv0_ref, c0_ref, n0_ref, m0_ref, *rest):
    consts, (hm_ref, convo_ref, co_ref, no_ref, mo_ref, cbuf, c_s, n_s, m_s) = rest[:10], rest[10:]
    for b in range(m3_ref.shape[0]):
        _mlstm_chunk(m3_ref.at[b], conv0_ref.at[b], c0_ref.at[b], n0_ref.at[b], m0_ref.at[b], *consts,
                     hm_ref.at[b], convo_ref.at[b], co_ref.at[b], no_ref.at[b], mo_ref.at[b],
                     cbuf.at[b], c_s.at[b], n_s.at[b], m_s.at[b])


def _mlstm_chunk(m3_ref, conv0_ref, c0_ref, n0_ref, m0_ref, convw_ref, convb_ref, wq_ref, wk_ref,
                 wif_ref, wift_ref, bif_ref, bift_ref, normw_ref, skip_ref,
                 hm_ref, convo_ref, co_ref, no_ref, mo_ref, cbuf, c_s, n_s, m_s):
    j = pl.program_id(1)
    L = m3_ref.shape[0]
    W = MLSTM_WIDTH
    D = HEAD_DIM

    @pl.when(j == 0)
    def _():
        cbuf[0:8, :] = jnp.zeros((8, W), F32)
        cbuf[8 - (CONV_WIDTH - 1):8, :] = conv0_ref[...]
        c_s[...] = c0_ref[...]
        n_s[...] = n0_ref[...]
        m_s[...] = m0_ref[...]

    blk = m3_ref[...]
    c_in = blk[:, 0:W]
    v_m = blk[:, W:2 * W]
    z = blk[:, 2 * W:3 * W]
    cbuf[8:8 + L, :] = c_in
    acc = jnp.zeros((L, W), F32) + convb_ref[...]
    for tap in range(CONV_WIDTH):
        off = 8 - (CONV_WIDTH - 1) + tap
        acc = acc + cbuf[off:off + L, :] * convw_ref[tap:tap + 1, :]
    xc = _silu(acc)
    convo_ref[...] = cbuf[8 + L - (CONV_WIDTH - 1):8 + L, :]
    cbuf[0:8, :] = cbuf[L:L + 8, :]

    xcb = xc.astype(BF16)
    q_m = _dot(xcb, wq_ref[...])
    k_m = _dot(xcb, wk_ref[...])
    gate_in = jnp.concatenate([q_m, k_m, v_m], axis=1)
    nn, nt = ((1,), (0,)), ((1,), (1,))
    g_col = _dot_split(gate_in, wif_ref[...], nn) + bif_ref[...]
    g_row = _dot_split(wift_ref[...], gate_in, nt) + bift_ref[...]
    i_col, lf_col = g_col[:, 0:MLSTM_HEADS], _log_sigmoid(g_col[:, MLSTM_HEADS:])
    i_row, lf_row = g_row[0:MLSTM_HEADS, :], _log_sigmoid(g_row[MLSTM_HEADS:, :])
    rr = lax.broadcasted_iota(jnp.int32, (L, L), 0)
    cc = lax.broadcasted_iota(jnp.int32, (L, L), 1)
    causal = cc <= rr
    tri = causal.astype(F32)
    b_col = _dot_split(tri, lf_col, nn, a_exact=True)
    b_row = _dot_split(lf_row, tri, nt, b_exact=True)
    ks = k_m * (D ** -0.5)
    kb = ks.astype(BF16)
    vb = v_m.astype(BF16)
    n_h = MLSTM_HEADS
    heads = range(n_h)
    own = (lax.broadcasted_iota(jnp.int32, (n_h * L, W), 0) // L
           == lax.broadcasted_iota(jnp.int32, (n_h * L, W), 1) // D)
    q4 = jnp.where(own, jnp.concatenate([q_m] * n_h, axis=0), 0.0)
    q4b = q4.astype(BF16)
    m_prev = m_s[...]
    per_row = lambda a: jnp.concatenate([a[:, h:h + 1] for h in heads], axis=0)
    per_key = lambda a: jnp.concatenate([jnp.broadcast_to(a[h:h + 1, :], (L, L)) for h in heads], axis=0)
    bc = per_row(b_col)
    m_prev_r = jnp.concatenate([jnp.broadcast_to(m_prev[:, h:h + 1], (L, 1)) for h in heads], axis=0)
    causal4 = (lax.broadcasted_iota(jnp.int32, (n_h * L, L), 1)
               <= lax.broadcasted_iota(jnp.int32, (n_h * L, L), 0) % L)
    d_intra = jnp.where(causal4, bc - per_key(b_row) + per_key(i_row), -jnp.inf)
    a_inter = bc + m_prev_r
    m_t = jnp.maximum(a_inter, jnp.max(d_intra, axis=1, keepdims=True))
    s = _dot_nt(q4b, kb) * jnp.exp(d_intra - m_t)
    w_inter = jnp.exp(a_inter - m_t)
    c_all = c_s[...]
    c_bd = jnp.where(lax.broadcasted_iota(jnp.int32, (W, W), 0) // D == lax.broadcasted_iota(jnp.int32, (W, W), 1) // D,
                     jnp.concatenate([c_all] * n_h, axis=1), 0.0)
    n_all = n_s[...]
    n_flat = jnp.concatenate([n_all[h:h + 1, :] for h in heads], axis=1)
    num = _dot(s.astype(BF16), vb) + w_inter * _dot_nt(q4b, c_bd.astype(BF16))
    den = jnp.sum(s, axis=1, keepdims=True) + w_inter * jnp.sum(q4 * n_flat, axis=1, keepdims=True)
    hh = jnp.where(own, num / jnp.maximum(jnp.abs(den), jnp.exp(-m_t)), 0.0)
    mu = jnp.sum(hh, axis=1, keepdims=True) * (1.0 / D)
    hc = jnp.where(own, hh - mu, 0.0)
    var = jnp.sum(hc * hc, axis=1, keepdims=True) * (1.0 / D)
    hn = hc * lax.rsqrt(var + LN_EPS)
    h_all = hn[0:L]
    for h in range(1, n_h):
        h_all = h_all + hn[h * L:(h + 1) * L]
    g_tot = b_col[L - 1:L, :]
    a_end = g_tot + m_prev
    d_end = g_tot - b_col + i_col
    m_new = jnp.maximum(a_end, jnp.max(d_end, axis=0, keepdims=True))
    w_s = jnp.exp(d_end - m_new)
    decay = jnp.exp(a_end - m_new)
    w_s_l = jnp.concatenate([jnp.broadcast_to(w_s[:, h:h + 1], (L, D)) for h in heads], axis=1)
    upd = _dot_tn((v_m * w_s_l).astype(BF16), kb)
    decay_r = jnp.concatenate([jnp.broadcast_to(decay[:, h:h + 1], (D, 1)) for h in heads], axis=0)
    c_s[...] = decay_r * c_all + jnp.concatenate([upd[h * D:(h + 1) * D, h * D:(h + 1) * D] for h in heads], axis=0)
    k_sum = jnp.sum(w_s_l * ks, axis=0, keepdims=True)
    n_s[...] = (jnp.concatenate([jnp.broadcast_to(decay[:, h:h + 1], (1, D)) for h in heads], axis=0) * n_all
                + jnp.concatenate([k_sum[:, h * D:(h + 1) * D] for h in heads], axis=0))
    m_s[...] = m_new
    hm = (h_all * normw_ref[...] + skip_ref[...] * xc) * _silu(z)
    hm_ref[...] = hm.astype(hm_ref.dtype)
    co_ref[...] = c_s[...]
    no_ref[...] = n_s[...]
    mo_ref[...] = m_s[...]


def _mlstm(m3, conv0, c0, n0, m0, conv_w, conv_b, wq_bd, wk_bd, w_if, b_if, norm_w, skip, *, chunk, bb):
    b, t, _ = m3.shape
    W = MLSTM_WIDTH
    full = lambda a: pl.BlockSpec(a.shape, lambda bi, j: (0,) * a.ndim)
    per_b = lambda a: pl.BlockSpec((bb,) + a.shape[1:], lambda bi, j: (bi,) + (0,) * (a.ndim - 1))
    consts = [conv_w, conv_b.reshape(1, W), wq_bd, wk_bd, w_if, w_if.T, b_if.reshape(1, -1),
              b_if.reshape(-1, 1), norm_w.reshape(1, W), skip.reshape(1, W)]
    states = [conv0, c0, n0, m0]
    return pl.pallas_call(
        _mlstm_kernel,
        grid=(b // bb, t // chunk),
        in_specs=[pl.BlockSpec((bb, chunk, 3 * W), lambda bi, j: (bi, j, 0))] + [per_b(s) for s in states]
                 + [full(c) for c in consts],
        out_specs=[pl.BlockSpec((bb, chunk, W), lambda bi, j: (bi, j, 0))] + [per_b(s) for s in states],
        out_shape=[jax.ShapeDtypeStruct((b, t, W), BF16)] + [jax.ShapeDtypeStruct(s.shape, F32) for s in states],
        scratch_shapes=[pltpu.VMEM((bb, chunk + 8, W), F32), pltpu.VMEM((bb,) + c0.shape[1:], F32),
                        pltpu.VMEM((bb,) + n0.shape[1:], F32), pltpu.VMEM((bb,) + m0.shape[1:], F32)],
        compiler_params=_cparams("arbitrary", "arbitrary"),
        name="mlstm",
    )(m3, *states, *consts)


def _interleave(ref, scr):
    dil, rows, width = ref.shape
    if dil == 1:
        return ref[0].astype(F32)
    for r in range(dil):
        for c in range(width // 128):
            scr[c, pl.ds(r, rows, stride=dil), :] = ref[r, :, c * 128:(c + 1) * 128].astype(F32)
    return jnp.concatenate([scr[c] for c in range(width // 128)], axis=1)


def _combine(os, lses):
    tm = os[0].shape[0]
    lane = lax.broadcasted_iota(jnp.int32, (tm, GROUP_WIDTH), 1)

    def spread(l2):
        out = jnp.zeros((tm, GROUP_WIDTH), F32)
        for h in range(HEADS_PER_GROUP):
            out = jnp.where(lane // HEAD_DIM == h, l2[:, 32 * h:32 * h + 1], out)
        return out

    ls = [spread(l) for l in lses]
    top = jnp.maximum(jnp.maximum(ls[0], ls[1]), ls[2])
    es = [jnp.exp(l - top) for l in ls]
    tot = es[0] + es[1] + es[2]
    return [(o * (e / tot)).astype(BF16) for o, e in zip(os, es)]


def _outproj_kernel(combine, alpha, *refs):
    if combine:
        (o0, o1, o2, l0, l1, l2, hm_ref, x_ref, wo_ref, g_ref, b_ref, wq_ref, x1_ref, qc_ref, scr) = refs
        att = _combine([_interleave(o, scr) for o in (o0, o1, o2)], [_interleave(l, scr) for l in (l0, l1, l2)])
    else:
        (a_ref, hm_ref, x_ref, wo_ref, g_ref, b_ref, wq_ref, x1_ref, qc_ref) = refs
        att = [a_ref[:, g * GROUP_WIDTH:(g + 1) * GROUP_WIDTH].astype(BF16) for g in range(len(ATT_GROUPS))]
    mix = _dot(hm_ref[...].astype(BF16), wo_ref[ATT_WIDTH:, :])
    for g, a in enumerate(att):
        mix = mix + _dot(a, wo_ref[g * GROUP_WIDTH:(g + 1) * GROUP_WIDTH, :])
    x1 = _layer_norm(alpha * x_ref[...] + mix, g_ref[...], b_ref[...])
    x1_ref[...] = x1
    qc_ref[...] = (_dot(x1.astype(BF16), wq_ref[...]) * ((x1.shape[1] // MEM_HEADS) ** -0.5)).astype(qc_ref.dtype)


def _outproj(att_parts, hm, x, w_out, ln_g, ln_b, w_cq, alpha, *, tm):
    n, d = x.shape
    combine = len(att_parts) > 1
    row = lambda a: pl.BlockSpec((tm, a.shape[1]), lambda i: (i, 0))
    full = lambda a: pl.BlockSpec(a.shape, lambda i: (0, 0))

    def split(a):
        _, dil, ts, w = a.shape
        tiles_per_seq = ts * dil // tm
        return pl.BlockSpec((None, dil, tm // dil, w), lambda i: (i // tiles_per_seq, 0, i % tiles_per_seq, 0))

    consts = [w_out, ln_g.reshape(1, d), ln_b.reshape(1, d), w_cq]
    return pl.pallas_call(
        functools.partial(_outproj_kernel, combine, alpha),
        grid=(n // tm,),
        in_specs=[split(a) if combine else row(a) for a in att_parts] + [row(hm), row(x)] + [full(c) for c in consts],
        out_specs=[pl.BlockSpec((tm, d), lambda i: (i, 0))] * 2,
        out_shape=[jax.ShapeDtypeStruct((n, d), F32), jax.ShapeDtypeStruct((n, d), BF16)],
        scratch_shapes=[pltpu.VMEM((GROUP_WIDTH // 128, tm, 128), F32)] if combine else [],
        compiler_params=_cparams("arbitrary"),
        name="outproj_ln1",
    )(*att_parts, hm, x, *consts)


def _memkv_kernel(mem_ref, wk_ref, wv_ref, kv_ref):
    d = mem_ref.shape[2]
    mem = mem_ref[0].astype(BF16)
    kv_ref[0, :, 0:d] = _dot(mem, wk_ref[...])
    kv_ref[0, :, d:] = _dot(mem, wv_ref[...])


def _memkv(mem, w_ck, w_cv):
    b, m, d = mem.shape
    return pl.pallas_call(
        _memkv_kernel,
        grid=(b,),
        in_specs=[pl.BlockSpec((1, m, d), lambda i: (i, 0, 0)),
                  pl.BlockSpec(w_ck.shape, lambda i: (0, 0)), pl.BlockSpec(w_cv.shape, lambda i: (0, 0))],
        out_specs=pl.BlockSpec((1, m, 2 * d), lambda i: (i, 0, 0)),
        out_shape=jax.ShapeDtypeStruct((b, m, 2 * d), F32),
        compiler_params=_cparams("arbitrary"),
        name="memory_kv",
    )(mem, w_ck, w_cv)


def _xattn_kernel(q_ref, kv_ref, o_ref):
    d = q_ref.shape[2]
    hd = d // MEM_HEADS
    for b in range(q_ref.shape[0]):
        q = q_ref[b]
        for h in range(MEM_HEADS):
            k = kv_ref[b, :, h * hd:(h + 1) * hd].astype(BF16)
            v = kv_ref[b, :, d + h * hd:d + (h + 1) * hd].astype(BF16)
            s = _dot_nt(q[:, h * hd:(h + 1) * hd], k)
            p = jnp.exp(s - jnp.max(s, axis=1, keepdims=True))
            p = p / jnp.sum(p, axis=1, keepdims=True)
            o_ref[b, :, h * hd:(h + 1) * hd] = _dot(p.astype(BF16), v).astype(o_ref.dtype)


def _xattn(qc, kv, *, tq, bb=1):
    b, t, d = qc.shape
    m = kv.shape[1]
    return pl.pallas_call(
        _xattn_kernel,
        grid=(b // bb, t // tq),
        in_specs=[pl.BlockSpec((bb, tq, d), lambda bi, j: (bi, j, 0)),
                  pl.BlockSpec((bb, m, 2 * d), lambda bi, j: (bi, 0, 0))],
        out_specs=pl.BlockSpec((bb, tq, d), lambda bi, j: (bi, j, 0)),
        out_shape=jax.ShapeDtypeStruct((b, t, d), BF16),
        compiler_params=_cparams("arbitrary", "arbitrary"),
        name="cross_attention",
    )(qc, kv)


def _xout_kernel(alpha, o_ref, x_ref, w_ref, g_ref, b_ref, y_ref):
    y_ref[...] = _layer_norm(alpha * x_ref[...] + _dot(o_ref[...], w_ref[...]), g_ref[...], b_ref[...])


def _xout(oc, x1, w_co, ln_g, ln_b, alpha, *, tm):
    n, d = x1.shape
    row = pl.BlockSpec((tm, d), lambda i: (i, 0))
    full = lambda a: pl.BlockSpec(a.shape, lambda i: (0, 0))
    consts = [w_co, ln_g.reshape(1, d), ln_b.reshape(1, d)]
    return pl.pallas_call(
        functools.partial(_xout_kernel, alpha),
        grid=(n // tm,),
        in_specs=[row, row] + [full(c) for c in consts],
        out_specs=row,
        out_shape=jax.ShapeDtypeStruct((n, d), F32),
        compiler_params=_cparams("arbitrary"),
        name="cross_out_ln2",
    )(oc, x1, *consts)


def _first_index_of_max(vals, idx, big):
    mx = jnp.max(vals, axis=0, keepdims=True)
    return mx, jnp.min(jnp.where(vals == mx, idx, big), axis=0, keepdims=True)


def _route(xb, wrt_ref, br_ref):
    tm = xb.shape[0]
    n_e = wrt_ref.shape[0]
    per_group = n_e // N_EXPERT_GROUPS
    logits = _dot_nt(wrt_ref[...], xb)
    scores = _sigmoid(logits)
    biased = scores + br_ref[...]
    e_idx = lax.broadcasted_iota(jnp.int32, (n_e, tm), 0).astype(F32)
    g_scores = []
    for g in range(N_EXPERT_GROUPS):
        sub = biased[g * per_group:(g + 1) * per_group, :]
        sidx = lax.broadcasted_iota(jnp.int32, (per_group, tm), 0).astype(F32)
        m1, a1 = _first_index_of_max(sub, sidx, per_group)
        m2 = jnp.max(jnp.where(sidx == a1, -jnp.inf, sub), axis=0, keepdims=True)
        g_scores.append(m1 + m2)
    gs = jnp.concatenate(g_scores, axis=0)
    g_idx = lax.broadcasted_iota(jnp.int32, (N_EXPERT_GROUPS, tm), 0).astype(F32)
    g_sel = jnp.zeros((N_EXPERT_GROUPS, tm), F32)
    work = gs
    for _ in range(TOPK_GROUPS):
        _, a = _first_index_of_max(work, g_idx, N_EXPERT_GROUPS)
        hit = g_idx == a
        g_sel = jnp.where(hit, 1.0, g_sel)
        work = jnp.where(hit, -jnp.inf, work)
    e_mask = jnp.concatenate(
        [jnp.broadcast_to(g_sel[g:g + 1, :], (per_group, tm)) for g in range(N_EXPERT_GROUPS)], axis=0)
    work = jnp.where(e_mask > 0.5, biased, -jnp.inf)
    sel = jnp.zeros((n_e, tm), F32)
    picks = []
    for _ in range(TOP_K):
        cand = jnp.where(sel > 0.5, -jnp.inf, work)
        mx = jnp.max(cand, axis=0, keepdims=True)
        a = jnp.min(jnp.where((cand == mx) & (sel < 0.5), e_idx, float(n_e)), axis=0, keepdims=True)
        pick = jnp.where(e_idx == a, 1.0, 0.0)
        picks.append(pick)
        sel = sel + pick
    w_sel = sel * scores
    gates_t = w_sel / jnp.sum(w_sel, axis=0, keepdims=True) * ROUTED_SCALE
    return gates_t, sel, picks


ROW_CHUNK = 16
TILE_CHUNKS = 32
TILE_ROWS = ROW_CHUNK * TILE_CHUNKS
TOKEN_BLOCK = 256
GATHER_AHEAD = 3
TILE_SLOTS = GATHER_AHEAD + 1
EAGER_PIECES = 4
assert EAGER_PIECES <= TILE_SLOTS * TILE_CHUNKS


def _slab_rows(tb, n_e):
    return -(-(TOP_K * tb + n_e * (ROW_CHUNK - 1)) // 128) * 128


def _block_of(refs, first_blocks):
    i = pl.program_id(0)
    x = refs[0][...]
    for ref, first in zip(refs[1:], first_blocks[1:]):
        x = jnp.where(i >= first, ref[...], x)
    return x


def _multi_specs(arrays, tb):
    firsts, specs, start = [], [], 0
    for a in arrays:
        nb = a.shape[0] // tb
        firsts.append(start)
        specs.append(pl.BlockSpec((tb, a.shape[1]),
                                  lambda i, start=start, nb=nb: (jnp.clip(i - start, 0, nb - 1), 0)))
        start += nb
    return firsts, specs, start


def _dispatch_kernel(first_blocks, *refs):
    n_x = len(first_blocks)
    x_refs, (wrt_ref, br_ref, xs_ref, pwt_ref, meta_ref) = refs[:n_x], refs[n_x:]
    tb = x_refs[0].shape[0]
    n_e = wrt_ref.shape[0]
    slab = xs_ref.shape[1]
    xb = _block_of(x_refs, first_blocks).astype(BF16)
    gates_t, sel, picks = _route(xb, wrt_ref, br_ref)
    cnt = jnp.sum(sel, axis=1, keepdims=True)
    padded = jnp.floor((cnt + (ROW_CHUNK - 1)) * (1.0 / ROW_CHUNK)) * ROW_CHUNK
    padded_b = jnp.broadcast_to(padded, (n_e, 128))
    er = lax.broadcasted_iota(jnp.int32, (n_e, n_e), 0)
    ec = lax.broadcasted_iota(jnp.int32, (n_e, n_e), 1)
    off_b = _dot((ec < er).astype(BF16), padded_b.astype(BF16))
    tr = lax.broadcasted_iota(jnp.int32, (tb, tb), 0)
    tc = lax.broadcasted_iota(jnp.int32, (tb, tb), 1)
    rank = _dot(sel.astype(BF16), (tr < tc).astype(BF16))
    dest = off_b[:, 0:1] + rank
    dest_k = [jnp.sum(p * dest, axis=0, keepdims=True) for p in picks]
    gate_k = [jnp.sum(p * gates_t, axis=0, keepdims=True) for p in picks]
    meta_ref[0, 0] = padded_b
    meta_ref[0, 1] = off_b

    stacked = jnp.concatenate(dest_k + [jnp.zeros((8 - TOP_K, tb), F32)] + gate_k
                              + [jnp.zeros((128 - 8 - TOP_K, tb), F32)], axis=0)
    cols = jnp.transpose(stacked)
    step = 512
    for c0 in range(0, slab, step):
        lanes = lax.broadcasted_iota(jnp.int32, (tb, step), 1).astype(F32) + float(c0)
        w = jnp.zeros((tb, step), F32)
        for k in range(TOP_K):
            w = jnp.where(lanes == cols[:, k:k + 1], cols[:, 8 + k:9 + k], w)
        pwt_ref[0, :, c0:c0 + step] = w.astype(pwt_ref.dtype)
        onehot_t = jnp.where(w != 0.0, 1.0, 0.0).astype(BF16)
        xs_ref[0, c0:c0 + step, :] = _dot_tn(onehot_t, xb).astype(xs_ref.dtype)


def _dispatch(xs_in, w_router_t, b_router):
    d = xs_in[0].shape[1]
    n_e = w_router_t.shape[0]
    tb = TOKEN_BLOCK
    firsts, x_specs, nblk = _multi_specs(xs_in, tb)
    slab = _slab_rows(tb, n_e)
    assert slab % 512 == 0
    return pl.pallas_call(
        functools.partial(_dispatch_kernel, tuple(firsts)),
        grid=(nblk,),
        in_specs=x_specs + [pl.BlockSpec(w_router_t.shape, lambda i: (0, 0)),
                            pl.BlockSpec((n_e, 1), lambda i: (0, 0))],
        out_specs=[pl.BlockSpec((1, slab, d), lambda i: (i, 0, 0)),
                   pl.BlockSpec((1, tb, slab), lambda i: (i, 0, 0)),
                   pl.BlockSpec((1, 2, n_e, 128), lambda i: (i, 0, 0, 0))],
        out_shape=[jax.ShapeDtypeStruct((nblk, slab, d), BF16),
                   jax.ShapeDtypeStruct((nblk, tb, slab), BF16),
                   jax.ShapeDtypeStruct((nblk, 2, n_e, 128), F32)],
        compiler_params=_cparams("arbitrary"),
        name="moe_dispatch",
    )(*xs_in, w_router_t, b_router.reshape(n_e, 1))


def _schedule_kernel(n_e, nblk, t_max, zero_row, slot_ref, pieces_ref, first_ref, total_ref, tile0_ref,
                     te_ref, nv_ref, row_ref, nt_ref):
    def per_expert(e, carry):
        def per_blocks(bu, c):
            for u in range(unroll):
                idx = e * nblk + bu * unroll + u
                p = slot_ref[idx]
                r0 = first_ref[idx]
                for j in range(EAGER_PIECES):
                    row_ref[p + j] = r0 + j * ROW_CHUNK
            return c
        unroll = max(u for u in range(1, 9) if nblk % u == 0)
        lax.fori_loop(0, nblk // unroll, per_blocks, 0)

        def long_segment(blk, c):
            idx = e * nblk + blk
            p = slot_ref[idx]
            r0 = first_ref[idx]

            def per_piece(j, cc):
                row_ref[p + j] = r0 + j * ROW_CHUNK
                return cc
            lax.fori_loop(EAGER_PIECES, pieces_ref[idx], per_piece, 0)
            return c
        lax.fori_loop(0, nblk, long_segment, 0)

        cnt = total_ref[e]
        tile = tile0_ref[e]
        n_t = (cnt + (TILE_CHUNKS - 1)) // TILE_CHUNKS
        start = tile * TILE_CHUNKS

        def pad(p, c):
            row_ref[p] = zero_row
            return c
        lax.fori_loop(start + cnt, start + n_t * TILE_CHUNKS, pad, 0)

        def per_tile(i, c):
            te_ref[tile + i] = e
            nv_ref[tile + i] = jnp.minimum(TILE_CHUNKS, cnt - i * TILE_CHUNKS)
            return c
        lax.fori_loop(0, n_t, per_tile, 0)
        return carry
    lax.fori_loop(0, n_e, per_expert, 0)

    last = n_e - 1
    tile = tile0_ref[last] + (total_ref[last] + (TILE_CHUNKS - 1)) // TILE_CHUNKS
    nt_ref[0] = tile
    last_e = te_ref[jnp.maximum(tile - 1, 0)]

    def idle_tile(t, c):
        te_ref[t] = last_e
        nv_ref[t] = 0
        return c
    lax.fori_loop(tile, t_max, idle_tile, 0)

    def idle_piece(p, c):
        row_ref[p] = zero_row
        return c
    lax.fori_loop(tile * TILE_CHUNKS, t_max * TILE_CHUNKS, idle_piece, 0)


def _tile_schedule(meta, slab):
    nblk, _, n_e, _ = meta.shape
    pieces = (meta[:, 0, :, 0].astype(jnp.int32) // ROW_CHUNK).T
    first_row = (meta[:, 1, :, 0].astype(jnp.int32) + (jnp.arange(nblk, dtype=jnp.int32) * slab)[:, None]).T
    before = jnp.cumsum(pieces, axis=1) - pieces
    total = jnp.sum(pieces, axis=1)
    tiles = (total + (TILE_CHUNKS - 1)) // TILE_CHUNKS
    tile0 = jnp.cumsum(tiles) - tiles
    slot = tile0[:, None] * TILE_CHUNKS + before
    t_max = -(-(TOP_K * nblk * TOKEN_BLOCK + nblk * n_e * (ROW_CHUNK - 1)) // TILE_ROWS) + n_e + TILE_SLOTS
    assert slab - ROW_CHUNK >= TOP_K * TOKEN_BLOCK + n_e * (ROW_CHUNK - 1)
    smem = pl.BlockSpec(memory_space=pltpu.SMEM)
    i32 = lambda n: jax.ShapeDtypeStruct((n,), jnp.int32)
    return pl.pallas_call(
        functools.partial(_schedule_kernel, n_e, nblk, t_max, slab - ROW_CHUNK),
        in_specs=[smem] * 5,
        out_specs=[smem, smem, smem, smem],
        out_shape=[i32(t_max), i32(t_max), i32(t_max * TILE_CHUNKS), i32(1)],
        name="moe_schedule",
    )(slot.reshape(-1), pieces.reshape(-1), first_row.reshape(-1), total, tile0)


def _grouped_kernel(te_ref, nv_ref, row_ref, nt_ref, xs_hbm, wg_ref, wu_ref, wd_ref, ys_hbm,
                    lhs, obuf, wg_b, wu_b, wd_b, sem_in, sem_out):
    t = pl.program_id(0)
    n_tiles = nt_ref[0]

    def piece(ref, tile, s):
        row = pl.multiple_of(row_ref[tile * TILE_CHUNKS + s], ROW_CHUNK)
        return ref.at[pl.ds(row, ROW_CHUNK)]

    def local(buf, slot, s):
        start = s * ROW_CHUNK if isinstance(s, int) else pl.multiple_of(s * ROW_CHUNK, ROW_CHUNK)
        return buf.at[slot, pl.ds(start, ROW_CHUNK)]

    def gather(tile, s):
        slot = tile % TILE_SLOTS
        return pltpu.make_async_copy(piece(xs_hbm, tile, s), local(lhs, slot, s), sem_in.at[slot])

    def scatter(tile, s):
        slot = tile % TILE_SLOTS
        return pltpu.make_async_copy(local(obuf, slot, s), piece(ys_hbm, tile, s), sem_out.at[slot])

    def gathered_tile(slot):
        return pltpu.make_async_copy(xs_hbm.at[pl.ds(0, TILE_ROWS)], lhs.at[slot], sem_in.at[slot])

    def scattered_tile(slot):
        return pltpu.make_async_copy(obuf.at[slot], ys_hbm.at[pl.ds(0, TILE_ROWS)], sem_out.at[slot])

    def for_valid_pieces(tile, fn):
        def body(s, carry):
            fn(tile, s)
            return carry
        lax.fori_loop(0, nv_ref[tile], body, 0)

    @pl.when(t == 0)
    def _():
        for tile in range(GATHER_AHEAD):
            for s in range(TILE_CHUNKS):
                gather(tile, s).start()

    @pl.when((t >= TILE_SLOTS) & (t - TILE_SLOTS < n_tiles))
    def _():
        full = nv_ref[t - TILE_SLOTS] == TILE_CHUNKS

        @pl.when(full)
        def _():
            scattered_tile(t % TILE_SLOTS).wait()

        @pl.when(jnp.logical_not(full))
        def _():
            for_valid_pieces(t - TILE_SLOTS, lambda tile, s: scatter(tile, s).wait())

    @pl.when((t < n_tiles) & ((t == 0) | (te_ref[t] != te_ref[jnp.maximum(t - 1, 0)])))
    def _():
        wg_b[...] = wg_ref[0].astype(BF16)
        wu_b[...] = wu_ref[0].astype(BF16)
        wd_b[...] = wd_ref[0].astype(BF16)

    @pl.when(t < n_tiles)
    def _():
        slot = t % TILE_SLOTS
        gathered_tile(slot).wait()
        for s in range(TILE_CHUNKS):
            gather(t + GATHER_AHEAD, s).start()
        x = lhs[slot]
        h = _silu(_dot(x, wg_b[...])) * _dot(x, wu_b[...])
        obuf[slot] = _dot(h.astype(BF16), wd_b[...]).astype(obuf.dtype)

    @pl.when((t >= n_tiles) & (t < n_tiles + GATHER_AHEAD))
    def _():
        gathered_tile(t % TILE_SLOTS).wait()

    @pl.when(t < n_tiles)
    def _():
        full = nv_ref[t] == TILE_CHUNKS

        @pl.when(full)
        def _():
            for s in range(TILE_CHUNKS):
                scatter(t, s).start()

        @pl.when(jnp.logical_not(full))
        def _():
            for_valid_pieces(t, lambda tile, s: scatter(tile, s).start())


def _grouped_experts(xs, tile_e, n_valid, rows_tbl, n_tiles, w_gate, w_up, w_down):
    rows, d = xs.shape
    n_e, _, ff = w_gate.shape
    t_max = tile_e.shape[0]
    grid_spec = pltpu.PrefetchScalarGridSpec(
        num_scalar_prefetch=4,
        grid=(t_max,),
        in_specs=[pl.BlockSpec(memory_space=pl.ANY),
                  pl.BlockSpec((1, d, ff), lambda t, te, nv, sr, nt: (te[t], 0, 0)),
                  pl.BlockSpec((1, d, ff), lambda t, te, nv, sr, nt: (te[t], 0, 0)),
                  pl.BlockSpec((1, ff, d), lambda t, te, nv, sr, nt: (te[t], 0, 0))],
        out_specs=pl.BlockSpec(memory_space=pl.ANY),
        scratch_shapes=[pltpu.VMEM((TILE_SLOTS, TILE_ROWS, d), BF16), pltpu.VMEM((TILE_SLOTS, TILE_ROWS, d), BF16),
                        pltpu.VMEM((d, ff), BF16), pltpu.VMEM((d, ff), BF16), pltpu.VMEM((ff, d), BF16),
                        pltpu.SemaphoreType.DMA((TILE_SLOTS,)), pltpu.SemaphoreType.DMA((TILE_SLOTS,))],
    )
    return pl.pallas_call(
        _grouped_kernel,
        grid_spec=grid_spec,
        out_shape=jax.ShapeDtypeStruct((rows, d), BF16),
        input_output_aliases={4: 0},
        compiler_params=_cparams("arbitrary"),
        name="moe_grouped_experts",
    )(tile_e, n_valid, rows_tbl, n_tiles, xs, w_gate, w_up, w_down)


def _moe_out_kernel(alpha, first_blocks, pwt_ref, ys_ref, *refs):
    n_x = len(first_blocks)
    x_refs, (sg_ref, su_ref, sd_ref, g_ref, b_ref), y_refs = refs[:n_x], refs[n_x:n_x + 5], refs[n_x + 5:]
    i = pl.program_id(0)
    routed = _dot(pwt_ref[0], ys_ref[0])
    x = _block_of(x_refs, first_blocks)
    xb = x.astype(BF16)
    hs = _silu(_dot(xb, sg_ref[...])) * _dot(xb, su_ref[...])
    shared = _dot(hs.astype(BF16), sd_ref[...])
    y = _layer_norm(alpha * x + (routed + shared), g_ref[...], b_ref[...])
    bounds = list(first_blocks[1:]) + [pl.num_programs(0)]
    for y_ref, lo, hi in zip(y_refs, first_blocks, bounds):
        @pl.when((i >= lo) & (i < hi))
        def _(y_ref=y_ref):
            y_ref[...] = y


def _moe_out(pwt, ys, xs_in, ws_gate, ws_up, ws_down, ln_g, ln_b, alpha):
    d = xs_in[0].shape[1]
    nblk, tb, slab = pwt.shape
    firsts, x_specs, _ = _multi_specs(xs_in, tb)
    consts = [ws_gate, ws_up, ws_down, ln_g.reshape(1, d), ln_b.reshape(1, d)]
    return pl.pallas_call(
        functools.partial(_moe_out_kernel, alpha, tuple(firsts)),
        grid=(nblk,),
        in_specs=[pl.BlockSpec((1, tb, slab), lambda i: (i, 0, 0)),
                  pl.BlockSpec((1, slab, d), lambda i: (i, 0, 0))] + x_specs
                 + [pl.BlockSpec(c.shape, lambda i: (0, 0)) for c in consts],
        out_specs=x_specs,
        out_shape=[jax.ShapeDtypeStruct(x.shape, F32) for x in xs_in],
        compiler_params=_cparams("arbitrary"),
        name="moe_combine",
    )(pwt, ys.reshape(nblk, slab, d), *xs_in, *consts)


def _moe(xs_in, w_router_t, b_router, w_gate, w_up, w_down, ws_gate, ws_up, ws_down, ln_g, ln_b, alpha):
    xs, pwt, meta = _dispatch(xs_in, w_router_t, b_router)
    nblk, slab, d = xs.shape
    tile_e, n_valid, rows_tbl, n_tiles = _tile_schedule(meta, slab)
    ys = _grouped_experts(xs.reshape(nblk * slab, d), tile_e, n_valid, rows_tbl, n_tiles, w_gate, w_up, w_down)
    return _moe_out(pwt, ys, xs_in, ws_gate, ws_up, ws_down, ln_g, ln_b, alpha)


def _rope_tables(pos):
    half = HEAD_DIM // 2
    inv_freq = ROPE_THETA ** (-jnp.arange(half, dtype=F32) / half)
    ang = pos.astype(F32)[:, None] * inv_freq[None, :]
    cos, sin = jnp.cos(ang), jnp.sin(ang)
    return jnp.tile(jnp.concatenate([cos, cos], axis=1), (1, 2)), jnp.tile(jnp.concatenate([-sin, sin], axis=1), (1, 2))


def _block_diag(w):
    h, d, _ = w.shape
    out = jnp.zeros((h * d, h * d), w.dtype)
    for i in range(h):
        out = out.at[i * d:(i + 1) * d, i * d:(i + 1) * d].set(w[i])
    return out


def _pick(n, pref):
    return pref if n % pref == 0 else n


def kernel(x_prompt, x_sample, mem_prompt, cache_win128, cache_win512, cache_win2048, cache_mem_kv, state_conv, state_C, state_n, state_m, w_in, conv_w, conv_b, wq_m, wk_m, w_if, b_if, mh_norm_w, skip_m, w_out, ln1_g, ln1_b, w_cq, w_ck, w_cv, w_co, ln2_g, ln2_b, w_router, b_router, w_gate, w_up, w_down, ws_gate, ws_up, ws_down, ln3_g, ln3_b):
    depth = w_in.shape[0]
    assert depth == 1
    alpha = float((2 * depth) ** 0.25)
    bp, seq, d = x_prompt.shape
    bs, dec, _ = x_sample.shape
    assert seq % ATT_GROUPS[-1][0] == 0
    for c, (window, _) in zip((cache_win128, cache_win512, cache_win2048), ATT_GROUPS):
        assert c.shape[2] == window

    l = 0
    bf = lambda a: a.astype(BF16)
    w_in_b, w_out_b = bf(w_in[l]), bf(w_out[l])
    w_cq_b, w_ck_b, w_cv_b, w_co_b = bf(w_cq[l]), bf(w_ck[l]), bf(w_cv[l]), bf(w_co[l])
    wq_bd, wk_bd = bf(_block_diag(wq_m[l])), bf(_block_diag(wk_m[l]))
    w_router_t = bf(w_router[l].T)
    ws_gate_b, ws_up_b, ws_down_b = bf(ws_gate[l]), bf(ws_up[l]), bf(ws_down[l])

    def tail_of_layer(x1, qc, kv, batch, t, tm, tq, bb=1):
        n = batch * t
        oc = _xattn(qc.reshape(batch, t, d), kv, tq=tq, bb=bb).reshape(n, d)
        return _xout(oc, x1, w_co_b, ln2_g[l], ln2_b[l], alpha, tm=tm)

    def mlstm(m3, batch, t, states, chunk, bb):
        return _mlstm(m3.reshape(batch, t, -1), *states, conv_w[l], conv_b[l], wq_bd, wk_bd, w_if[l], b_if[l],
                      mh_norm_w[l], skip_m[l], chunk=chunk, bb=bb)

    np_ = bp * seq
    xp = x_prompt.reshape(np_, d)
    cos_p, sin_p = _rope_tables(jnp.arange(seq))
    *qkv, m3, t128, t512, t2048 = _inproj_prompt(xp, w_in_b, cos_p, sin_p, seq=seq, tm=512)
    parts, lses = [], []
    for g in range(len(ATT_GROUPS)):
        o, lse = _win_attn(*qkv[3 * g:3 * g + 3])
        parts.append(o)
        lses.append(lse)
    zeros_p = [jnp.zeros((bp, CONV_WIDTH - 1, MLSTM_WIDTH), F32), jnp.zeros((bp, MLSTM_WIDTH, HEAD_DIM), F32),
               jnp.zeros((bp, MLSTM_HEADS, HEAD_DIM), F32), jnp.zeros((bp, 1, MLSTM_HEADS), F32)]
    hm_p, p_conv, p_c, p_n, p_m = mlstm(m3, bp, seq, zeros_p, 128, bp)
    x1, qc = _outproj(parts + lses, hm_p.reshape(np_, -1), xp, w_out_b, ln1_g[l], ln1_b[l], w_cq_b, alpha, tm=512)
    kv_p = _memkv(mem_prompt, w_ck_b, w_cv_b)
    x2_p = tail_of_layer(x1, qc, kv_p, bp, seq, 512, 512)

    ns = bs * dec
    xs = x_sample.reshape(ns, d)
    cos_s, sin_s = _rope_tables(jnp.tile(PAST_LEN + jnp.arange(dec), bs))
    qs, ks, vs, m3s = _inproj(xs, w_in_b, cos_s, sin_s, tm=ns)
    caches = [c[l].reshape(bs, c.shape[2], 2 * GROUP_WIDTH) for c in (cache_win128, cache_win512, cache_win2048)]
    att_s, s128, s512, s2048 = _dec_attn(qs.reshape(bs, dec, -1), ks.reshape(bs, dec, -1), vs.reshape(bs, dec, -1),
                                         caches)
    states_s = [state_conv[l], state_C[l].reshape(bs, MLSTM_WIDTH, HEAD_DIM), state_n[l],
                state_m[l].reshape(bs, 1, MLSTM_HEADS)]
    hm_s, s_conv, s_c, s_n, s_m = mlstm(m3s, bs, dec, states_s, dec, 4 if bs % 4 == 0 else 1)
    x1s, qcs = _outproj([att_s.reshape(ns, -1)], hm_s.reshape(ns, -1), xs, w_out_b, ln1_g[l], ln1_b[l], w_cq_b,
                        alpha, tm=ns)
    kv_s = cache_mem_kv[l].reshape(bs, cache_mem_kv.shape[2], 2 * d)
    x2_s = tail_of_layer(x1s, qcs, kv_s, bs, dec, ns, dec, bb=4 if bs % 4 == 0 else 1)

    y_p, y_s = _moe([x2_p, x2_s], w_router_t, b_router[l], w_gate[l], w_up[l], w_down[l],
                    ws_gate_b, ws_up_b, ws_down_b, ln3_g[l], ln3_b[l], alpha)

    win_shape = lambda a, b_: a.reshape(1, b_, a.shape[1], 2, HEADS_PER_GROUP, HEAD_DIM)
    return (y_p.reshape(bp, seq, d), y_s.reshape(bs, dec, d),
            win_shape(t128, bp), win_shape(t512, bp), win_shape(t2048, bp),
            kv_p.reshape(1, bp, mem_prompt.shape[1], 2, MEM_HEADS, d // MEM_HEADS),
            p_conv[None], p_c.reshape(1, bp, MLSTM_HEADS, HEAD_DIM, HEAD_DIM), p_n[None],
            p_m.reshape(1, bp, MLSTM_HEADS),
            win_shape(s128, bs), win_shape(s512, bs), win_shape(s2048, bs),
            s_conv[None], s_c.reshape(1, bs, MLSTM_HEADS, HEAD_DIM, HEAD_DIM), s_n[None],
            s_m.reshape(1, bs, MLSTM_HEADS))
```

```python
import functools
import math

import jax
import jax.numpy as jnp
from jax import lax
from jax.experimental import pallas as pl
from jax.experimental.pallas import tpu as pltpu

F32 = jnp.float32
BF16 = jnp.bfloat16

HEAD_DIM = 64
ATT_GROUPS = ((128, 1), (512, 4), (2048, 16))
HEADS_PER_GROUP = 4
GROUP_WIDTH = HEADS_PER_GROUP * HEAD_DIM
ATT_WIDTH = GROUP_WIDTH * len(ATT_GROUPS)
N_KEYS = 129
ROPE_THETA = 10000.0
PAST_LEN = 8192
MLSTM_HEADS = 4
MLSTM_WIDTH = MLSTM_HEADS * HEAD_DIM
CONV_WIDTH = 4
MEM_HEADS = 4
N_EXPERT_GROUPS = 8
TOPK_GROUPS = 4
TOP_K = 6
ROUTED_SCALE = 2.5
LN_EPS = 1e-5
NEG = -1e30
VMEM_LIMIT = 56 * 1024 * 1024


def _cparams(*sem):
    return pltpu.CompilerParams(dimension_semantics=sem, vmem_limit_bytes=VMEM_LIMIT)


def _dot(a, b):
    return jnp.dot(a, b, preferred_element_type=F32)


def _dot_nt(a, b):
    return lax.dot_general(a, b, (((1,), (1,)), ((), ())), preferred_element_type=F32)


def _dot_tn(a, b):
    return lax.dot_general(a, b, (((0,), (0,)), ((), ())), preferred_element_type=F32)


def _split_bf16(a):
    hi = a.astype(BF16)
    return hi, (a - hi.astype(F32)).astype(BF16)


def _dot_split(a, b, dims, a_exact=False, b_exact=False):
    dn = (dims, ((), ()))
    dot = lambda x, y: lax.dot_general(x, y, dn, preferred_element_type=F32)
    a_hi, a_lo = _split_bf16(a)
    b_hi, b_lo = _split_bf16(b)
    out = dot(a_hi, b_hi)
    if not b_exact:
        out = out + dot(a_hi, b_lo)
    if not a_exact:
        out = out + dot(a_lo, b_hi)
    return out


def _layer_norm(x, g, b):
    mu = jnp.mean(x, axis=-1, keepdims=True)
    xc = x - mu
    var = jnp.mean(xc * xc, axis=-1, keepdims=True)
    return xc * lax.rsqrt(var + LN_EPS) * g + b


def _sigmoid(x):
    return 1.0 / (1.0 + jnp.exp(-x))


def _silu(x):
    return x * _sigmoid(x)


def _log_sigmoid(x):
    return jnp.minimum(x, 0.0) - jnp.log(1.0 + jnp.exp(-jnp.abs(x)))


def _project_qkv(x_ref, w_ref, cos_ref, sin_ref):
    tm = x_ref.shape[0]
    x = x_ref[...].astype(BF16)
    cos = jnp.concatenate([cos_ref[...]] * (ATT_WIDTH // 128), axis=1)
    sin = jnp.concatenate([sin_ref[...]] * (ATT_WIDTH // 128), axis=1)
    lane = lax.broadcasted_iota(jnp.int32, (tm, ATT_WIDTH), 1)
    first_half = (lane % HEAD_DIM) < (HEAD_DIM // 2)

    def rope(t):
        fwd = pltpu.roll(t, ATT_WIDTH - HEAD_DIM // 2, 1)
        bwd = pltpu.roll(t, HEAD_DIM // 2, 1)
        return t * cos + jnp.where(first_half, fwd, bwd) * sin

    q = rope(_dot(x, w_ref[:, 0:ATT_WIDTH])) * (HEAD_DIM ** -0.5)
    k = rope(_dot(x, w_ref[:, ATT_WIDTH:2 * ATT_WIDTH]))
    v = _dot(x, w_ref[:, 2 * ATT_WIDTH:3 * ATT_WIDTH])
    return q, k, v, _dot(x, w_ref[:, 3 * ATT_WIDTH:])


def _inproj_kernel(x_ref, w_ref, cos_ref, sin_ref, q_ref, k_ref, v_ref, m_ref):
    q, k, v, m = _project_qkv(x_ref, w_ref, cos_ref, sin_ref)
    q_ref[...] = q.astype(q_ref.dtype)
    k_ref[...] = k.astype(k_ref.dtype)
    v_ref[...] = v.astype(v_ref.dtype)
    m_ref[...] = m


def _inproj_prompt_kernel(x_ref, w_ref, cos_ref, sin_ref, *refs):
    n_g = len(ATT_GROUPS)
    qkv_refs, m_ref, tails, scr = refs[:3 * n_g], refs[3 * n_g], refs[3 * n_g + 1:4 * n_g + 1], refs[4 * n_g + 1]
    tm = x_ref.shape[0]
    q, k, v, m = _project_qkv(x_ref, w_ref, cos_ref, sin_ref)
    m_ref[...] = m
    for g, (_, dil) in enumerate(ATT_GROUPS):
        cols = slice(g * GROUP_WIDTH, (g + 1) * GROUP_WIDTH)
        for a, val in enumerate((q, k, v)):
            o_ref = qkv_refs[3 * g + a]
            if dil == 1:
                o_ref[0, 0] = val[:, cols].astype(o_ref.dtype)
                continue
            for c in range(GROUP_WIDTH // 128):
                scr[c] = val[:, g * GROUP_WIDTH + c * 128:g * GROUP_WIDTH + (c + 1) * 128]
            for r in range(dil):
                for c in range(GROUP_WIDTH // 128):
                    o_ref[0, r, :, c * 128:(c + 1) * 128] = scr[c, pl.ds(r, tm // dil, stride=dil), :].astype(o_ref.dtype)
        t_ref = tails[g]
        rows = t_ref.shape[1]
        t_ref[0, :, 0:GROUP_WIDTH] = k[tm - rows:, cols]
        t_ref[0, :, GROUP_WIDTH:] = v[tm - rows:, cols]


def _inproj_prompt(x, w_in, cos, sin, *, seq, tm):
    n, d = x.shape
    tiles_per_seq = seq // tm
    batch = n // seq
    out_shape, out_specs = [], []
    for _, dil in ATT_GROUPS:
        for _ in range(3):
            out_shape.append(jax.ShapeDtypeStruct((batch, dil, seq // dil, GROUP_WIDTH), BF16))
            out_specs.append(pl.BlockSpec((1, dil, tm // dil, GROUP_WIDTH),
                                          lambda i: (i // tiles_per_seq, 0, i % tiles_per_seq, 0)))
    m_width = w_in.shape[1] - 3 * ATT_WIDTH
    out_shape.append(jax.ShapeDtypeStruct((n, m_width), F32))
    out_specs.append(pl.BlockSpec((tm, m_width), lambda i: (i, 0)))
    for window, _ in ATT_GROUPS:
        rows = min(window, tm)
        first = tiles_per_seq - window // rows if window > rows else tiles_per_seq - 1
        out_shape.append(jax.ShapeDtypeStruct((batch, min(window, seq), 2 * GROUP_WIDTH), F32))
        out_specs.append(pl.BlockSpec(
            (1, rows, 2 * GROUP_WIDTH),
            lambda i, first=first: (i // tiles_per_seq, jnp.maximum(i % tiles_per_seq - first, 0), 0)))
    return pl.pallas_call(
        _inproj_prompt_kernel,
        grid=(n // tm,),
        in_specs=[pl.BlockSpec((tm, d), lambda i: (i, 0)),
                  pl.BlockSpec(w_in.shape, lambda i: (0, 0)),
                  pl.BlockSpec((tm, 128), lambda i: (i % tiles_per_seq, 0)),
                  pl.BlockSpec((tm, 128), lambda i: (i % tiles_per_seq, 0))],
        out_specs=out_specs,
        out_shape=out_shape,
        scratch_shapes=[pltpu.VMEM((GROUP_WIDTH // 128, tm, 128), F32)],
        compiler_params=_cparams("arbitrary"),
        name="inproj_rope_prompt",
    )(x, w_in, cos, sin)


def _inproj(x, w_in, cos, sin, *, tm):
    n, d = x.shape
    out_shape = [jax.ShapeDtypeStruct((n, ATT_WIDTH), F32)] * 3 + [
        jax.ShapeDtypeStruct((n, w_in.shape[1] - 3 * ATT_WIDTH), F32)]
    row_spec = lambda w: pl.BlockSpec((tm, w), lambda i: (i, 0))
    out_specs = [row_spec(ATT_WIDTH)] * 3 + [row_spec(w_in.shape[1] - 3 * ATT_WIDTH)]
    tiles_per_seq = cos.shape[0] // tm
    return pl.pallas_call(
        _inproj_kernel,
        grid=(n // tm,),
        in_specs=[row_spec(d),
                  pl.BlockSpec(w_in.shape, lambda i: (0, 0)),
                  pl.BlockSpec((tm, 128), lambda i: (i % tiles_per_seq, 0)),
                  pl.BlockSpec((tm, 128), lambda i: (i % tiles_per_seq, 0))],
        out_specs=out_specs,
        out_shape=out_shape,
        compiler_params=_cparams("arbitrary"),
        name="inproj_rope",
    )(x, w_in, cos, sin)


def _win_attn_kernel(q_ref, kp_ref, kc_ref, vp_ref, vc_ref, o_ref, lse_ref):
    j = pl.program_id(2)
    tq = q_ref.shape[1]
    sub = N_KEYS - 1
    assert kp_ref.shape[1] == sub and tq % sub == 0
    k_all = jnp.concatenate([kp_ref[0], kc_ref[0]], axis=0)
    v_all = jnp.concatenate([vp_ref[0], vc_ref[0]], axis=0)
    n_h = HEADS_PER_GROUP
    row = lax.broadcasted_iota(jnp.int32, (n_h * sub, 2 * sub), 0) % sub
    col = lax.broadcasted_iota(jnp.int32, (n_h * sub, 2 * sub), 1)
    band = (col >= row) & (col <= row + sub)
    head_rows = lax.broadcasted_iota(jnp.int32, (n_h * sub, GROUP_WIDTH), 0) // sub
    head_lanes = lax.broadcasted_iota(jnp.int32, (n_h * sub, GROUP_WIDTH), 1) // HEAD_DIM
    own = head_rows == head_lanes
    lane_l = lax.broadcasted_iota(jnp.int32, (sub, 128), 1)
    for i in range(tq // sub):
        q = q_ref[0, i * sub:(i + 1) * sub, :]
        kk = k_all[i * sub:(i + 2) * sub]
        vv = v_all[i * sub:(i + 2) * sub]
        valid = band & ((j > 0) | (col >= sub)) if i == 0 else band
        q4 = jnp.concatenate([q] * n_h, axis=0)
        q4 = jnp.where(own, q4, jnp.zeros_like(q4))
        s = jnp.where(valid, _dot_nt(q4, kk), NEG)
        m = jnp.max(s, axis=1, keepdims=True)
        p = jnp.exp(s - m)
        l = jnp.sum(p, axis=1, keepdims=True)
        o4 = jnp.where(own, _dot(p.astype(BF16), vv) / l, 0.0)
        lse4 = m + jnp.log(l)
        o_acc = o4[0:sub]
        lse_acc = jnp.zeros((sub, 128), F32)
        for h in range(n_h):
            if h:
                o_acc = o_acc + o4[h * sub:(h + 1) * sub]
            lse_acc = jnp.where(lane_l // 32 == h, lse4[h * sub:(h + 1) * sub], lse_acc)
        o_ref[0, i * sub:(i + 1) * sub, :] = o_acc.astype(o_ref.dtype)
        lse_ref[0, i * sub:(i + 1) * sub, :] = lse_acc


def _win_attn(q, k, v, *, tq=512):
    b, dil, ts, _ = q.shape
    tq = min(tq, ts)
    sub = N_KEYS - 1
    cur = pl.BlockSpec((None, 1, tq, GROUP_WIDTH), lambda bi, r, j: (bi, r, j, 0))
    prev = pl.BlockSpec((None, 1, sub, GROUP_WIDTH),
                        lambda bi, r, j: (bi, r, jnp.maximum(j * (tq // sub) - 1, 0), 0))
    return pl.pallas_call(
        _win_attn_kernel,
        grid=(b, dil, ts // tq),
        in_specs=[cur, prev, cur, prev, cur],
        out_specs=[pl.BlockSpec((None, 1, tq, GROUP_WIDTH), lambda bi, r, j: (bi, r, j, 0)),
                   pl.BlockSpec((None, 1, tq, 128), lambda bi, r, j: (bi, r, j, 0))],
        out_shape=[jax.ShapeDtypeStruct((b, dil, ts, GROUP_WIDTH), BF16),
                   jax.ShapeDtypeStruct((b, dil, ts, 128), F32)],
        compiler_params=_cparams("arbitrary", "arbitrary", "arbitrary"),
        name="window_attention",
    )(q, k, k, v, v)


def _dec_attn_kernel(q_ref, k_ref, v_ref, c0_ref, c1_ref, c2_ref, att_ref, o0_ref, o1_ref, o2_ref,
                     e0_ref, e1_ref, e2_ref):
    t_new = q_ref.shape[1]
    n_h = HEADS_PER_GROUP
    q = q_ref[0]
    k_new = k_ref[0]
    v_new = v_ref[0]
    n_q = n_h * t_new
    own = (lax.broadcasted_iota(jnp.int32, (n_q, GROUP_WIDTH), 0) // t_new
           == lax.broadcasted_iota(jnp.int32, (n_q, GROUP_WIDTH), 1) // HEAD_DIM)
    outs, lses = [], []
    for g, ((window, dil), c_ref, o_ref, e_ref) in enumerate(
            zip(ATT_GROUPS, (c0_ref, c1_ref, c2_ref), (o0_ref, o1_ref, o2_ref), (e0_ref, e1_ref, e2_ref))):
        w = c_ref.shape[1]
        cols = slice(g * GROUP_WIDTH, (g + 1) * GROUP_WIDTH)
        kv_new = jnp.concatenate([k_new[:, cols], v_new[:, cols]], axis=1)
        n_chunk = e_ref.shape[0]
        pad_rows = e_ref.shape[1] - (w + t_new)
        if pad_rows:
            @pl.when(pl.program_id(0) == 0)
            def _(e_ref=e_ref, w=w, pad_rows=pad_rows):
                e_ref[:, w + t_new:, :] = jnp.zeros((n_chunk, pad_rows, 128), F32)
        for c in range(n_chunk):
            e_ref[c, 0:w, :] = c_ref[0, :, c * 128:(c + 1) * 128]
            e_ref[c, w:w + t_new, :] = kv_new[:, c * 128:(c + 1) * 128]
            o_ref[0, :, c * 128:(c + 1) * 128] = e_ref[c, t_new:w + t_new, :]
        classes = min(dil, t_new)
        parts = [jnp.concatenate([e_ref[c, pl.ds(r, DEC_SPAN, stride=dil), :] for c in range(n_chunk)], axis=1)
                 for r in range(classes)]
        kv_all = jnp.concatenate(parts, axis=0) if classes > 1 else parts[0]
        k_all = kv_all[:, 0:GROUP_WIDTH].astype(BF16)
        v_all = kv_all[:, GROUP_WIDTH:].astype(BF16)
        q4 = jnp.where(own, jnp.concatenate([q[:, cols]] * n_h, axis=0), 0.0)
        s = _dot_nt(q4.astype(BF16), k_all)
        t_idx = lax.broadcasted_iota(jnp.int32, s.shape, 0) % t_new
        col = lax.broadcasted_iota(jnp.int32, s.shape, 1)
        first = t_idx // dil
        step_i = col % DEC_SPAN
        valid = (col // DEC_SPAN == t_idx % dil) & (step_i >= first) & (step_i <= first + (N_KEYS - 1))
        s = jnp.where(valid, s, NEG)
        m = jnp.max(s, axis=1, keepdims=True)
        p = jnp.exp(s - m)
        l = jnp.sum(p, axis=1, keepdims=True)
        outs.append(_dot(p.astype(BF16), v_all) / l)
        lses.append(m + jnp.log(l))
    top = jnp.maximum(jnp.maximum(lses[0], lses[1]), lses[2])
    es = [jnp.exp(l - top) for l in lses]
    tot = es[0] + es[1] + es[2]
    for g in range(len(ATT_GROUPS)):
        weighted = jnp.where(own, outs[g] * (es[g] / tot), 0.0)
        acc = weighted[0:t_new]
        for h in range(1, n_h):
            acc = acc + weighted[h * t_new:(h + 1) * t_new]
        att_ref[0, :, g * GROUP_WIDTH:(g + 1) * GROUP_WIDTH] = acc


DEC_SPAN = 136


def _dec_attn(q, k, v, caches):
    b, t_new, _ = q.shape
    assert N_KEYS + t_new - 1 <= DEC_SPAN
    tok = pl.BlockSpec((1, t_new, ATT_WIDTH), lambda i: (i, 0, 0))
    cspec = [pl.BlockSpec((1,) + c.shape[1:], lambda i: (i, 0, 0)) for c in caches]

    def ext_rows(c, dil):
        return max(c.shape[1] + t_new, (min(dil, t_new) - 1) + (DEC_SPAN - 1) * dil + 1)

    return pl.pallas_call(
        _dec_attn_kernel,
        grid=(b,),
        in_specs=[tok, tok, tok] + cspec,
        out_specs=[tok] + cspec,
        out_shape=[jax.ShapeDtypeStruct(q.shape, F32)] + [jax.ShapeDtypeStruct(c.shape, F32) for c in caches],
        scratch_shapes=[pltpu.VMEM((c.shape[2] // 128, ext_rows(c, dil), 128), F32)
                        for c, (_, dil) in zip(caches, ATT_GROUPS)],
        compiler_params=_cparams("arbitrary"),
        name="decode_attention",
    )(q, k, v, *caches)


def _mlstm_kernel(m3_ref, conv0_ref, c0_ref, n0_ref, m0_ref, *rest):
    consts, (hm_ref, convo_ref, co_ref, no_ref, mo_ref, cbuf, c_s, n_s, m_s) = rest[:10], rest[10:]
    for b in range(m3_ref.shape[0]):
        _mlstm_chunk(m3_ref.at[b], conv0_ref.at[b], c0_ref.at[b], n0_ref.at[b], m0_ref.at[b], *consts,
                     hm_ref.at[b], convo_ref.at[b], co_ref.at[b], no_ref.at[b], mo_ref.at[b],
                     cbuf.at[b], c_s.at[b], n_s.at[b], m_s.at[b])


def _mlstm_chunk(m3_ref, conv0_ref, c0_ref, n0_ref, m0_ref, convw_ref, convb_ref, wq_ref, wk_ref,
                 wif_ref, wift_ref, bif_ref, bift_ref, normw_ref, skip_ref,
                 hm_ref, convo_ref, co_ref, no_ref, mo_ref, cbuf, c_s, n_s, m_s):
    j = pl.program_id(1)
    L = m3_ref.shape[0]
    W = MLSTM_WIDTH
    D = HEAD_DIM

    @pl.when(j == 0)
    def _():
        cbuf[0:8, :] = jnp.zeros((8, W), F32)
        cbuf[8 - (CONV_WIDTH - 1):8, :] = conv0_ref[...]
        c_s[...] = c0_ref[...]
        n_s[...] = n0_ref[...]
        m_s[...] = m0_ref[...]

    blk = m3_ref[...]
    c_in = blk[:, 0:W]
    v_m = blk[:, W:2 * W]
    z = blk[:, 2 * W:3 * W]
    cbuf[8:8 + L, :] = c_in
    acc = jnp.zeros((L, W), F32) + convb_ref[...]
    for tap in range(CONV_WIDTH):
        off = 8 - (CONV_WIDTH - 1) + tap
        acc = acc + cbuf[off:off + L, :] * convw_ref[tap:tap + 1, :]
    xc = _silu(acc)
    convo_ref[...] = cbuf[8 + L - (CONV_WIDTH - 1):8 + L, :]
    cbuf[0:8, :] = cbuf[L:L + 8, :]

    xcb = xc.astype(BF16)
    q_m = _dot(xcb, wq_ref[...])
    k_m = _dot(xcb, wk_ref[...])
    gate_in = jnp.concatenate([q_m, k_m, v_m], axis=1)
    nn, nt = ((1,), (0,)), ((1,), (1,))
    g_col = _dot_split(gate_in, wif_ref[...], nn) + bif_ref[...]
    g_row = _dot_split(wift_ref[...], gate_in, nt) + bift_ref[...]
    i_col, lf_col = g_col[:, 0:MLSTM_HEADS], _log_sigmoid(g_col[:, MLSTM_HEADS:])
    i_row, lf_row = g_row[0:MLSTM_HEADS, :], _log_sigmoid(g_row[MLSTM_HEADS:, :])
    rr = lax.broadcasted_iota(jnp.int32, (L, L), 0)
    cc = lax.broadcasted_iota(jnp.int32, (L, L), 1)
    causal = cc <= rr
    tri = causal.astype(F32)
    b_col = _dot_split(tri, lf_col, nn, a_exact=True)
    b_row = _dot_split(lf_row, tri, nt, b_exact=True)
    ks = k_m * (D ** -0.5)
    kb = ks.astype(BF16)
    vb = v_m.astype(BF16)
    n_h = MLSTM_HEADS
    heads = range(n_h)
    own = (lax.broadcasted_iota(jnp.int32, (n_h * L, W), 0) // L
           == lax.broadcasted_iota(jnp.int32, (n_h * L, W), 1) // D)
    q4 = jnp.where(own, jnp.concatenate([q_m] * n_h, axis=0), 0.0)
    q4b = q4.astype(BF16)
    m_prev = m_s[...]
    per_row = lambda a: jnp.concatenate([a[:, h:h + 1] for h in heads], axis=0)
    per_key = lambda a: jnp.concatenate([jnp.broadcast_to(a[h:h + 1, :], (L, L)) for h in heads], axis=0)
    bc = per_row(b_col)
    m_prev_r = jnp.concatenate([jnp.broadcast_to(m_prev[:, h:h + 1], (L, 1)) for h in heads], axis=0)
    causal4 = (lax.broadcasted_iota(jnp.int32, (n_h * L, L), 1)
               <= lax.broadcasted_iota(jnp.int32, (n_h * L, L), 0) % L)
    d_intra = jnp.where(causal4, bc - per_key(b_row) + per_key(i_row), -jnp.inf)
    a_inter = bc + m_prev_r
    m_t = jnp.maximum(a_inter, jnp.max(d_intra, axis=1, keepdims=True))
    s = _dot_nt(q4b, kb) * jnp.exp(d_intra - m_t)
    w_inter = jnp.exp(a_inter - m_t)
    c_all = c_s[...]
    c_bd = jnp.where(lax.broadcasted_iota(jnp.int32, (W, W), 0) // D == lax.broadcasted_iota(jnp.int32, (W, W), 1) // D,
                     jnp.concatenate([c_all] * n_h, axis=1), 0.0)
    n_all = n_s[...]
    n_flat = jnp.concatenate([n_all[h:h + 1, :] for h in heads], axis=1)
    num = _dot(s.astype(BF16), vb) + w_inter * _dot_nt(q4b, c_bd.astype(BF16))
    den = jnp.sum(s, axis=1, keepdims=True) + w_inter * jnp.sum(q4 * n_flat, axis=1, keepdims=True)
    hh = jnp.where(own, num / jnp.maximum(jnp.abs(den), jnp.exp(-m_t)), 0.0)
    mu = jnp.sum(hh, axis=1, keepdims=True) * (1.0 / D)
    hc = jnp.where(own, hh - mu, 0.0)
    var = jnp.sum(hc * hc, axis=1, keepdims=True) * (1.0 / D)
    hn = hc * lax.rsqrt(var + LN_EPS)
    h_all = hn[0:L]
    for h in range(1, n_h):
        h_all = h_all + hn[h * L:(h + 1) * L]
    g_tot = b_col[L - 1:L, :]
    a_end = g_tot + m_prev
    d_end = g_tot - b_col + i_col
    m_new = jnp.maximum(a_end, jnp.max(d_end, axis=0, keepdims=True))
    w_s = jnp.exp(d_end - m_new)
    decay = jnp.exp(a_end - m_new)
    w_s_l = jnp.concatenate([jnp.broadcast_to(w_s[:, h:h + 1], (L, D)) for h in heads], axis=1)
    upd = _dot_tn((v_m * w_s_l).astype(BF16), kb)
    decay_r = jnp.concatenate([jnp.broadcast_to(decay[:, h:h + 1], (D, 1)) for h in heads], axis=0)
    c_s[...] = decay_r * c_all + jnp.concatenate([upd[h * D:(h + 1) * D, h * D:(h + 1) * D] for h in heads], axis=0)
    k_sum = jnp.sum(w_s_l * ks, axis=0, keepdims=True)
    n_s[...] = (jnp.concatenate([jnp.broadcast_to(decay[:, h:h + 1], (1, D)) for h in heads], axis=0) * n_all
                + jnp.concatenate([k_sum[:, h * D:(h + 1) * D] for h in heads], axis=0))
    m_s[...] = m_new
    hm = (h_all * normw_ref[...] + skip_ref[...] * xc) * _silu(z)
    hm_ref[...] = hm.astype(hm_ref.dtype)
    co_ref[...] = c_s[...]
    no_ref[...] = n_s[...]
    mo_ref[...] = m_s[...]


def _mlstm(m3, conv0, c0, n0, m0, conv_w, conv_b, wq_bd, wk_bd, w_if, b_if, norm_w, skip, *, chunk, bb):
    b, t, _ = m3.shape
    W = MLSTM_WIDTH
    full = lambda a: pl.BlockSpec(a.shape, lambda bi, j: (0,) * a.ndim)
    per_b = lambda a: pl.BlockSpec((bb,) + a.shape[1:], lambda bi, j: (bi,) + (0,) * (a.ndim - 1))
    consts = [conv_w, conv_b.reshape(1, W), wq_bd, wk_bd, w_if, w_if.T, b_if.reshape(1, -1),
              b_if.reshape(-1, 1), norm_w.reshape(1, W), skip.reshape(1, W)]
    states = [conv0, c0, n0, m0]
    return pl.pallas_call(
        _mlstm_kernel,
        grid=(b // bb, t // chunk),
        in_specs=[pl.BlockSpec((bb, chunk, 3 * W), lambda bi, j: (bi, j, 0))] + [per_b(s) for s in states]
                 + [full(c) for c in consts],
        out_specs=[pl.BlockSpec((bb, chunk, W), lambda bi, j: (bi, j, 0))] + [per_b(s) for s in states],
        out_shape=[jax.ShapeDtypeStruct((b, t, W), BF16)] + [jax.ShapeDtypeStruct(s.shape, F32) for s in states],
        scratch_shapes=[pltpu.VMEM((bb, chunk + 8, W), F32), pltpu.VMEM((bb,) + c0.shape[1:], F32),
                        pltpu.VMEM((bb,) + n0.shape[1:], F32), pltpu.VMEM((bb,) + m0.shape[1:], F32)],
        compiler_params=_cparams("arbitrary", "arbitrary"),
        name="mlstm",
    )(m3, *states, *consts)


def _interleave(ref, scr):
    dil, rows, width = ref.shape
    if dil == 1:
        return ref[0].astype(F32)
    for r in range(dil):
        for c in range(width // 128):
            scr[c, pl.ds(r, rows, stride=dil), :] = ref[r, :, c * 128:(c + 1) * 128].astype(F32)
    return jnp.concatenate([scr[c] for c in range(width // 128)], axis=1)


def _combine(os, lses):
    tm = os[0].shape[0]
    lane = lax.broadcasted_iota(jnp.int32, (tm, GROUP_WIDTH), 1)

    def spread(l2):
        out = jnp.zeros((tm, GROUP_WIDTH), F32)
        for h in range(HEADS_PER_GROUP):
            out = jnp.where(lane // HEAD_DIM == h, l2[:, 32 * h:32 * h + 1], out)
        return out

    ls = [spread(l) for l in lses]
    top = jnp.maximum(jnp.maximum(ls[0], ls[1]), ls[2])
    es = [jnp.exp(l - top) for l in ls]
    tot = es[0] + es[1] + es[2]
    return [(o * (e / tot)).astype(BF16) for o, e in zip(os, es)]


def _outproj_kernel(combine, alpha, *refs):
    if combine:
        (o0, o1, o2, l0, l1, l2, hm_ref, x_ref, wo_ref, g_ref, b_ref, wq_ref, x1_ref, qc_ref, scr) = refs
        att = _combine([_interleave(o, scr) for o in (o0, o1, o2)], [_interleave(l, scr) for l in (l0, l1, l2)])
    else:
        (a_ref, hm_ref, x_ref, wo_ref, g_ref, b_ref, wq_ref, x1_ref, qc_ref) = refs
        att = [a_ref[:, g * GROUP_WIDTH:(g + 1) * GROUP_WIDTH].astype(BF16) for g in range(len(ATT_GROUPS))]
    mix = _dot(hm_ref[...].astype(BF16), wo_ref[ATT_WIDTH:, :])
    for g, a in enumerate(att):
        mix = mix + _dot(a, wo_ref[g * GROUP_WIDTH:(g + 1) * GROUP_WIDTH, :])
    x1 = _layer_norm(alpha * x_ref[...] + mix, g_ref[...], b_ref[...])
    x1_ref[...] = x1
    qc_ref[...] = (_dot(x1.astype(BF16), wq_ref[...]) * ((x1.shape[1] // MEM_HEADS) ** -0.5)).astype(qc_ref.dtype)


def _outproj(att_parts, hm, x, w_out, ln_g, ln_b, w_cq, alpha, *, tm):
    n, d = x.shape
    combine = len(att_parts) > 1
    row = lambda a: pl.BlockSpec((tm, a.shape[1]), lambda i: (i, 0))
    full = lambda a: pl.BlockSpec(a.shape, lambda i: (0, 0))

    def split(a):
        _, dil, ts, w = a.shape
        tiles_per_seq = ts * dil // tm
        return pl.BlockSpec((None, dil, tm // dil, w), lambda i: (i // tiles_per_seq, 0, i % tiles_per_seq, 0))

    consts = [w_out, ln_g.reshape(1, d), ln_b.reshape(1, d), w_cq]
    return pl.pallas_call(
        functools.partial(_outproj_kernel, combine, alpha),
        grid=(n // tm,),
        in_specs=[split(a) if combine else row(a) for a in att_parts] + [row(hm), row(x)] + [full(c) for c in consts],
        out_specs=[pl.BlockSpec((tm, d), lambda i: (i, 0))] * 2,
        out_shape=[jax.ShapeDtypeStruct((n, d), F32), jax.ShapeDtypeStruct((n, d), BF16)],
        scratch_shapes=[pltpu.VMEM((GROUP_WIDTH // 128, tm, 128), F32)] if combine else [],
        compiler_params=_cparams("arbitrary"),
        name="outproj_ln1",
    )(*att_parts, hm, x, *consts)


def _memkv_kernel(mem_ref, wk_ref, wv_ref, kv_ref):
    d = mem_ref.shape[2]
    mem = mem_ref[0].astype(BF16)
    kv_ref[0, :, 0:d] = _dot(mem, wk_ref[...])
    kv_ref[0, :, d:] = _dot(mem, wv_ref[...])


def _memkv(mem, w_ck, w_cv):
    b, m, d = mem.shape
    return pl.pallas_call(
        _memkv_kernel,
        grid=(b,),
        in_specs=[pl.BlockSpec((1, m, d), lambda i: (i, 0, 0)),
                  pl.BlockSpec(w_ck.shape, lambda i: (0, 0)), pl.BlockSpec(w_cv.shape, lambda i: (0, 0))],
        out_specs=pl.BlockSpec((1, m, 2 * d), lambda i: (i, 0, 0)),
        out_shape=jax.ShapeDtypeStruct((b, m, 2 * d), F32),
        compiler_params=_cparams("arbitrary"),
        name="memory_kv",
    )(mem, w_ck, w_cv)


def _xattn_kernel(q_ref, kv_ref, o_ref):
    d = q_ref.shape[2]
    hd = d // MEM_HEADS
    q = q_ref[0]
    for h in range(MEM_HEADS):
        k = kv_ref[0, :, h * hd:(h + 1) * hd].astype(BF16)
        v = kv_ref[0, :, d + h * hd:d + (h + 1) * hd].astype(BF16)
        s = _dot_nt(q[:, h * hd:(h + 1) * hd], k)
        p = jnp.exp(s - jnp.max(s, axis=1, keepdims=True))
        p = p / jnp.sum(p, axis=1, keepdims=True)
        o_ref[0, :, h * hd:(h + 1) * hd] = _dot(p.astype(BF16), v).astype(o_ref.dtype)


def _xattn(qc, kv, *, tq):
    b, t, d = qc.shape
    m = kv.shape[1]
    return pl.pallas_call(
        _xattn_kernel,
        grid=(b, t // tq),
        in_specs=[pl.BlockSpec((1, tq, d), lambda bi, j: (bi, j, 0)),
                  pl.BlockSpec((1, m, 2 * d), lambda bi, j: (bi, 0, 0))],
        out_specs=pl.BlockSpec((1, tq, d), lambda bi, j: (bi, j, 0)),
        out_shape=jax.ShapeDtypeStruct((b, t, d), BF16),
        compiler_params=_cparams("arbitrary", "arbitrary"),
        name="cross_attention",
    )(qc, kv)


def _xout_kernel(alpha, o_ref, x_ref, w_ref, g_ref, b_ref, y_ref):
    y_ref[...] = _layer_norm(alpha * x_ref[...] + _dot(o_ref[...], w_ref[...]), g_ref[...], b_ref[...])


def _xout(oc, x1, w_co, ln_g, ln_b, alpha, *, tm):
    n, d = x1.shape
    row = pl.BlockSpec((tm, d), lambda i: (i, 0))
    full = lambda a: pl.BlockSpec(a.shape, lambda i: (0, 0))
    consts = [w_co, ln_g.reshape(1, d), ln_b.reshape(1, d)]
    return pl.pallas_call(
        functools.partial(_xout_kernel, alpha),
        grid=(n // tm,),
        in_specs=[row, row] + [full(c) for c in consts],
        out_specs=row,
        out_shape=jax.ShapeDtypeStruct((n, d), F32),
        compiler_params=_cparams("arbitrary"),
        name="cross_out_ln2",
    )(oc, x1, *consts)


def _first_index_of_max(vals, idx, big):
    mx = jnp.max(vals, axis=0, keepdims=True)
    return mx, jnp.min(jnp.where(vals == mx, idx, big), axis=0, keepdims=True)


def _route(xb, wrt_ref, br_ref):
    tm = xb.shape[0]
    n_e = wrt_ref.shape[0]
    per_group = n_e // N_EXPERT_GROUPS
    logits = _dot_nt(wrt_ref[...], xb)
    scores = _sigmoid(logits)
    biased = scores + br_ref[...]
    e_idx = lax.broadcasted_iota(jnp.int32, (n_e, tm), 0).astype(F32)
    g_scores = []
    for g in range(N_EXPERT_GROUPS):
        sub = biased[g * per_group:(g + 1) * per_group, :]
        sidx = lax.broadcasted_iota(jnp.int32, (per_group, tm), 0).astype(F32)
        m1, a1 = _first_index_of_max(sub, sidx, per_group)
        m2 = jnp.max(jnp.where(sidx == a1, -jnp.inf, sub), axis=0, keepdims=True)
        g_scores.append(m1 + m2)
    gs = jnp.concatenate(g_scores, axis=0)
    g_idx = lax.broadcasted_iota(jnp.int32, (N_EXPERT_GROUPS, tm), 0).astype(F32)
    g_sel = jnp.zeros((N_EXPERT_GROUPS, tm), F32)
    work = gs
    for _ in range(TOPK_GROUPS):
        _, a = _first_index_of_max(work, g_idx, N_EXPERT_GROUPS)
        hit = g_idx == a
        g_sel = jnp.where(hit, 1.0, g_sel)
        work = jnp.where(hit, -jnp.inf, work)
    e_mask = jnp.concatenate(
        [jnp.broadcast_to(g_sel[g:g + 1, :], (per_group, tm)) for g in range(N_EXPERT_GROUPS)], axis=0)
    work = jnp.where(e_mask > 0.5, biased, -jnp.inf)
    sel = jnp.zeros((n_e, tm), F32)
    picks = []
    for _ in range(TOP_K):
        cand = jnp.where(sel > 0.5, -jnp.inf, work)
        mx = jnp.max(cand, axis=0, keepdims=True)
        a = jnp.min(jnp.where((cand == mx) & (sel < 0.5), e_idx, float(n_e)), axis=0, keepdims=True)
        pick = jnp.where(e_idx == a, 1.0, 0.0)
        picks.append(pick)
        sel = sel + pick
    w_sel = sel * scores
    gates_t = w_sel / jnp.sum(w_sel, axis=0, keepdims=True) * ROUTED_SCALE
    return gates_t, sel, picks


ROW_CHUNK = 16
TILE_CHUNKS = 32
TILE_ROWS = ROW_CHUNK * TILE_CHUNKS
TOKEN_BLOCK = 256
GATHER_AHEAD = 3
TILE_SLOTS = GATHER_AHEAD + 1
MAX_SEGMENT_PIECES = TOKEN_BLOCK // ROW_CHUNK
assert MAX_SEGMENT_PIECES <= TILE_SLOTS * TILE_CHUNKS


def _slab_rows(tb, n_e):
    return -(-(TOP_K * tb + n_e * (ROW_CHUNK - 1)) // 128) * 128


def _block_of(refs, first_blocks):
    i = pl.program_id(0)
    x = refs[0][...]
    for ref, first in zip(refs[1:], first_blocks[1:]):
        x = jnp.where(i >= first, ref[...], x)
    return x


def _multi_specs(arrays, tb):
    firsts, specs, start = [], [], 0
    for a in arrays:
        nb = a.shape[0] // tb
        firsts.append(start)
        specs.append(pl.BlockSpec((tb, a.shape[1]),
                                  lambda i, start=start, nb=nb: (jnp.clip(i - start, 0, nb - 1), 0)))
        start += nb
    return firsts, specs, start


def _dispatch_kernel(first_blocks, *refs):
    n_x = len(first_blocks)
    x_refs, (wrt_ref, br_ref, xs_ref, pwt_ref, meta_ref) = refs[:n_x], refs[n_x:]
    tb = x_refs[0].shape[0]
    n_e = wrt_ref.shape[0]
    slab = xs_ref.shape[1]
    xb = _block_of(x_refs, first_blocks).astype(BF16)
    gates_t, sel, picks = _route(xb, wrt_ref, br_ref)
    cnt = jnp.sum(sel, axis=1, keepdims=True)
    padded = jnp.floor((cnt + (ROW_CHUNK - 1)) * (1.0 / ROW_CHUNK)) * ROW_CHUNK
    padded_b = jnp.broadcast_to(padded, (n_e, 128))
    er = lax.broadcasted_iota(jnp.int32, (n_e, n_e), 0)
    ec = lax.broadcasted_iota(jnp.int32, (n_e, n_e), 1)
    off_b = _dot((ec < er).astype(BF16), padded_b.astype(BF16))
    tr = lax.broadcasted_iota(jnp.int32, (tb, tb), 0)
    tc = lax.broadcasted_iota(jnp.int32, (tb, tb), 1)
    rank = _dot(sel.astype(BF16), (tr < tc).astype(BF16))
    dest = off_b[:, 0:1] + rank
    dest_k = [jnp.sum(p * dest, axis=0, keepdims=True) for p in picks]
    gate_k = [jnp.sum(p * gates_t, axis=0, keepdims=True) for p in picks]
    meta_ref[0, 0] = padded_b
    meta_ref[0, 1] = off_b

    stacked = jnp.concatenate(dest_k + [jnp.zeros((8 - TOP_K, tb), F32)] + gate_k
                              + [jnp.zeros((128 - 8 - TOP_K, tb), F32)], axis=0)
    cols = jnp.transpose(stacked)
    step = 512
    for c0 in range(0, slab, step):
        lanes = lax.broadcasted_iota(jnp.int32, (tb, step), 1).astype(F32) + float(c0)
        w = jnp.zeros((tb, step), F32)
        for k in range(TOP_K):
            w = jnp.where(lanes == cols[:, k:k + 1], cols[:, 8 + k:9 + k], w)
        pwt_ref[0, :, c0:c0 + step] = w.astype(pwt_ref.dtype)
        onehot_t = jnp.where(w != 0.0, 1.0, 0.0).astype(BF16)
        xs_ref[0, c0:c0 + step, :] = _dot_tn(onehot_t, xb).astype(xs_ref.dtype)


def _dispatch(xs_in, w_router_t, b_router):
    d = xs_in[0].shape[1]
    n_e = w_router_t.shape[0]
    tb = TOKEN_BLOCK
    firsts, x_specs, nblk = _multi_specs(xs_in, tb)
    slab = _slab_rows(tb, n_e)
    assert slab % 512 == 0
    return pl.pallas_call(
        functools.partial(_dispatch_kernel, tuple(firsts)),
        grid=(nblk,),
        in_specs=x_specs + [pl.BlockSpec(w_router_t.shape, lambda i: (0, 0)),
                            pl.BlockSpec((n_e, 1), lambda i: (0, 0))],
        out_specs=[pl.BlockSpec((1, slab, d), lambda i: (i, 0, 0)),
                   pl.BlockSpec((1, tb, slab), lambda i: (i, 0, 0)),
                   pl.BlockSpec((1, 2, n_e, 128), lambda i: (i, 0, 0, 0))],
        out_shape=[jax.ShapeDtypeStruct((nblk, slab, d), BF16),
                   jax.ShapeDtypeStruct((nblk, tb, slab), BF16),
                   jax.ShapeDtypeStruct((nblk, 2, n_e, 128), F32)],
        compiler_params=_cparams("arbitrary"),
        name="moe_dispatch",
    )(*xs_in, w_router_t, b_router.reshape(n_e, 1))


def _schedule_kernel(n_e, nblk, t_max, zero_row, slot_ref, first_ref, total_ref, tile0_ref,
                     te_ref, nv_ref, row_ref, nt_ref):
    pos = (lax.broadcasted_iota(jnp.int32, row_ref.shape, 0) * 128
           + lax.broadcasted_iota(jnp.int32, row_ref.shape, 1))
    pos_row = pos * ROW_CHUNK

    def per_expert(e, table):
        def per_block(blk, tbl):
            idx = e * nblk + blk
            s = slot_ref[idx]
            return jnp.where(pos >= s, pos_row + (first_ref[idx] - s * ROW_CHUNK), tbl)
        table = lax.fori_loop(0, nblk, per_block, table)
        return jnp.where(pos >= tile0_ref[e] * TILE_CHUNKS + total_ref[e], zero_row, table)
    row_ref[...] = lax.fori_loop(0, n_e, per_expert, jnp.full(row_ref.shape, zero_row, jnp.int32))

    tile = (lax.broadcasted_iota(jnp.int32, te_ref.shape, 0) * 128
            + lax.broadcasted_iota(jnp.int32, te_ref.shape, 1))

    def per_expert_tiles(e, carry):
        te, nv = carry
        t0 = tile0_ref[e]
        hit = tile >= t0
        left = jnp.clip(total_ref[e] - (tile - t0) * TILE_CHUNKS, 0, TILE_CHUNKS)
        return jnp.where(hit, e, te), jnp.where(hit, left, nv)
    zeros = jnp.zeros(te_ref.shape, jnp.int32)
    te, nv = lax.fori_loop(0, n_e, per_expert_tiles, (zeros, zeros))
    te_ref[...] = te
    nv_ref[...] = nv
    last = n_e - 1
    nt_ref[0] = tile0_ref[last] + (total_ref[last] + (TILE_CHUNKS - 1)) // TILE_CHUNKS


def _tile_schedule(meta, slab):
    nblk, _, n_e, _ = meta.shape
    pieces = (meta[:, 0, :, 0].astype(jnp.int32) // ROW_CHUNK).T
    first_row = (meta[:, 1, :, 0].astype(jnp.int32) + (jnp.arange(nblk, dtype=jnp.int32) * slab)[:, None]).T
    before = jnp.cumsum(pieces, axis=1) - pieces
    total = jnp.sum(pieces, axis=1)
    tiles = (total + (TILE_CHUNKS - 1)) // TILE_CHUNKS
    tile0 = jnp.cumsum(tiles) - tiles
    slot = tile0[:, None] * TILE_CHUNKS + before
    t_max = -(-(TOP_K * nblk * TOKEN_BLOCK + nblk * n_e * (ROW_CHUNK - 1)) // TILE_ROWS) + n_e + TILE_SLOTS
    assert slab - ROW_CHUNK >= TOP_K * TOKEN_BLOCK + n_e * (ROW_CHUNK - 1)
    smem = pl.BlockSpec(memory_space=pltpu.SMEM)
    vregs = lambda n: jax.ShapeDtypeStruct((-(-n // 1024) * 8, 128), jnp.int32)
    te, nv, rows, n_tiles = pl.pallas_call(
        functools.partial(_schedule_kernel, n_e, nblk, t_max, slab - ROW_CHUNK),
        in_specs=[smem] * 4,
        out_specs=[pl.BlockSpec(memory_space=pltpu.VMEM)] * 3 + [smem],
        out_shape=[vregs(t_max), vregs(t_max), vregs(t_max * TILE_CHUNKS), jax.ShapeDtypeStruct((1,), jnp.int32)],
        name="moe_schedule",
    )(slot.reshape(-1), first_row.reshape(-1), total, tile0)
    return te.reshape(-1)[:t_max], nv.reshape(-1)[:t_max], rows.reshape(-1)[:t_max * TILE_CHUNKS], n_tiles


def _grouped_kernel(te_ref, nv_ref, row_ref, nt_ref, xs_hbm, wg_ref, wu_ref, wd_ref, ys_hbm,
                    lhs, obuf, wg_b, wu_b, wd_b, sem_in, sem_out):
    t = pl.program_id(0)
    n_tiles = nt_ref[0]

    def piece(ref, tile, s):
        row = pl.multiple_of(row_ref[tile * TILE_CHUNKS + s], ROW_CHUNK)
        return ref.at[pl.ds(row, ROW_CHUNK)]

    def local(buf, slot, s):
        start = s * ROW_CHUNK if isinstance(s, int) else pl.multiple_of(s * ROW_CHUNK, ROW_CHUNK)
        return buf.at[slot, pl.ds(start, ROW_CHUNK)]

    def gather(tile, s):
        slot = tile % TILE_SLOTS
        return pltpu.make_async_copy(piece(xs_hbm, tile, s), local(lhs, slot, s), sem_in.at[slot])

    def scatter(tile, s):
        slot = tile % TILE_SLOTS
        return pltpu.make_async_copy(local(obuf, slot, s), piece(ys_hbm, tile, s), sem_out.at[slot])

    def gathered_tile(slot):
        return pltpu.make_async_copy(xs_hbm.at[pl.ds(0, TILE_ROWS)], lhs.at[slot], sem_in.at[slot])

    def scattered_tile(slot):
        return pltpu.make_async_copy(obuf.at[slot], ys_hbm.at[pl.ds(0, TILE_ROWS)], sem_out.at[slot])

    def for_valid_pieces(tile, fn):
        def body(s, carry):
            fn(tile, s)
            return carry
        lax.fori_loop(0, nv_ref[tile], body, 0)

    @pl.when(t == 0)
    def _():
        for tile in range(GATHER_AHEAD):
            for s in range(TILE_CHUNKS):
                gather(tile, s).start()

    @pl.when((t >= TILE_SLOTS) & (t - TILE_SLOTS < n_tiles))
    def _():
        full = nv_ref[t - TILE_SLOTS] == TILE_CHUNKS

        @pl.when(full)
        def _():
            scattered_tile(t % TILE_SLOTS).wait()

        @pl.when(jnp.logical_not(full))
        def _():
            for_valid_pieces(t - TILE_SLOTS, lambda tile, s: scatter(tile, s).wait())

    @pl.when((t < n_tiles) & ((t == 0) | (te_ref[t] != te_ref[jnp.maximum(t - 1, 0)])))
    def _():
        wg_b[...] = wg_ref[0].astype(BF16)
        wu_b[...] = wu_ref[0].astype(BF16)
        wd_b[...] = wd_ref[0].astype(BF16)

    @pl.when(t < n_tiles)
    def _():
        slot = t % TILE_SLOTS
        gathered_tile(slot).wait()
        for s in range(TILE_CHUNKS):
            gather(t + GATHER_AHEAD, s).start()
        x = lhs[slot]
        h = _silu(_dot(x, wg_b[...])) * _dot(x, wu_b[...])
        obuf[slot] = _dot(h.astype(BF16), wd_b[...]).astype(obuf.dtype)

    @pl.when((t >= n_tiles) & (t < n_tiles + GATHER_AHEAD))
    def _():
        gathered_tile(t % TILE_SLOTS).wait()

    @pl.when(t < n_tiles)
    def _():
        full = nv_ref[t] == TILE_CHUNKS

        @pl.when(full)
        def _():
            for s in range(TILE_CHUNKS):
                scatter(t, s).start()

        @pl.when(jnp.logical_not(full))
        def _():
            for_valid_pieces(t, lambda tile, s: scatter(tile, s).start())


def _grouped_experts(xs, tile_e, n_valid, rows_tbl, n_tiles, w_gate, w_up, w_down):
    rows, d = xs.shape
    n_e, _, ff = w_gate.shape
    t_max = tile_e.shape[0]
    grid_spec = pltpu.PrefetchScalarGridSpec(
        num_scalar_prefetch=4,
        grid=(t_max,),
        in_specs=[pl.BlockSpec(memory_space=pl.ANY),
                  pl.BlockSpec((1, d, ff), lambda t, te, nv, sr, nt: (te[t], 0, 0)),
                  pl.BlockSpec((1, d, ff), lambda t, te, nv, sr, nt: (te[t], 0, 0)),
                  pl.BlockSpec((1, ff, d), lambda t, te, nv, sr, nt: (te[t], 0, 0))],
        out_specs=pl.BlockSpec(memory_space=pl.ANY),
        scratch_shapes=[pltpu.VMEM((TILE_SLOTS, TILE_ROWS, d), BF16), pltpu.VMEM((TILE_SLOTS, TILE_ROWS, d), BF16),
                        pltpu.VMEM((d, ff), BF16), pltpu.VMEM((d, ff), BF16), pltpu.VMEM((ff, d), BF16),
                        pltpu.SemaphoreType.DMA((TILE_SLOTS,)), pltpu.SemaphoreType.DMA((TILE_SLOTS,))],
    )
    return pl.pallas_call(
        _grouped_kernel,
        grid_spec=grid_spec,
        out_shape=jax.ShapeDtypeStruct((rows, d), BF16),
        input_output_aliases={4: 0},
        compiler_params=_cparams("arbitrary"),
        name="moe_grouped_experts",
    )(tile_e, n_valid, rows_tbl, n_tiles, xs, w_gate, w_up, w_down)


def _moe_out_kernel(alpha, first_blocks, pwt_ref, ys_ref, *refs):
    n_x = len(first_blocks)
    x_refs, (sg_ref, su_ref, sd_ref, g_ref, b_ref), y_refs = refs[:n_x], refs[n_x:n_x + 5], refs[n_x + 5:]
    i = pl.program_id(0)
    routed = _dot(pwt_ref[0], ys_ref[0])
    x = _block_of(x_refs, first_blocks)
    xb = x.astype(BF16)
    hs = _silu(_dot(xb, sg_ref[...])) * _dot(xb, su_ref[...])
    shared = _dot(hs.astype(BF16), sd_ref[...])
    y = _layer_norm(alpha * x + (routed + shared), g_ref[...], b_ref[...])
    bounds = list(first_blocks[1:]) + [pl.num_programs(0)]
    for y_ref, lo, hi in zip(y_refs, first_blocks, bounds):
        @pl.when((i >= lo) & (i < hi))
        def _(y_ref=y_ref):
            y_ref[...] = y


def _moe_out(pwt, ys, xs_in, ws_gate, ws_up, ws_down, ln_g, ln_b, alpha):
    d = xs_in[0].shape[1]
    nblk, tb, slab = pwt.shape
    firsts, x_specs, _ = _multi_specs(xs_in, tb)
    consts = [ws_gate, ws_up, ws_down, ln_g.reshape(1, d), ln_b.reshape(1, d)]
    return pl.pallas_call(
        functools.partial(_moe_out_kernel, alpha, tuple(firsts)),
        grid=(nblk,),
        in_specs=[pl.BlockSpec((1, tb, slab), lambda i: (i, 0, 0)),
                  pl.BlockSpec((1, slab, d), lambda i: (i, 0, 0))] + x_specs
                 + [pl.BlockSpec(c.shape, lambda i: (0, 0)) for c in consts],
        out_specs=x_specs,
        out_shape=[jax.ShapeDtypeStruct(x.shape, F32) for x in xs_in],
        compiler_params=_cparams("arbitrary"),
        name="moe_combine",
    )(pwt, ys.reshape(nblk, slab, d), *xs_in, *consts)


def _moe(xs_in, w_router_t, b_router, w_gate, w_up, w_down, ws_gate, ws_up, ws_down, ln_g, ln_b, alpha):
    xs, pwt, meta = _dispatch(xs_in, w_router_t, b_router)
    nblk, slab, d = xs.shape
    tile_e, n_valid, rows_tbl, n_tiles = _tile_schedule(meta, slab)
    ys = _grouped_experts(xs.reshape(nblk * slab, d), tile_e, n_valid, rows_tbl, n_tiles, w_gate, w_up, w_down)
    return _moe_out(pwt, ys, xs_in, ws_gate, ws_up, ws_down, ln_g, ln_b, alpha)


def _rope_tables(pos):
    half = HEAD_DIM // 2
    inv_freq = ROPE_THETA ** (-jnp.arange(half, dtype=F32) / half)
    ang = pos.astype(F32)[:, None] * inv_freq[None, :]
    cos, sin = jnp.cos(ang), jnp.sin(ang)
    return jnp.tile(jnp.concatenate([cos, cos], axis=1), (1, 2)), jnp.tile(jnp.concatenate([-sin, sin], axis=1), (1, 2))


def _block_diag(w):
    h, d, _ = w.shape
    out = jnp.zeros((h * d, h * d), w.dtype)
    for i in range(h):
        out = out.at[i * d:(i + 1) * d, i * d:(i + 1) * d].set(w[i])
    return out


def _pick(n, pref):
    return pref if n % pref == 0 else n


def kernel(x_prompt, x_sample, mem_prompt, cache_win128, cache_win512, cache_win2048, cache_mem_kv, state_conv, state_C, state_n, state_m, w_in, conv_w, conv_b, wq_m, wk_m, w_if, b_if, mh_norm_w, skip_m, w_out, ln1_g, ln1_b, w_cq, w_ck, w_cv, w_co, ln2_g, ln2_b, w_router, b_router, w_gate, w_up, w_down, ws_gate, ws_up, ws_down, ln3_g, ln3_b):
    depth = w_in.shape[0]
    assert depth == 1
    alpha = float((2 * depth) ** 0.25)
    bp, seq, d = x_prompt.shape
    bs, dec, _ = x_sample.shape
    assert seq % ATT_GROUPS[-1][0] == 0
    for c, (window, _) in zip((cache_win128, cache_win512, cache_win2048), ATT_GROUPS):
        assert c.shape[2] == window

    l = 0
    bf = lambda a: a.astype(BF16)
    w_in_b, w_out_b = bf(w_in[l]), bf(w_out[l])
    w_cq_b, w_ck_b, w_cv_b, w_co_b = bf(w_cq[l]), bf(w_ck[l]), bf(w_cv[l]), bf(w_co[l])
    wq_bd, wk_bd = bf(_block_diag(wq_m[l])), bf(_block_diag(wk_m[l]))
    w_router_t = bf(w_router[l].T)
    ws_gate_b, ws_up_b, ws_down_b = bf(ws_gate[l]), bf(ws_up[l]), bf(ws_down[l])

    def tail_of_layer(x1, qc, kv, batch, t, tm, tq):
        n = batch * t
        oc = _xattn(qc.reshape(batch, t, d), kv, tq=tq).reshape(n, d)
        return _xout(oc, x1, w_co_b, ln2_g[l], ln2_b[l], alpha, tm=tm)

    def mlstm(m3, batch, t, states, chunk, bb):
        return _mlstm(m3.reshape(batch, t, -1), *states, conv_w[l], conv_b[l], wq_bd, wk_bd, w_if[l], b_if[l],
                      mh_norm_w[l], skip_m[l], chunk=chunk, bb=bb)

    np_ = bp * seq
    xp = x_prompt.reshape(np_, d)
    cos_p, sin_p = _rope_tables(jnp.arange(seq))
    *qkv, m3, t128, t512, t2048 = _inproj_prompt(xp, w_in_b, cos_p, sin_p, seq=seq, tm=512)
    parts, lses = [], []
    for g in range(len(ATT_GROUPS)):
        o, lse = _win_attn(*qkv[3 * g:3 * g + 3])
        parts.append(o)
        lses.append(lse)
    zeros_p = [jnp.zeros((bp, CONV_WIDTH - 1, MLSTM_WIDTH), F32), jnp.zeros((bp, MLSTM_WIDTH, HEAD_DIM), F32),
               jnp.zeros((bp, MLSTM_HEADS, HEAD_DIM), F32), jnp.zeros((bp, 1, MLSTM_HEADS), F32)]
    hm_p, p_conv, p_c, p_n, p_m = mlstm(m3, bp, seq, zeros_p, 128, bp)
    x1, qc = _outproj(parts + lses, hm_p.reshape(np_, -1), xp, w_out_b, ln1_g[l], ln1_b[l], w_cq_b, alpha, tm=512)
    kv_p = _memkv(mem_prompt, w_ck_b, w_cv_b)
    x2_p = tail_of_layer(x1, qc, kv_p, bp, seq, 512, 512)

    ns = bs * dec
    xs = x_sample.reshape(ns, d)
    cos_s, sin_s = _rope_tables(jnp.tile(PAST_LEN + jnp.arange(dec), bs))
    qs, ks, vs, m3s = _inproj(xs, w_in_b, cos_s, sin_s, tm=ns)
    caches = [c[l].reshape(bs, c.shape[2], 2 * GROUP_WIDTH) for c in (cache_win128, cache_win512, cache_win2048)]
    att_s, s128, s512, s2048 = _dec_attn(qs.reshape(bs, dec, -1), ks.reshape(bs, dec, -1), vs.reshape(bs, dec, -1),
                                         caches)
    states_s = [state_conv[l], state_C[l].reshape(bs, MLSTM_WIDTH, HEAD_DIM), state_n[l],
                state_m[l].reshape(bs, 1, MLSTM_HEADS)]
    hm_s, s_conv, s_c, s_n, s_m = mlstm(m3s, bs, dec, states_s, dec, 4 if bs % 4 == 0 else 1)
    x1s, qcs = _outproj([att_s.reshape(ns, -1)], hm_s.reshape(ns, -1), xs, w_out_b, ln1_g[l], ln1_b[l], w_cq_b,
                        alpha, tm=ns)
    kv_s = cache_mem_kv[l].reshape(bs, cache_mem_kv.shape[2], 2 * d)
    x2_s = tail_of_layer(x1s, qcs, kv_s, bs, dec, ns, dec)

    y_p, y_s = _moe([x2_p, x2_s], w_router_t, b_router[l], w_gate[l], w_up[l], w_down[l],
                    ws_gate_b, ws_up_b, ws_down_b, ln3_g[l], ln3_b[l], alpha)

    win_shape = lambda a, b_: a.reshape(1, b_, a.shape[1], 2, HEADS_PER_GROUP, HEAD_DIM)
    return (y_p.reshape(bp, seq, d), y_s.reshape(bs, dec, d),
            win_shape(t128, bp), win_shape(t512, bp), win_shape(t2048, bp),
            kv_p.reshape(1, bp, mem_prompt.shape[1], 2, MEM_HEADS, d // MEM_HEADS),
            p_conv[None], p_c.reshape(1, bp, MLSTM_HEADS, HEAD_DIM, HEAD_DIM), p_n[None],
            p_m.reshape(1, bp, MLSTM_HEADS),
            win_shape(s128, bs), win_shape(s512, bs), win_shape(s2048, bs),
            s_conv[None], s_c.reshape(1, bs, MLSTM_HEADS, HEAD_DIM, HEAD_DIM), s_n[None],
            s_m.reshape(1, bs, MLSTM_HEADS))
```
